```python
import math
import jax, jax.numpy as jnp
from jax import lax
import numpy as np

D_MODEL = 1024
BATCH = 1
SEQ = 16384
DEPTH = 2
DEC_BATCH = 128
DEC_SEQ = 4
PAST_LEN = 16384
PAGE_SIZE = 128

BR_WIDTH = 256
N_BRANCH = 5
SSD_HEADS = 4
SSD_HEAD_DIM = 64
SSD_GROUPS = 2
SSD_STATE = 128
SSD_CONV = 4
SSD_CHUNK = 64
SSD_CONV_DIM = BR_WIDTH + 2 * SSD_GROUPS * SSD_STATE
DIFF_HEADS = 4
DIFF_KV_HEADS = 2
DIFF_DIM = 32
GLA_HEADS = 4
GLA_DK = 32
GLA_DV = 64
GLA_GATE_RANK = 16
GLA_TAU = 16.0
GLA_CHUNK = 64
MLA_HEADS = 4
MLA_Q_RANK = 256
MLA_KV_RANK = 128
MLA_NOPE = 64
MLA_ROPE = 32
MLA_V = 64
ROPE_THETA = 10000.0
MEM_LEN = 256
MEM_HEADS = 4
MEM_DIM = 64
FFN_DIM = 2816
FFN_CONV = 3
Q_BLOCK = 128
EPS = 1e-6

IN_WIDTHS = (
    BR_WIDTH,
    SSD_CONV_DIM,
    SSD_HEADS,
    DIFF_HEADS * 2 * DIFF_DIM,
    DIFF_KV_HEADS * 2 * DIFF_DIM,
    DIFF_KV_HEADS * 2 * DIFF_DIM,
    GLA_HEADS * GLA_DK,
    GLA_HEADS * GLA_DK,
    GLA_HEADS * GLA_DV,
    GLA_GATE_RANK,
    GLA_HEADS * GLA_DV,
    MLA_Q_RANK,
    MLA_KV_RANK,
    MLA_ROPE,
    MEM_HEADS * MEM_DIM,
    N_BRANCH * D_MODEL,
)
D_IN = sum(IN_WIDTHS)

kernel_name = 'hybrid_ssd_diff_gla_mla_mem_decoder_step'


def split_cols(u, widths):
    return jnp.split(u, np.cumsum(widths)[:-1].tolist(), axis=-1)


def rmsnorm(x, g):
    xf = x.astype(jnp.float32)
    y = xf * lax.rsqrt(jnp.mean(xf * xf, axis=-1, keepdims=True) + EPS)
    return (y * g.astype(jnp.float32)).astype(x.dtype)


def causal_dwconv(x, state, w, b):
    k = w.shape[0]
    t = x.shape[1]
    xp = jnp.concatenate([state.astype(x.dtype), x], axis=1)
    y = b
    for i in range(k):
        y = y + xp[:, i:i + t] * w[i]
    return y, xp[:, xp.shape[1] - (k - 1):]


def rope(x, pos):
    half = x.shape[-1] // 2
    freqs = jnp.power(ROPE_THETA, -jnp.arange(half, dtype=jnp.float32) / half)
    ang = pos.astype(jnp.float32)[:, None] * freqs
    shape = (1, pos.shape[0]) + (1,) * (x.ndim - 3) + (half,)
    cos = jnp.cos(ang).reshape(shape)
    sin = jnp.sin(ang).reshape(shape)
    xf = x.astype(jnp.float32)
    x1, x2 = xf[..., :half], xf[..., half:]
    return jnp.concatenate([x1 * cos - x2 * sin, x1 * sin + x2 * cos], axis=-1).astype(x.dtype)


def alibi_slopes(n):
    return jnp.power(2.0, -8.0 * jnp.arange(1, n + 1, dtype=jnp.float32) / n)


def sweep_queries(fn, q_args, q_pos):
    t = q_pos.shape[0]
    if t <= Q_BLOCK or t % Q_BLOCK:
        return fn(*q_args, q_pos)
    nb = t // Q_BLOCK

    def to_blocks(a):
        return jnp.moveaxis(a.reshape(a.shape[0], nb, Q_BLOCK, *a.shape[2:]), 1, 0)

    blocks = tuple(to_blocks(a) for a in q_args)
    out = lax.map(lambda xs: fn(*xs[0], xs[1]), (blocks, q_pos.reshape(nb, Q_BLOCK)))
    out = jnp.moveaxis(out, 0, 1)
    return out.reshape(out.shape[0], t, *out.shape[3:])


def ssd_scan(x, dt, a, bm, cm, h0):
    f32 = jnp.float32
    b, t, h, p = x.shape
    g = bm.shape[2]
    c = min(SSD_CHUNK, t)
    pad = (-t) % c
    x = x.astype(f32)
    dt = dt.astype(f32)
    bm = jnp.repeat(bm.astype(f32), h // g, axis=2)
    cm = jnp.repeat(cm.astype(f32), h // g, axis=2)
    if pad:
        pw = ((0, 0), (0, pad), (0, 0), (0, 0))
        x, bm, cm = jnp.pad(x, pw), jnp.pad(bm, pw), jnp.pad(cm, pw)
        dt = jnp.pad(dt, ((0, 0), (0, pad), (0, 0)))
    n = (t + pad) // c
    xdt = (x * dt[..., None]).reshape(b, n, c, h, p)
    bm = bm.reshape(b, n, c, h, -1)
    cm = cm.reshape(b, n, c, h, -1)
    cs = jnp.cumsum(jnp.moveaxis((dt * a).reshape(b, n, c, h), 3, 1), axis=-1)
    mask = jnp.tril(jnp.ones((c, c), bool))
    lmat = jnp.exp(jnp.where(mask, cs[..., :, None] - cs[..., None, :], -jnp.inf))
    y_diag = jnp.einsum('bclhn,bcshn,bhcls,bcshp->bclhp', cm, bm, lmat, xdt)
    states = jnp.einsum('bclhn,bhcl,bclhp->bchpn', bm, jnp.exp(cs[..., -1:] - cs), xdt)

    def step(s, inp):
        st, dec = inp
        return s * dec[..., None, None] + st, s

    h_fin, h_in = lax.scan(step, h0.astype(f32),
                           (jnp.moveaxis(states, 1, 0), jnp.moveaxis(jnp.exp(cs[..., -1]), 2, 0)))
    y_off = jnp.einsum('bclhn,cbhpn,bhcl->bclhp', cm, h_in, jnp.exp(cs))
    y = (y_diag + y_off).reshape(b, n * c, h, p)[:, :t]
    return y, h_fin


def ssd_branch(z, xbc, dt_raw, conv_state, h0, conv_w, conv_b, dt_bias, a_log, d_skip, norm_g):
    b, t, _ = z.shape
    f32 = jnp.float32
    xbc, conv_new = causal_dwconv(xbc, conv_state, conv_w, conv_b)
    xbc = jax.nn.silu(xbc)
    xs, bm, cm = split_cols(xbc, (BR_WIDTH, SSD_GROUPS * SSD_STATE, SSD_GROUPS * SSD_STATE))
    xs = xs.reshape(b, t, SSD_HEADS, SSD_HEAD_DIM)
    bm = bm.reshape(b, t, SSD_GROUPS, SSD_STATE)
    cm = cm.reshape(b, t, SSD_GROUPS, SSD_STATE)
    dt = jax.nn.softplus(dt_raw.astype(f32) + dt_bias.astype(f32))
    a = -jnp.exp(a_log.astype(f32))
    y, h_new = ssd_scan(xs, dt, a, bm, cm, h0)
    y = (y + xs.astype(f32) * d_skip.astype(f32)[:, None]).astype(z.dtype)
    y = rmsnorm(y.reshape(b, t, BR_WIDTH) * jax.nn.silu(z), norm_g)
    return y, conv_new, h_new.astype(h0.dtype)


def diff_branch(q, k, v, k_past, v_past, pos, kpos, lq1, lk1, lq2, lk2, norm_g, lam_init):
    f32 = jnp.float32
    b, t, _ = q.shape
    grp = DIFF_HEADS // DIFF_KV_HEADS
    q = q.reshape(b, t, DIFF_KV_HEADS, grp, 2, DIFF_DIM)
    k = k.reshape(b, t, DIFF_KV_HEADS, 2 * DIFF_DIM)
    v = v.reshape(b, t, DIFF_KV_HEADS, 2 * DIFF_DIM)
    k_all = k if k_past is None else jnp.concatenate([k_past.astype(k.dtype), k], axis=1)
    v_all = v if v_past is None else jnp.concatenate([v_past.astype(v.dtype), v], axis=1)
    k_maps = k_all.reshape(b, k_all.shape[1], DIFF_KV_HEADS, 2, DIFF_DIM)
    lam = (jnp.exp(jnp.sum(lq1.astype(f32) * lk1.astype(f32)))
           - jnp.exp(jnp.sum(lq2.astype(f32) * lk2.astype(f32))) + lam_init)
    slopes = alibi_slopes(DIFF_HEADS).reshape(DIFF_KV_HEADS, grp, 1, 1, 1)
    scale = DIFF_DIM ** -0.5

    def block(qb, qp):
        s = jnp.einsum('btkgid,bskid->bkgits', qb, k_maps, preferred_element_type=f32) * scale
        dist = (qp[:, None] - kpos[None, :]).astype(f32)
        s = jnp.where(dist >= 0, s - slopes * dist, -jnp.inf)
        p = jax.nn.softmax(s, axis=-1)
        w = (p[:, :, :, 0] - lam * p[:, :, :, 1]).astype(v_all.dtype)
        return jnp.einsum('bkgts,bskv->btkgv', w, v_all)

    o = sweep_queries(block, (q,), pos).reshape(b, t, DIFF_HEADS, 2 * DIFF_DIM)
    o = rmsnorm(o, norm_g) * (1.0 - lam_init)
    return o.reshape(b, t, BR_WIDTH), k, v


def gla_scan(q, k, v, log_a, s0):
    f32 = jnp.float32
    b, t, h, _ = q.shape
    c = min(GLA_CHUNK, t)
    pad = (-t) % c

    def prep(a):
        a = a.astype(f32)
        if pad:
            a = jnp.pad(a, ((0, 0), (0, pad), (0, 0), (0, 0)))
        return a.reshape(b, a.shape[1] // c, c, h, a.shape[-1]).transpose(1, 0, 3, 2, 4)

    mask = jnp.tril(jnp.ones((c, c), bool))[:, :, None]

    def step(s, inp):
        qc, kc, vc, gc = inp
        bc = jnp.cumsum(gc, axis=2)
        o_inter = jnp.einsum('bhtk,bhkv->bhtv', qc * jnp.exp(bc), s)
        dec = jnp.exp(jnp.where(mask, bc[:, :, :, None] - bc[:, :, None], -jnp.inf))
        att = jnp.einsum('bhtk,bhsk,bhtsk->bhts', qc, kc, dec)
        o_intra = jnp.einsum('bhts,bhsv->bhtv', att, vc)
        b_last = bc[:, :, -1:]
        s_new = (s * jnp.exp(b_last[:, :, 0])[..., None]
                 + jnp.einsum('bhsk,bhsv->bhkv', kc * jnp.exp(b_last - bc), vc))
        return s_new, o_inter + o_intra

    s_fin, o = lax.scan(step, s0.astype(f32), (prep(q), prep(k), prep(v), prep(log_a)))
    o = o.transpose(1, 0, 3, 2, 4).reshape(b, -1, h, v.shape[-1])[:, :t]
    return o, s_fin


def gla_branch(q, k, v, g1, r, s0, w_g2, b_g, norm_g):
    b, t, _ = q.shape
    q = q.reshape(b, t, GLA_HEADS, GLA_DK) * (GLA_DK ** -0.5)
    k = k.reshape(b, t, GLA_HEADS, GLA_DK)
    v = v.reshape(b, t, GLA_HEADS, GLA_DV)
    log_a = jax.nn.log_sigmoid((g1 @ w_g2 + b_g).astype(jnp.float32)) / GLA_TAU
    o, s = gla_scan(q, k, v, log_a.reshape(b, t, GLA_HEADS, GLA_DK), s0)
    o = rmsnorm(o, norm_g).astype(r.dtype).reshape(b, t, BR_WIDTH) * jax.nn.silu(r)
    return o, s.astype(s0.dtype)


def mla_branch(cq, ckv, kr, c_past, kr_past, pos, kpos, q_norm, kv_norm, w_uq, w_uk, w_uv):
    f32 = jnp.float32
    b, t, _ = cq.shape
    q = (rmsnorm(cq, q_norm) @ w_uq).reshape(b, t, MLA_HEADS, MLA_NOPE + MLA_ROPE)
    q_nope = q[..., :MLA_NOPE]
    q_rope = rope(q[..., MLA_NOPE:], pos)
    c = rmsnorm(ckv, kv_norm)
    kr = rope(kr, pos)
    q_lat = jnp.einsum('bthn,chn->bthc', q_nope, w_uk)
    c_all = c if c_past is None else jnp.concatenate([c_past.astype(c.dtype), c], axis=1)
    kr_all = kr if kr_past is None else jnp.concatenate([kr_past.astype(kr.dtype), kr], axis=1)
    scale = (MLA_NOPE + MLA_ROPE) ** -0.5

    def block(qlb, qrb, qp):
        s = (jnp.einsum('bthc,bsc->bhts', qlb, c_all, preferred_element_type=f32)
             + jnp.einsum('bthr,bsr->bhts', qrb, kr_all, preferred_element_type=f32)) * scale
        s = jnp.where(qp[:, None] >= kpos[None, :], s, -jnp.inf)
        p = jax.nn.softmax(s, axis=-1).astype(c_all.dtype)
        return jnp.einsum('bhts,bsc->bthc', p, c_all)

    o_lat = sweep_queries(block, (q_lat, q_rope), pos)
    o = jnp.einsum('bthc,chv->bthv', o_lat, w_uv)
    return o.reshape(b, t, BR_WIDTH), c, kr


def mem_branch(q, mk, mv):
    b, t, _ = q.shape
    q = q.reshape(b, t, MEM_HEADS, MEM_DIM)
    s = jnp.einsum('bthd,bshd->bhts', q, mk.astype(q.dtype), preferred_element_type=jnp.float32) * (MEM_DIM ** -0.5)
    p = jax.nn.softmax(s, axis=-1).astype(q.dtype)
    return jnp.einsum('bhts,bshd->bthd', p, mv.astype(q.dtype)).reshape(b, t, BR_WIDTH)


def conv_ffn(h, state, w_in, conv_w, conv_b, w_out):
    a, g = jnp.split(h @ w_in, 2, axis=-1)
    a, state_new = causal_dwconv(a, state, conv_w, conv_b)
    return (jax.nn.gelu(a) * g) @ w_out, state_new


def gather_pages(cache, li, page_table):
    g = cache[li, page_table]
    return g.reshape(page_table.shape[0], page_table.shape[1] * PAGE_SIZE, *cache.shape[3:])


def stk(outs, j):
    return jnp.stack([o[j] for o in outs])


def trunk_layer(x, past_len, li, P, st):
    b, t, _ = x.shape
    kpos = jnp.arange(past_len + t, dtype=jnp.int32)
    pos = kpos[past_len:]
    h = rmsnorm(x, P['g_pre_mix'][li])
    (z, xbc, dt_raw, dq, dk, dv, gq, gk, gv, gg, gr, cq, ckv, kr, mq, gates) = split_cols(h @ P['w_in'][li], IN_WIDTHS)
    y_ssd, ssd_conv, ssd_h = ssd_branch(z, xbc, dt_raw, st['ssd_conv'], st['ssd_h'], P['ssd_conv_w'][li],
                                        P['ssd_conv_b'][li], P['ssd_dt_bias'][li], P['ssd_a_log'][li],
                                        P['ssd_d'][li], P['ssd_norm'][li])
    lam_init = 0.8 - 0.6 * math.exp(-0.3 * li)
    y_diff, k_rows, v_rows = diff_branch(dq, dk, dv, st['past_dk'], st['past_dv'], pos, kpos,
                                         P['diff_lq1'][li], P['diff_lk1'][li], P['diff_lq2'][li],
                                         P['diff_lk2'][li], P['diff_norm'][li], lam_init)
    y_gla, gla_s = gla_branch(gq, gk, gv, gg, gr, st['gla_s'], P['gla_wg2'][li], P['gla_bg'][li], P['gla_norm'][li])
    y_mla, c_rows, kr_rows = mla_branch(cq, ckv, kr, st['past_ckv'], st['past_kr'], pos, kpos,
                                        P['mla_q_norm'][li], P['mla_kv_norm'][li], P['mla_wuq'][li],
                                        P['mla_wuk'][li], P['mla_wuv'][li])
    y_mem = mem_branch(mq, st['mem_k'], st['mem_v'])
    gates = jax.nn.sigmoid(gates.reshape(b, t, N_BRANCH, D_MODEL))
    w_branch = P['w_branch'][li]
    merged = sum(gates[:, :, i] * (y @ w_branch[i])
                 for i, y in enumerate((y_ssd, y_diff, y_gla, y_mla, y_mem)))
    x = x + rmsnorm(merged @ P['w_out'][li], P['g_post_mix'][li])
    h2 = rmsnorm(x, P['g_pre_ffn'][li])
    f, ffn_conv = conv_ffn(h2, st['ffn_conv'], P['w_ffn_in'][li], P['ffn_conv_w'][li],
                           P['ffn_conv_b'][li], P['w_ffn_out'][li])
    x = x + rmsnorm(f, P['g_post_ffn'][li])
    return x, (k_rows, v_rows, c_rows, kr_rows, ssd_h, ssd_conv, gla_s, ffn_conv)


def setup_inputs(seed: int = 0) -> dict:
    key = jax.random.key(seed)
    ks = iter(jax.random.split(key, 64))
    f32 = jnp.float32

    def nrm(shape, scale=1.0):
        return scale * jax.random.normal(next(ks), shape, f32)

    def gain(shape):
        return 1.0 + 0.05 * nrm(shape)

    n_pages = PAST_LEN // PAGE_SIZE
    n_used = DEC_BATCH * n_pages
    n_pool = n_used + max(1, n_used // 4)
    page_table = jax.random.permutation(next(ks), n_pool)[:n_used].reshape(DEC_BATCH, n_pages).astype(jnp.int32)
    dt0 = jnp.exp(jax.random.uniform(next(ks), (DEPTH, SSD_HEADS), f32, math.log(1e-3), math.log(1e-1)))
    a0 = jax.random.uniform(next(ks), (DEPTH, SSD_HEADS), f32, 1.0, 16.0)
    return {
        'x_prompt': nrm((BATCH, SEQ, D_MODEL)),
        'x_sample': nrm((DEC_BATCH, DEC_SEQ, D_MODEL)),
        'cache_diff_k': nrm((DEPTH, n_pool, PAGE_SIZE, DIFF_KV_HEADS, 2 * DIFF_DIM)),
        'cache_diff_v': nrm((DEPTH, n_pool, PAGE_SIZE, DIFF_KV_HEADS, 2 * DIFF_DIM)),
        'cache_mla_ckv': nrm((DEPTH, n_pool, PAGE_SIZE, MLA_KV_RANK)),
        'cache_mla_krope': nrm((DEPTH, n_pool, PAGE_SIZE, MLA_ROPE)),
        'cache_mem_k': nrm((DEPTH, DEC_BATCH, MEM_LEN, MEM_HEADS, MEM_DIM)),
        'cache_mem_v': nrm((DEPTH, DEC_BATCH, MEM_LEN, MEM_HEADS, MEM_DIM)),
        'state_ssd': nrm((DEPTH, DEC_BATCH, SSD_HEADS, SSD_HEAD_DIM, SSD_STATE), 0.5),
        'state_ssd_conv': nrm((DEPTH, DEC_BATCH, SSD_CONV - 1, SSD_CONV_DIM)),
        'state_gla': nrm((DEPTH, DEC_BATCH, GLA_HEADS, GLA_DK, GLA_DV), 0.5),
        'state_ffn_conv': nrm((DEPTH, DEC_BATCH, FFN_CONV - 1, FFN_DIM)),
        'page_table': page_table,
        'mem_prompt': nrm((BATCH, MEM_LEN, D_MODEL)),
        'w_in': nrm((DEPTH, D_MODEL, D_IN), D_MODEL ** -0.5),
        'ssd_conv_w': nrm((DEPTH, SSD_CONV, SSD_CONV_DIM), SSD_CONV ** -0.5),
        'ssd_conv_b': nrm((DEPTH, SSD_CONV_DIM), 0.02),
        'ssd_dt_bias': dt0 + jnp.log(-jnp.expm1(-dt0)),
        'ssd_a_log': jnp.log(a0),
        'ssd_d': gain((DEPTH, SSD_HEADS)),
        'ssd_norm': gain((DEPTH, BR_WIDTH)),
        'diff_lq1': nrm((DEPTH, DIFF_DIM), 0.1),
        'diff_lk1': nrm((DEPTH, DIFF_DIM), 0.1),
        'diff_lq2': nrm((DEPTH, DIFF_DIM), 0.1),
        'diff_lk2': nrm((DEPTH, DIFF_DIM), 0.1),
        'diff_norm': gain((DEPTH, 2 * DIFF_DIM)),
        'gla_wg2': nrm((DEPTH, GLA_GATE_RANK, GLA_HEADS * GLA_DK), GLA_GATE_RANK ** -0.5),
        'gla_bg': nrm((DEPTH, GLA_HEADS * GLA_DK), 0.1),
        'gla_norm': gain((DEPTH, GLA_DV)),
        'mla_q_norm': gain((DEPTH, MLA_Q_RANK)),
        'mla_kv_norm': gain((DEPTH, MLA_KV_RANK)),
        'mla_wuq': nrm((DEPTH, MLA_Q_RANK, MLA_HEADS * (MLA_NOPE + MLA_ROPE)), MLA_Q_RANK ** -0.5),
        'mla_wuk': nrm((DEPTH, MLA_KV_RANK, MLA_HEADS, MLA_NOPE), MLA_KV_RANK ** -0.5),
        'mla_wuv': nrm((DEPTH, MLA_KV_RANK, MLA_HEADS, MLA_V), MLA_KV_RANK ** -0.5),
        'w_mem_k': nrm((DEPTH, D_MODEL, MEM_HEADS * MEM_DIM), D_MODEL ** -0.5),
        'w_mem_v': nrm((DEPTH, D_MODEL, MEM_HEADS * MEM_DIM), D_MODEL ** -0.5),
        'w_branch': nrm((DEPTH, N_BRANCH, BR_WIDTH, D_MODEL), BR_WIDTH ** -0.5),
        'w_out': nrm((DEPTH, D_MODEL, D_MODEL), D_MODEL ** -0.5),
        'g_pre_mix': gain((DEPTH, D_MODEL)),
        'g_post_mix': gain((DEPTH, D_MODEL)),
        'g_pre_ffn': gain((DEPTH, D_MODEL)),
        'g_post_ffn': gain((DEPTH, D_MODEL)),
        'w_ffn_in': nrm((DEPTH, D_MODEL, 2 * FFN_DIM), D_MODEL ** -0.5),
        'ffn_conv_w': nrm((DEPTH, FFN_CONV, FFN_DIM), FFN_CONV ** -0.5),
        'ffn_conv_b': nrm((DEPTH, FFN_DIM), 0.02),
        'w_ffn_out': nrm((DEPTH, FFN_DIM, D_MODEL), FFN_DIM ** -0.5),
    }


def reference(x_prompt, x_sample, cache_diff_k, cache_diff_v, cache_mla_ckv, cache_mla_krope,
              cache_mem_k, cache_mem_v, state_ssd, state_ssd_conv, state_gla, state_ffn_conv,
              page_table, mem_prompt, w_in, ssd_conv_w, ssd_conv_b, ssd_dt_bias, ssd_a_log, ssd_d,
              ssd_norm, diff_lq1, diff_lk1, diff_lq2, diff_lk2, diff_norm, gla_wg2, gla_bg, gla_norm,
              mla_q_norm, mla_kv_norm, mla_wuq, mla_wuk, mla_wuv, w_mem_k, w_mem_v, w_branch, w_out,
              g_pre_mix, g_post_mix, g_pre_ffn, g_post_ffn, w_ffn_in, ffn_conv_w, ffn_conv_b, w_ffn_out):
    P = dict(w_in=w_in, ssd_conv_w=ssd_conv_w, ssd_conv_b=ssd_conv_b, ssd_dt_bias=ssd_dt_bias,
             ssd_a_log=ssd_a_log, ssd_d=ssd_d, ssd_norm=ssd_norm, diff_lq1=diff_lq1, diff_lk1=diff_lk1,
             diff_lq2=diff_lq2, diff_lk2=diff_lk2, diff_norm=diff_norm, gla_wg2=gla_wg2, gla_bg=gla_bg,
             gla_norm=gla_norm, mla_q_norm=mla_q_norm, mla_kv_norm=mla_kv_norm, mla_wuq=mla_wuq,
             mla_wuk=mla_wuk, mla_wuv=mla_wuv, w_branch=w_branch, w_out=w_out, g_pre_mix=g_pre_mix,
             g_post_mix=g_post_mix, g_pre_ffn=g_pre_ffn, g_post_ffn=g_post_ffn, w_ffn_in=w_ffn_in,
             ffn_conv_w=ffn_conv_w, ffn_conv_b=ffn_conv_b, w_ffn_out=w_ffn_out)
    b = x_prompt.shape[0]
    mem_len = mem_prompt.shape[1]
    past_len = page_table.shape[1] * PAGE_SIZE
    dtp = x_prompt.dtype
    yp, ys = x_prompt, x_sample
    new_p, new_s, mem_kp, mem_vp = [], [], [], []
    for li in range(DEPTH):
        mk = (mem_prompt @ w_mem_k[li]).reshape(b, mem_len, MEM_HEADS, MEM_DIM)
        mv = (mem_prompt @ w_mem_v[li]).reshape(b, mem_len, MEM_HEADS, MEM_DIM)
        mem_kp.append(mk)
        mem_vp.append(mv)
        st_p = {
            'ssd_conv': jnp.zeros((b, SSD_CONV - 1, SSD_CONV_DIM), dtp),
            'ssd_h': jnp.zeros((b, SSD_HEADS, SSD_HEAD_DIM, SSD_STATE), dtp),
            'gla_s': jnp.zeros((b, GLA_HEADS, GLA_DK, GLA_DV), dtp),
            'ffn_conv': jnp.zeros((b, FFN_CONV - 1, FFN_DIM), dtp),
            'mem_k': mk, 'mem_v': mv,
            'past_dk': None, 'past_dv': None, 'past_ckv': None, 'past_kr': None,
        }
        yp, out_p = trunk_layer(yp, 0, li, P, st_p)
        st_s = {
            'ssd_conv': state_ssd_conv[li],
            'ssd_h': state_ssd[li],
            'gla_s': state_gla[li],
            'ffn_conv': state_ffn_conv[li],
            'mem_k': cache_mem_k[li], 'mem_v': cache_mem_v[li],
            'past_dk': gather_pages(cache_diff_k, li, page_table),
            'past_dv': gather_pages(cache_diff_v, li, page_table),
            'past_ckv': gather_pages(cache_mla_ckv, li, page_table),
            'past_kr': gather_pages(cache_mla_krope, li, page_table),
        }
        ys, out_s = trunk_layer(ys, past_len, li, P, st_s)
        new_p.append(out_p)
        new_s.append(out_s)
    return (yp, ys,
            stk(new_p, 0), stk(new_p, 1), stk(new_p, 2), stk(new_p, 3),
            jnp.stack(mem_kp), jnp.stack(mem_vp),
            stk(new_p, 4), stk(new_p, 5), stk(new_p, 6), stk(new_p, 7),
            stk(new_s, 0), stk(new_s, 1), stk(new_s, 2), stk(new_s, 3),
            stk(new_s, 4), stk(new_s, 5), stk(new_s, 6), stk(new_s, 7))
```

```python
import functools
import math

import numpy as np
import jax
import jax.numpy as jnp
from jax import lax
from jax.experimental import pallas as pl
from jax.experimental.pallas import tpu as pltpu

F32 = jnp.float32
BF16 = jnp.bfloat16

BR = 256
N_BRANCH = 5
SSD_HEADS, SSD_HEAD_DIM, SSD_GROUPS, SSD_STATE, SSD_CONV = 4, 64, 2, 128, 4
SSD_CONV_DIM = BR + 2 * SSD_GROUPS * SSD_STATE
DIFF_HEADS, DIFF_KV, DIFF_DIM = 4, 2, 32
GLA_HEADS, GLA_DK, GLA_DV, GLA_RANK, GLA_TAU = 4, 32, 64, 16, 16.0
MLA_HEADS, MLA_QR, MLA_KVR, MLA_NOPE, MLA_ROPE, MLA_V = 4, 256, 128, 64, 32, 64
ROPE_THETA = 10000.0
MEM_HEADS, MEM_DIM = 4, 64
FFN_CONV = 3
PAGE = 128
EPS = 1e-6
NEG = -1e30

IN_WIDTHS = (BR, SSD_CONV_DIM, SSD_HEADS, DIFF_HEADS * 2 * DIFF_DIM, DIFF_KV * 2 * DIFF_DIM,
             DIFF_KV * 2 * DIFF_DIM, GLA_HEADS * GLA_DK, GLA_HEADS * GLA_DK, GLA_HEADS * GLA_DV,
             GLA_RANK, GLA_HEADS * GLA_DV, MLA_QR, MLA_KVR, MLA_ROPE, MEM_HEADS * MEM_DIM)
PROJ_WIDTHS = (256, 768, 128, 256, 128, 128, 128, 128, 256, 128, 256, 256, 128, 128, 256)

LANES = 128
SUBLANES = 8
VMEM_LIMIT = 56 * 1024 * 1024


def _cparams(sem):
    return pltpu.CompilerParams(dimension_semantics=sem, vmem_limit_bytes=VMEM_LIMIT)


def _dot(a, b):
    return jnp.dot(a, b, preferred_element_type=F32)


def _dot_nt(a, b):
    return lax.dot_general(a, b, (((1,), (1,)), ((), ())), preferred_element_type=F32)


def _dot_tn(a, b):
    return lax.dot_general(a, b, (((0,), (0,)), ((), ())), preferred_element_type=F32)


def _split3(x):
    hi = x.astype(BF16)
    r = x - hi.astype(F32)
    mid = r.astype(BF16)
    lo = (r - mid.astype(F32)).astype(BF16)
    return hi, mid, lo


def _xdot_l(m01, x):
    hi, mid, lo = _split3(x)
    return _dot(m01, hi) + _dot(m01, mid) + _dot(m01, lo)


def _xdot_r(x, m01):
    hi, mid, lo = _split3(x)
    return _dot(hi, m01) + _dot(mid, m01) + _dot(lo, m01)


def _xdot_nt(m01, x):
    hi, mid, lo = _split3(x)
    return _dot_nt(m01, hi) + _dot_nt(m01, mid) + _dot_nt(m01, lo)


def _rms(x, g):
    return x * lax.rsqrt(jnp.mean(x * x, axis=-1, keepdims=True) + EPS) * g


def _silu(x):
    return x * (1.0 / (1.0 + jnp.exp(-x)))


def _sigmoid(x):
    return 1.0 / (1.0 + jnp.exp(-x))


def _softplus(x):
    return jnp.maximum(x, 0.0) + jnp.log(1.0 + jnp.exp(-jnp.abs(x)))


def _log_sigmoid(x):
    return -_softplus(-x)


def _gelu_tanh(x):
    c = math.sqrt(2.0 / math.pi)
    return 0.5 * x * (1.0 + jnp.tanh(c * (x + 0.044715 * (x * x * x))))


def _lane_iota(shape):
    return lax.broadcasted_iota(jnp.int32, shape, len(shape) - 1)


def _row_iota(shape):
    return lax.broadcasted_iota(jnp.int32, shape, len(shape) - 2)


def _proj_in_kernel(x_ref, g_ref, w_ref, *o_refs):
    xn = _rms(x_ref[...], g_ref[...]).astype(BF16)
    off = 0
    for o_ref in o_refs:
        wd = o_ref.shape[1]
        o_ref[...] = _dot(xn, w_ref[:, off:off + wd])
        off += wd


def _proj_in(x, g, w, tm):
    n, d = x.shape
    return pl.pallas_call(
        _proj_in_kernel,
        grid=(n // tm,),
        in_specs=[pl.BlockSpec((tm, d), lambda i: (i, 0)),
                  pl.BlockSpec((1, d), lambda i: (0, 0)),
                  pl.BlockSpec(w.shape, lambda i: (0, 0))],
        out_specs=[pl.BlockSpec((tm, wd), lambda i: (i, 0)) for wd in PROJ_WIDTHS],
        out_shape=[jax.ShapeDtypeStruct((n, wd), F32) for wd in PROJ_WIDTHS],
        compiler_params=_cparams(("arbitrary",)),
        name="proj_in",
    )(x, g, w)


def _conv_taps(ext_ref, cur, w_ref, b_ref, taps, shift, pad, tm):
    acc = b_ref[...] + w_ref[taps - 1:taps, :] * cur
    for j in range(taps - 1):
        off = pad - (taps - 1 - j) * shift
        acc = acc + w_ref[j:j + 1, :] * ext_ref[off:off + tm, :]
    return acc


def _conv_silu_kernel(x_ref, st_ref, w_ref, b_ref, y_ref, ns_ref, ext_ref, *, taps, shift, pad):
    tm = x_ref.shape[0]

    @pl.when(pl.program_id(0) == 0)
    def _():
        ext_ref[0:pad, :] = st_ref[...]

    cur = x_ref[...]
    ext_ref[pad:pad + tm, :] = cur
    y_ref[...] = _silu(_conv_taps(ext_ref, cur, w_ref, b_ref, taps, shift, pad, tm))
    tail = ext_ref[tm:tm + pad, :]
    ns_ref[...] = tail
    ext_ref[0:pad, :] = tail


def _conv_silu(x, state, w, b, shift, tm):
    n, c = x.shape
    taps = w.shape[0]
    pad = state.shape[0]
    assert tm >= pad and pad >= (taps - 1) * shift
    return pl.pallas_call(
        functools.partial(_conv_silu_kernel, taps=taps, shift=shift, pad=pad),
        grid=(n // tm,),
        in_specs=[pl.BlockSpec((tm, c), lambda i: (i, 0)),
                  pl.BlockSpec((pad, c), lambda i: (0, 0)),
                  pl.BlockSpec((taps, c), lambda i: (0, 0)),
                  pl.BlockSpec((1, c), lambda i: (0, 0))],
        out_specs=[pl.BlockSpec((tm, c), lambda i: (i, 0)),
                   pl.BlockSpec((pad, c), lambda i: (0, 0))],
        out_shape=[jax.ShapeDtypeStruct((n, c), F32), jax.ShapeDtypeStruct((pad, c), F32)],
        scratch_shapes=[pltpu.VMEM((pad + tm, c), F32)],
        compiler_params=_cparams(("arbitrary",)),
        name="conv_silu",
    )(x, state, w, b)


def _ssd_kernel(z_ref, xbc_ref, dt_ref, h0_ref, prm_ref, dskip_ref, ng_ref, tri_ref, exp_ref, eye_ref,
                y_ref, hout_ref, st_ref, *, chunk, t_valid, n_chunks):
    L = chunk
    c = pl.program_id(1)

    @pl.when(c == 0)
    def _():
        st_ref[...] = h0_ref[0].reshape(SSD_HEADS * SSD_HEAD_DIM, SSD_STATE).T

    xbc = xbc_ref[0]
    xs = xbc[:, 0:BR]
    bm = xbc[:, BR:2 * BR].astype(BF16)
    cm = xbc[:, 2 * BR:3 * BR].astype(BF16)
    dt = _softplus(dt_ref[0] + prm_ref[0:1, :])
    if t_valid < L:
        dt = jnp.where(_row_iota((L, LANES)) < t_valid, dt, 0.0)
    d_a = dt * prm_ref[1:2, :]
    tri = tri_ref[...]
    expand = exp_ref[...]
    cs = _xdot_l(tri, d_a)
    cs_exp = _xdot_r(cs, expand)
    dt_exp = _xdot_r(dt, expand)
    cs_t = _xdot_nt(eye_ref[...], cs)
    xdt = xs * dt_exp
    xdt_b = xdt.astype(BF16)
    causal = _row_iota((L, L)) >= _lane_iota((L, L))
    lane_head = _lane_iota((L, BR)) // SSD_HEAD_DIM
    st = st_ref[...]
    st_b = st.astype(BF16)
    y = xs * dskip_ref[...]
    for g in range(SSD_GROUPS):
        cg = cm[:, g * SSD_STATE:(g + 1) * SSD_STATE]
        bg = bm[:, g * SSD_STATE:(g + 1) * SSD_STATE]
        cb = _dot_nt(cg, bg)
        for hh in range(SSD_HEADS // SSD_GROUPS):
            h = g * (SSD_HEADS // SSD_GROUPS) + hh
            dec = jnp.exp(jnp.where(causal, cs[:, h:h + 1] - cs_t[h:h + 1, :], NEG))
            yd = _dot((cb * dec).astype(BF16), xdt_b)
            y = y + jnp.where(lane_head == h, yd, 0.0)
    half = _lane_iota((L, BR)) < (BR // 2)
    y_off = jnp.where(half, _dot(cm[:, 0:SSD_STATE], st_b), _dot(cm[:, SSD_STATE:2 * SSD_STATE], st_b))
    y = y + y_off * jnp.exp(cs_exp)
    cs_last = cs_exp[L - 1:L, :]
    xw = (xdt * jnp.exp(cs_last - cs_exp)).astype(BF16)
    half_s = _lane_iota((SSD_STATE, BR)) < (BR // 2)
    upd = jnp.where(half_s, _dot_tn(bm[:, 0:SSD_STATE], xw), _dot_tn(bm[:, SSD_STATE:2 * SSD_STATE], xw))
    st_new = st * jnp.exp(cs_last) + upd
    st_ref[...] = st_new
    y_ref[0] = _rms(y * _silu(z_ref[0]), ng_ref[...])

    @pl.when(c == n_chunks - 1)
    def _():
        hout_ref[0] = st_new.T.reshape(SSD_HEADS, SSD_HEAD_DIM, SSD_STATE)


def _ssd_scan(z, xbc, dt_raw, h0, prm, dskip, ng, chunk, t_valid):
    b, t, _ = z.shape
    n_chunks = t // chunk
    tri = jnp.asarray(np.tril(np.ones((chunk, chunk), np.float32)), BF16)
    expand = np.zeros((LANES, BR), np.float32)
    for h in range(SSD_HEADS):
        expand[h, h * SSD_HEAD_DIM:(h + 1) * SSD_HEAD_DIM] = 1.0
    eye = np.eye(SUBLANES, LANES, dtype=np.float32)
    row = lambda bi, ci: (bi, ci, 0)
    fixed2 = lambda bi, ci: (0, 0)
    return pl.pallas_call(
        functools.partial(_ssd_kernel, chunk=chunk, t_valid=t_valid, n_chunks=n_chunks),
        grid=(b, n_chunks),
        in_specs=[pl.BlockSpec((1, chunk, BR), row),
                  pl.BlockSpec((1, chunk, SSD_CONV_DIM), row),
                  pl.BlockSpec((1, chunk, LANES), row),
                  pl.BlockSpec((1, SSD_HEADS, SSD_HEAD_DIM, SSD_STATE), lambda bi, ci: (bi, 0, 0, 0)),
                  pl.BlockSpec((SUBLANES, LANES), fixed2),
                  pl.BlockSpec((1, BR), fixed2),
                  pl.BlockSpec((1, BR), fixed2),
                  pl.BlockSpec((chunk, chunk), fixed2),
                  pl.BlockSpec((LANES, BR), fixed2),
                  pl.BlockSpec((SUBLANES, LANES), fixed2)],
        out_specs=[pl.BlockSpec((1, chunk, BR), row),
                   pl.BlockSpec((1, SSD_HEADS, SSD_HEAD_DIM, SSD_STATE), lambda bi, ci: (bi, 0, 0, 0))],
        out_shape=[jax.ShapeDtypeStruct((b, t, BR), F32),
                   jax.ShapeDtypeStruct((b, SSD_HEADS, SSD_HEAD_DIM, SSD_STATE), F32)],
        scratch_shapes=[pltpu.VMEM((SSD_STATE, BR), F32)],
        compiler_params=_cparams(("arbitrary", "arbitrary")),
        name="ssd_scan",
    )(z, xbc, dt_raw, h0, prm, dskip, ng, tri, jnp.asarray(expand, BF16), jnp.asarray(eye, BF16))


def _pad_lanes(v, width):
    return jnp.pad(v, [(0, 0)] * (v.ndim - 1) + [(0, width - v.shape[-1])])


def _to_seq_major(a, db, steps, pad_to):
    c = a.shape[-1]
    a = a.reshape(steps, db, c).transpose(1, 0, 2)
    return jnp.pad(a, ((0, 0), (0, pad_to - steps), (0, 0)))


def _to_time_major(a, steps):
    db, _, c = a.shape
    return a[:, :steps].transpose(1, 0, 2).reshape(steps * db, c)


def _ssd_branch(z, xbc_raw, dt_raw, conv_state, h0, lp, db, steps):
    n = z.shape[0]
    tm = min(n, 512)
    xbc, conv_new = _conv_silu(xbc_raw, conv_state, lp["ssd_conv_w"], lp["ssd_conv_b"], shift=db, tm=tm)
    if db == 1:
        chunk = min(steps, 128)
        y, h_new = _ssd_scan(z[None], xbc[None], dt_raw[None], h0, lp["ssd_prm"], lp["ssd_dskip"],
                             lp["ssd_norm"], chunk, chunk)
        return y[0], conv_new, h_new
    zs, xs, ds = (_to_seq_major(a, db, steps, SUBLANES) for a in (z, xbc, dt_raw))
    y, h_new = _ssd_scan(zs, xs, ds, h0, lp["ssd_prm"], lp["ssd_dskip"], lp["ssd_norm"], SUBLANES, steps)
    return _to_time_major(y, steps), conv_new, h_new


def _gla_consts(chunk):
    L = chunk
    nl = int(math.log2(L))
    assert 2 ** nl == L
    t = np.arange(L)
    mats, masks = [], []
    for lvl in range(nl):
        b = L >> (lvl + 1)
        same = (t[:, None] // b) == (t[None, :] // b)
        mats.append((same & (t[None, :] <= t[:, None])).astype(np.float32))
    for lvl in range(nl):
        b = L >> (lvl + 1)
        same = (t[:, None] // b) == (t[None, :] // b)
        mats.append((same & (t[None, :] > t[:, None])).astype(np.float32))
    mats.append((t[None, :] <= t[:, None]).astype(np.float32))
    mats.append((t[None, :] > t[:, None]).astype(np.float32))
    for lvl in range(nl):
        b = L >> (lvl + 1)
        tb, sb = t[:, None] // b, t[None, :] // b
        masks.append(((tb % 2 == 1) & (sb == tb - 1)).astype(np.float32))
    masks.append((t[:, None] == t[None, :]).astype(np.float32))
    masks = np.stack([np.tile(m, (GLA_HEADS, 1)) for m in masks])
    ones = np.zeros((BR, BR), np.float32)
    for h in range(GLA_HEADS):
        ones[h * GLA_DV:(h + 1) * GLA_DV, h * GLA_DV:(h + 1) * GLA_DV] = 1.0
    return (jnp.asarray(np.concatenate(mats, 0), BF16), jnp.asarray(masks, F32), jnp.asarray(ones, BF16), nl)


def _gla_kernel(q_ref, k_ref, v_ref, gg_ref, r_ref, s0_ref, wg2_ref, bg_ref, ng_ref, mall_ref, masks_ref,
                ones_ref, y_ref, sout_ref, st_ref, *, chunk, t_valid, n_chunks, nl):
    L = chunk
    c = pl.program_id(1)

    @pl.when(c == 0)
    def _():
        st_ref[...] = s0_ref[0]

    q = q_ref[0] * (GLA_DK ** -0.5)
    k = k_ref[0]
    g = _log_sigmoid(_dot(gg_ref[0].astype(BF16), wg2_ref[...]) + bg_ref[...]) * (1.0 / GLA_TAU)
    if t_valid < L:
        live = _row_iota((L, LANES)) < t_valid
        g = jnp.where(live, g, 0.0)
        k = jnp.where(live, k, 0.0)
    v_b = v_ref[0].astype(BF16)
    gsum = _xdot_l(mall_ref[...], g)
    lane_head = _lane_iota((L, LANES)) // GLA_DK

    def heads_on_rows(x):
        return jnp.concatenate([jnp.where(lane_head == h, x, 0.0) for h in range(GLA_HEADS)], axis=0).astype(BF16)

    att = masks_ref[nl] * _dot_nt(heads_on_rows(q), k.astype(BF16))
    for lvl in range(nl):
        qt = q * jnp.exp(gsum[lvl * L:(lvl + 1) * L])
        kt = k * jnp.exp(gsum[(nl + lvl) * L:(nl + lvl + 1) * L])
        att = att + masks_ref[lvl] * _dot_nt(heads_on_rows(qt), kt.astype(BF16))
    bc = gsum[2 * nl * L:(2 * nl + 1) * L]
    tail = gsum[(2 * nl + 1) * L:(2 * nl + 2) * L]
    st = st_ref[...]
    res = _dot(att.astype(BF16), v_b) + _dot_nt(heads_on_rows(q * jnp.exp(bc)), st.astype(BF16))
    out_head = _lane_iota((L, BR)) // GLA_DV
    o = jnp.zeros((L, BR), F32)
    for h in range(GLA_HEADS):
        o = o + jnp.where(out_head == h, res[h * L:(h + 1) * L], 0.0)
    st_new = st * jnp.exp(bc[L - 1:L, :]) + _dot_tn(v_b, (k * jnp.exp(tail)).astype(BF16))
    st_ref[...] = st_new
    msq = _xdot_r(o * o, ones_ref[...]) * (1.0 / GLA_DV)
    y_ref[0] = o * lax.rsqrt(msq + EPS) * ng_ref[...] * _silu(r_ref[0])

    @pl.when(c == n_chunks - 1)
    def _():
        sout_ref[0] = st_new


def _gla_scan(q, k, v, gg, r, s0, wg2, bg, ng, chunk, t_valid):
    b, t, _ = q.shape
    n_chunks = t // chunk
    mall, masks, ones, nl = _gla_consts(chunk)
    row = lambda bi, ci: (bi, ci, 0)
    fixed2 = lambda bi, ci: (0, 0)
    return pl.pallas_call(
        functools.partial(_gla_kernel, chunk=chunk, t_valid=t_valid, n_chunks=n_chunks, nl=nl),
        grid=(b, n_chunks),
        in_specs=[pl.BlockSpec((1, chunk, LANES), row),
                  pl.BlockSpec((1, chunk, LANES), row),
                  pl.BlockSpec((1, chunk, BR), row),
                  pl.BlockSpec((1, chunk, LANES), row),
                  pl.BlockSpec((1, chunk, BR), row),
                  pl.BlockSpec((1, BR, LANES), lambda bi, ci: (bi, 0, 0)),
                  pl.BlockSpec((LANES, LANES), fixed2),
                  pl.BlockSpec((1, LANES), fixed2),
                  pl.BlockSpec((1, BR), fixed2),
                  pl.BlockSpec(mall.shape, fixed2),
                  pl.BlockSpec(masks.shape, lambda bi, ci: (0, 0, 0)),
                  pl.BlockSpec((BR, BR), fixed2)],
        out_specs=[pl.BlockSpec((1, chunk, BR), row),
                   pl.BlockSpec((1, BR, LANES), lambda bi, ci: (bi, 0, 0))],
        out_shape=[jax.ShapeDtypeStruct((b, t, BR), F32), jax.ShapeDtypeStruct((b, BR, LANES), F32)],
        scratch_shapes=[pltpu.VMEM((BR, LANES), F32)],
        compiler_params=_cparams(("arbitrary", "arbitrary")),
        name="gla_scan",
    )(q, k, v, gg, r, s0, wg2, bg, ng, mall, masks, ones)


def _gla_state_in(s):
    b = s.shape[0]
    out = jnp.zeros((b, BR, LANES), F32)
    for h in range(GLA_HEADS):
        out = out.at[:, h * GLA_DV:(h + 1) * GLA_DV, h * GLA_DK:(h + 1) * GLA_DK].set(s[:, h].transpose(0, 2, 1))
    return out


def _gla_state_out(st):
    return jnp.stack([st[:, h * GLA_DV:(h + 1) * GLA_DV, h * GLA_DK:(h + 1) * GLA_DK].transpose(0, 2, 1)
                      for h in range(GLA_HEADS)], axis=1)


def _gla_branch(q, k, v, gg, r, s0, lp, db, steps):
    st0 = _gla_state_in(s0)
    args = (lp["gla_wg2"], lp["gla_bg"], lp["gla_norm"])
    if db == 1:
        chunk = min(steps, 64)
        y, st = _gla_scan(q[None], k[None], v[None], gg[None], r[None], st0, *args, chunk, chunk)
        return y[0], _gla_state_out(st)
    qs, ks, vs, gs, rs = (_to_seq_major(a, db, steps, SUBLANES) for a in (q, k, v, gg, r))
    y, st = _gla_scan(qs, ks, vs, gs, rs, st0, *args, SUBLANES, steps)
    return _to_time_major(y, steps), _gla_state_out(st)


DIFF_MAPS = DIFF_HEADS * 2


def _diff_stack_q(q, rows):
    lane = _lane_iota((rows, LANES))
    blocks = []
    for kv in range(DIFF_KV):
        for g in range(DIFF_HEADS // DIFF_KV):
            qg = q[:, g * LANES:(g + 1) * LANES]
            for i in range(2):
                lo = kv * 2 * DIFF_DIM + i * DIFF_DIM
                blocks.append(jnp.where((lane >= lo) & (lane < lo + DIFF_DIM), qg, 0.0))
    return jnp.concatenate(blocks, axis=0).astype(BF16)


def _diff_slope(blk):
    head = blk // 2
    return 2.0 ** (-8.0 * (head + 1) / DIFF_HEADS)


def _online_update(s, blk, rows, m_ref, l_ref, p_ref):
    r0 = blk * rows
    m_prev = m_ref[r0:r0 + rows, :]
    m_new = jnp.maximum(m_prev, jnp.max(s, axis=1, keepdims=True))
    alpha = jnp.exp(m_prev - m_new)
    p = jnp.exp(s - m_new[:, 0:1])
    l_ref[r0:r0 + rows, :] = alpha * l_ref[r0:r0 + rows, :] + jnp.sum(p, axis=1, keepdims=True)
    m_ref[r0:r0 + rows, :] = m_new
    p_ref[r0:r0 + rows, 0:s.shape[1]] = p.astype(p_ref.dtype)
    return alpha


def _diff_finish(acc_ref, l_ref, prm_ref, rows):
    lane = _lane_iota((rows, LANES))
    low = lane < 2 * DIFF_DIM
    lam = prm_ref[1:2, :]
    outs = []
    for g in range(DIFF_HEADS // DIFF_KV):
        parts = []
        for kv in range(DIFF_KV):
            b1 = ((kv * 2 + g) * 2) * rows
            b2 = b1 + rows
            o1 = acc_ref[b1:b1 + rows, :] / l_ref[b1:b1 + rows, :]
            o2 = acc_ref[b2:b2 + rows, :] / l_ref[b2:b2 + rows, :]
            parts.append(o1 - lam * o2)
        og = jnp.where(low, parts[0], parts[1])
        sq = og * og
        ss = jnp.where(low, jnp.sum(jnp.where(low, sq, 0.0), axis=1, keepdims=True),
                       jnp.sum(jnp.where(low, 0.0, sq), axis=1, keepdims=True))
        outs.append(og * lax.rsqrt(ss * (1.0 / (2 * DIFF_DIM)) + EPS) * prm_ref[0:1, :])
    return outs


def _diff_flash_kernel(q_ref, k_ref, v_ref, prm_ref, o_ref, qs_ref, m_ref, l_ref, acc_ref, p_ref, *, tq, tk, nk):
    qi, ki = pl.program_id(0), pl.program_id(1)
    last_k = ((qi + 1) * tq - 1) // tk

    @pl.when(ki == 0)
    def _():
        qs_ref[...] = _diff_stack_q(q_ref[...] * (DIFF_DIM ** -0.5), tq)
        m_ref[...] = jnp.full(m_ref.shape, NEG, F32)
        l_ref[...] = jnp.zeros(l_ref.shape, F32)
        acc_ref[...] = jnp.zeros(acc_ref.shape, F32)

    @pl.when(ki <= last_k)
    def _():
        s = _dot_nt(qs_ref[...], k_ref[...].astype(BF16))
        ndist = ((ki * tk + _lane_iota((tq, tk))) - (qi * tq + _row_iota((tq, tk)))).astype(F32)
        vis = ndist <= 0.0
        for blk in range(DIFF_MAPS):
            sb = jnp.where(vis, s[blk * tq:(blk + 1) * tq] + _diff_slope(blk) * ndist, NEG)
            alpha = _online_update(sb, blk, tq, m_ref, l_ref, p_ref)
            acc_ref[blk * tq:(blk + 1) * tq, :] = alpha * acc_ref[blk * tq:(blk + 1) * tq, :]
        acc_ref[...] = acc_ref[...] + _dot(p_ref[...], v_ref[...].astype(BF16))

    @pl.when(ki == nk - 1)
    def _():
        outs = _diff_finish(acc_ref, l_ref, prm_ref, tq)
        o_ref[:, 0:LANES] = outs[0]
        o_ref[:, LANES:2 * LANES] = outs[1]


def _diff_flash(q, k, v, prm, tq, tk):
    n = q.shape[0]
    nq, nk = n // tq, n // tk
    kmap = lambda qi, ki: (jnp.minimum(ki, ((qi + 1) * tq - 1) // tk), 0)
    return pl.pallas_call(
        functools.partial(_diff_flash_kernel, tq=tq, tk=tk, nk=nk),
        grid=(nq, nk),
        in_specs=[pl.BlockSpec((tq, BR), lambda qi, ki: (qi, 0)),
                  pl.BlockSpec((tk, LANES), kmap),
                  pl.BlockSpec((tk, LANES), kmap),
                  pl.BlockSpec((SUBLANES, LANES), lambda qi, ki: (0, 0))],
        out_specs=pl.BlockSpec((tq, BR), lambda qi, ki: (qi, 0)),
        out_shape=jax.ShapeDtypeStruct((n, BR), F32),
        scratch_shapes=[pltpu.VMEM((DIFF_MAPS * tq, LANES), BF16),
                        pltpu.VMEM((DIFF_MAPS * tq, LANES), F32),
                        pltpu.VMEM((DIFF_MAPS * tq, LANES), F32),
                        pltpu.VMEM((DIFF_MAPS * tq, LANES), F32),
                        pltpu.VMEM((DIFF_MAPS * tq, tk), BF16)],
        compiler_params=_cparams(("arbitrary", "arbitrary")),
        name="diff_flash",
    )(q, k, v, prm)


def _page_copies(pt_ref, caches, bufs, sems, layer, b, c, slot, pages_per_step):
    copies = []
    for a, (cache, buf) in enumerate(zip(caches, bufs)):
        for g in range(pages_per_step):
            page = pt_ref[b, c * pages_per_step + g]
            copies.append(pltpu.make_async_copy(cache.at[layer, page], buf.at[slot, g], sems.at[slot, a]))
    return copies


def _stream_pages(pt_ref, caches, bufs, sems, layer, pages_per_step, n_chunks):
    b, c = pl.program_id(0), pl.program_id(1)
    step = b * n_chunks + c
    total = pl.num_programs(0) * n_chunks
    slot = step % 2

    @pl.when(step == 0)
    def _():
        for cp in _page_copies(pt_ref, caches, bufs, sems, layer, b, c, slot, pages_per_step):
            cp.start()

    @pl.when(step + 1 < total)
    def _():
        wrap = c + 1 == n_chunks
        nb = jnp.where(wrap, b + 1, b)
        nc = jnp.where(wrap, 0, c + 1)
        for cp in _page_copies(pt_ref, caches, bufs, sems, layer, nb, nc, 1 - slot, pages_per_step):
            cp.start()

    for cp in _page_copies(pt_ref, caches, bufs, sems, layer, b, c, slot, pages_per_step):
        cp.wait()
    return slot


def _diff_decode_kernel(pt_ref, q_ref, kn_ref, vn_ref, prm_ref, kc_ref, vc_ref, o_ref,
                        kbuf, vbuf, sems, qs_ref, m_ref, l_ref, acc_ref, p_ref,
                        *, layer, pages_per_step, n_chunks, steps, past_len):
    G = pages_per_step
    R8 = SUBLANES
    c = pl.program_id(1)
    slot = _stream_pages(pt_ref, (kc_ref, vc_ref), (kbuf, vbuf), sems, layer, G, n_chunks)

    @pl.when(c == 0)
    def _():
        qs_ref[...] = _diff_stack_q(q_ref[0] * (DIFF_DIM ** -0.5), R8)
        m_ref[...] = jnp.full(m_ref.shape, NEG, F32)
        l_ref[...] = jnp.zeros(l_ref.shape, F32)
        acc_ref[...] = jnp.zeros(acc_ref.shape, F32)

    qs = qs_ref[...]
    s = jnp.concatenate([_dot(qs, kbuf[slot, g].reshape(LANES, PAGE).astype(BF16)) for g in range(G)], axis=1)
    width = G * PAGE
    ndist = ((c * width + _lane_iota((R8, width))) - (past_len + _row_iota((R8, width)))).astype(F32)
    for blk in range(DIFF_MAPS):
        sb = s[blk * R8:(blk + 1) * R8] + _diff_slope(blk) * ndist
        alpha = _online_update(sb, blk, R8, m_ref, l_ref, p_ref)
        acc_ref[blk * R8:(blk + 1) * R8, :] = alpha * acc_ref[blk * R8:(blk + 1) * R8, :]
    pv = jnp.zeros(acc_ref.shape, F32)
    for g in range(G):
        pv = pv + _dot_nt(p_ref[:, g * PAGE:(g + 1) * PAGE].astype(BF16),
                          vbuf[slot, g].reshape(LANES, PAGE).astype(BF16))
    acc_ref[...] = acc_ref[...] + pv

    @pl.when(c == n_chunks - 1)
    def _():
        pad = jnp.zeros((PAGE - R8, LANES), F32)
        kn = jnp.concatenate([kn_ref[0], pad], axis=0).astype(BF16)
        vn = jnp.concatenate([vn_ref[0], pad], axis=0).astype(BF16)
        sn = _dot_nt(qs, kn)
        kt = _lane_iota((R8, PAGE))
        qt = _row_iota((R8, PAGE))
        vis = (kt <= qt) & (kt < steps)
        nd = (kt - qt).astype(F32)
        for blk in range(DIFF_MAPS):
            sb = jnp.where(vis, sn[blk * R8:(blk + 1) * R8] + _diff_slope(blk) * nd, NEG)
            alpha = _online_update(sb, blk, R8, m_ref, l_ref, p_ref)
            acc_ref[blk * R8:(blk + 1) * R8, :] = alpha * acc_ref[blk * R8:(blk + 1) * R8, :]
        acc_ref[...] = acc_ref[...] + _dot(p_ref[:, 0:PAGE].astype(BF16), vn)
        outs = _diff_finish(acc_ref, l_ref, prm_ref, R8)
        o_ref[0, :, 0:LANES] = outs[0]
        o_ref[0, :, LANES:2 * LANES] = outs[1]


def _diff_decode(page_table, q, k_new, v_new, prm, kc, vc, layer, steps, pages_per_step):
    db, n_pages = page_table.shape
    G = pages_per_step
    n_chunks = n_pages // G
    rows = DIFF_MAPS * SUBLANES
    seq3 = lambda b, c, pt: (b, 0, 0)
    grid_spec = pltpu.PrefetchScalarGridSpec(
        num_scalar_prefetch=1,
        grid=(db, n_chunks),
        in_specs=[pl.BlockSpec((1, SUBLANES, BR), seq3),
                  pl.BlockSpec((1, SUBLANES, LANES), seq3),
                  pl.BlockSpec((1, SUBLANES, LANES), seq3),
                  pl.BlockSpec((SUBLANES, LANES), lambda b, c, pt: (0, 0)),
                  pl.BlockSpec(memory_space=pl.ANY),
                  pl.BlockSpec(memory_space=pl.ANY)],
        out_specs=pl.BlockSpec((1, SUBLANES, BR), seq3),
        scratch_shapes=[pltpu.VMEM((2, G, DIFF_KV, 2 * DIFF_DIM, PAGE), F32),
                        pltpu.VMEM((2, G, DIFF_KV, 2 * DIFF_DIM, PAGE), F32),
                        pltpu.SemaphoreType.DMA((2, 2)),
                        pltpu.VMEM((rows, LANES), BF16),
                        pltpu.VMEM((rows, LANES), F32),
                        pltpu.VMEM((rows, LANES), F32),
                        pltpu.VMEM((rows, LANES), F32),
                        pltpu.VMEM((rows, G * PAGE), F32)])
    return pl.pallas_call(
        functools.partial(_diff_decode_kernel, layer=layer, pages_per_step=G, n_chunks=n_chunks,
                          steps=steps, past_len=n_pages * PAGE),
        grid_spec=grid_spec,
        out_shape=jax.ShapeDtypeStruct((db, SUBLANES, BR), F32),
        compiler_params=_cparams(("arbitrary", "arbitrary")),
        name="diff_decode",
    )(page_table, q, k_new, v_new, prm, kc, vc)


MLA_QW = 2 * LANES
MLA_SCALE = (MLA_NOPE + MLA_ROPE) ** -0.5
ROPE_HALF = MLA_ROPE // 2


def _mla_prep_kernel(cq_ref, ckv_ref, kr_ref, cos_ref, sin_ref, qn_ref, kvn_ref, wuq_ref, wcat_ref,
                     qcat_ref, kcat_ref, c_ref, krout_ref):
    tm = cq_ref.shape[0]
    lane = _lane_iota((tm, LANES))
    cos = cos_ref[...]
    sin = sin_ref[...]
    q = _dot(_rms(cq_ref[...], qn_ref[...]).astype(BF16), wuq_ref[...])
    n_nope = MLA_HEADS * MLA_NOPE
    rq = q[:, n_nope:n_nope + LANES]
    rq = rq * cos + pltpu.roll(rq, LANES // 2, 1) * jnp.where(lane < LANES // 2, -sin, sin)
    q_in = jnp.concatenate([q[:, 0:n_nope], rq], axis=1) * MLA_SCALE
    qcat_ref[...] = _dot(q_in.astype(BF16), wcat_ref[...])
    c = _rms(ckv_ref[...], kvn_ref[...])
    c_ref[...] = c
    kr = kr_ref[...]
    swapped = jnp.where(lane < ROPE_HALF, pltpu.roll(kr, LANES - ROPE_HALF, 1), pltpu.roll(kr, ROPE_HALF, 1))
    kro = kr * cos + swapped * jnp.where(lane < ROPE_HALF, -sin, sin)
    kro = jnp.where(lane < MLA_ROPE, kro, 0.0)
    krout_ref[...] = kro[:, 0:MLA_ROPE]
    kcat_ref[...] = jnp.concatenate([c, kro], axis=1)


def _mla_prep(cq, ckv, kr, cos, sin, lp, tm):
    n = cq.shape[0]
    row = lambda i: (i, 0)
    fixed = lambda i: (0, 0)
    return pl.pallas_call(
        _mla_prep_kernel,
        grid=(n // tm,),
        in_specs=[pl.BlockSpec((tm, MLA_QR), row), pl.BlockSpec((tm, LANES), row), pl.BlockSpec((tm, LANES), row),
                  pl.BlockSpec((tm, LANES), row), pl.BlockSpec((tm, LANES), row),
                  pl.BlockSpec((1, MLA_QR), fixed), pl.BlockSpec((1, LANES), fixed),
                  pl.BlockSpec(lp["mla_wuq"].shape, fixed), pl.BlockSpec(lp["mla_wcat"].shape, fixed)],
        out_specs=[pl.BlockSpec((tm, MLA_HEADS * MLA_QW), row), pl.BlockSpec((tm, MLA_QW), row),
                   pl.BlockSpec((tm, LANES), row), pl.BlockSpec((tm, MLA_ROPE), row)],
        out_shape=[jax.ShapeDtypeStruct((n, MLA_HEADS * MLA_QW), F32), jax.ShapeDtypeStruct((n, MLA_QW), F32),
                   jax.ShapeDtypeStruct((n, LANES), F32), jax.ShapeDtypeStruct((n, MLA_ROPE), F32)],
        compiler_params=_cparams(("arbitrary",)),
        name="mla_prep",
    )(cq, ckv, kr, cos, sin, lp["mla_q_norm"], lp["mla_kv_norm"], lp["mla_wuq"], lp["mla_wcat"])


def _mla_stack_q(qcat):
    return jnp.concatenate([qcat[:, h * MLA_QW:(h + 1) * MLA_QW] for h in range(MLA_HEADS)], axis=0).astype(BF16)


def _mla_finish(acc_ref, l_ref, wuv_ref, rows):
    y = jnp.zeros((rows, BR), F32)
    for h in range(MLA_HEADS):
        o_lat = acc_ref[h * rows:(h + 1) * rows, :] / l_ref[h * rows:(h + 1) * rows, :]
        y = y + _dot(o_lat.astype(BF16), wuv_ref[h])
    return y


def _mla_flash_kernel(q_ref, k_ref, wuv_ref, o_ref, qs_ref, m_ref, l_ref, acc_ref, p_ref, *, tq, tk, nk):
    qi, ki = pl.program_id(0), pl.program_id(1)
    last_k = ((qi + 1) * tq - 1) // tk

    @pl.when(ki == 0)
    def _():
        qs_ref[...] = _mla_stack_q(q_ref[...])
        m_ref[...] = jnp.full(m_ref.shape, NEG, F32)
        l_ref[...] = jnp.zeros(l_ref.shape, F32)
        acc_ref[...] = jnp.zeros(acc_ref.shape, F32)

    @pl.when(ki <= last_k)
    def _():
        kb = k_ref[...].astype(BF16)
        s = _dot_nt(qs_ref[...], kb)
        vis = (ki * tk + _lane_iota((tq, tk))) <= (qi * tq + _row_iota((tq, tk)))
        for h in range(MLA_HEADS):
            sb = jnp.where(vis, s[h * tq:(h + 1) * tq], NEG)
            alpha = _online_update(sb, h, tq, m_ref, l_ref, p_ref)
            acc_ref[h * tq:(h + 1) * tq, :] = alpha * acc_ref[h * tq:(h + 1) * tq, :]
        acc_ref[...] = acc_ref[...] + _dot(p_ref[...], kb[:, 0:MLA_KVR])

    @pl.when(ki == nk - 1)
    def _():
        o_ref[...] = _mla_finish(acc_ref, l_ref, wuv_ref, tq)


def _mla_flash(qcat, kcat, wuv, tq, tk):
    n = qcat.shape[0]
    nq, nk = n // tq, n // tk
    kmap = lambda qi, ki: (jnp.minimum(ki, ((qi + 1) * tq - 1) // tk), 0)
    rows = MLA_HEADS * tq
    return pl.pallas_call(
        functools.partial(_mla_flash_kernel, tq=tq, tk=tk, nk=nk),
        grid=(nq, nk),
        in_specs=[pl.BlockSpec((tq, MLA_HEADS * MLA_QW), lambda qi, ki: (qi, 0)),
                  pl.BlockSpec((tk, MLA_QW), kmap),
                  pl.BlockSpec(wuv.shape, lambda qi, ki: (0, 0, 0))],
        out_specs=pl.BlockSpec((tq, BR), lambda qi, ki: (qi, 0)),
        out_shape=jax.ShapeDtypeStruct((n, BR), F32),
        scratch_shapes=[pltpu.VMEM((rows, MLA_QW), BF16),
                        pltpu.VMEM((rows, LANES), F32),
                        pltpu.VMEM((rows, LANES), F32),
                        pltpu.VMEM((rows, LANES), F32),
                        pltpu.VMEM((rows, tk), BF16)],
        compiler_params=_cparams(("arbitrary", "arbitrary")),
        name="mla_flash",
    )(qcat, kcat, wuv)


def _mla_decode_kernel(pt_ref, q_ref, kn_ref, wuv_ref, cc_ref, rc_ref, o_ref,
                       cbuf, rbuf, sems, qs_ref, m_ref, l_ref, acc_ref, p_ref,
                       *, layer, pages_per_step, n_chunks, steps):
    G = pages_per_step
    R8 = SUBLANES
    c = pl.program_id(1)
    slot = _stream_pages(pt_ref, (cc_ref, rc_ref), (cbuf, rbuf), sems, layer, G, n_chunks)

    @pl.when(c == 0)
    def _():
        qs_ref[...] = _mla_stack_q(q_ref[0])
        m_ref[...] = jnp.full(m_ref.shape, NEG, F32)
        l_ref[...] = jnp.zeros(l_ref.shape, F32)
        acc_ref[...] = jnp.zeros(acc_ref.shape, F32)

    qs = qs_ref[...]
    q_lat = qs[:, 0:MLA_KVR]
    q_rope = qs[:, MLA_KVR:MLA_KVR + MLA_ROPE]
    s = jnp.concatenate(
        [_dot_nt(q_lat, cbuf[slot, g].astype(BF16)) + _dot(q_rope, rbuf[slot, g].astype(BF16)) for g in range(G)],
        axis=1)
    for h in range(MLA_HEADS):
        alpha = _online_update(s[h * R8:(h + 1) * R8], h, R8, m_ref, l_ref, p_ref)
        acc_ref[h * R8:(h + 1) * R8, :] = alpha * acc_ref[h * R8:(h + 1) * R8, :]
    pv = jnp.zeros(acc_ref.shape, F32)
    for g in range(G):
        pv = pv + _dot(p_ref[:, g * PAGE:(g + 1) * PAGE].astype(BF16), cbuf[slot, g].astype(BF16))
    acc_ref[...] = acc_ref[...] + pv

    @pl.when(c == n_chunks - 1)
    def _():
        kn = jnp.concatenate([kn_ref[0], jnp.zeros((PAGE - R8, MLA_QW), F32)], axis=0).astype(BF16)
        sn = _dot_nt(qs, kn)
        kt = _lane_iota((R8, PAGE))
        vis = (kt <= _row_iota((R8, PAGE))) & (kt < steps)
        for h in range(MLA_HEADS):
            alpha = _online_update(jnp.where(vis, sn[h * R8:(h + 1) * R8], NEG), h, R8, m_ref, l_ref, p_ref)
            acc_ref[h * R8:(h + 1) * R8, :] = alpha * acc_ref[h * R8:(h + 1) * R8, :]
        acc_ref[...] = acc_ref[...] + _dot(p_ref[:, 0:PAGE].astype(BF16), kn[:, 0:MLA_KVR])
        o_ref[0] = _mla_finish(acc_ref, l_ref, wuv_ref, R8)


def _mla_decode(page_table, qcat, kcat_new, wuv, cc, rc, layer, steps, pages_per_step):
    db, n_pages = page_table.shape
    G = pages_per_step
    n_chunks = n_pages // G
    rows = MLA_HEADS * SUBLANES
    seq3 = lambda b, c, pt: (b, 0, 0)
    grid_spec = pltpu.PrefetchScalarGridSpec(
        num_scalar_prefetch=1,
        grid=(db, n_chunks),
        in_specs=[pl.BlockSpec((1, SUBLANES, MLA_HEADS * MLA_QW), seq3),
                  pl.BlockSpec((1, SUBLANES, MLA_QW), seq3),
                  pl.BlockSpec(wuv.shape, lambda b, c, pt: (0, 0, 0)),
                  pl.BlockSpec(memory_space=pl.ANY),
                  pl.BlockSpec(memory_space=pl.ANY)],
        out_specs=pl.BlockSpec((1, SUBLANES, BR), seq3),
        scratch_shapes=[pltpu.VMEM((2, G, PAGE, MLA_KVR), F32),
                        pltpu.VMEM((2, G, MLA_ROPE, PAGE), F32),
                        pltpu.SemaphoreType.DMA((2, 2)),
                        pltpu.VMEM((rows, MLA_QW), BF16),
                        pltpu.VMEM((rows, LANES), F32),
                        pltpu.VMEM((rows, LANES), F32),
                        pltpu.VMEM((rows, LANES), F32),
                        pltpu.VMEM((rows, G * PAGE), F32)])
    return pl.pallas_call(
        functools.partial(_mla_decode_kernel, layer=layer, pages_per_step=G, n_chunks=n_chunks, steps=steps),
        grid_spec=grid_spec,
        out_shape=jax.ShapeDtypeStruct((db, SUBLANES, BR), F32),
        compiler_params=_cparams(("arbitrary", "arbitrary")),
        name="mla_decode",
    )(page_table, qcat, kcat_new, wuv, cc, rc)


def _mem_attn_kernel(q_ref, mk_ref, mv_ref, o_ref, *, key_minor):
    rows = q_ref.shape[1]
    lane_head = _lane_iota((rows, BR)) // MEM_DIM
    q = q_ref[0] * (MEM_DIM ** -0.5)
    qs = jnp.concatenate([jnp.where(lane_head == h, q, 0.0) for h in range(MEM_HEADS)], axis=0).astype(BF16)
    mk = mk_ref[0].astype(BF16)
    mv = mv_ref[0].astype(BF16)
    s = _dot(qs, mk) if key_minor else _dot_nt(qs, mk)
    p = jnp.exp(s - jnp.max(s, axis=1, keepdims=True))
    p = (p / jnp.sum(p, axis=1, keepdims=True)).astype(BF16)
    r = _dot_nt(p, mv) if key_minor else _dot(p, mv)
    y = jnp.zeros((rows, BR), F32)
    for h in range(MEM_HEADS):
        y = y + jnp.where(lane_head == h, r[h * rows:(h + 1) * rows], 0.0)
    o_ref[0] = y


def _mem_attn(q3, mk3, mv3, key_minor):
    b, rows, _ = q3.shape
    per_seq = mk3.shape[0] == b and b > 1
    kv_map = (lambda i: (i, 0, 0)) if per_seq else (lambda i: (0, 0, 0))
    return pl.pallas_call(
        functools.partial(_mem_attn_kernel, key_minor=key_minor),
        grid=(b,),
        in_specs=[pl.BlockSpec((1, rows, BR), lambda i: (i, 0, 0)),
                  pl.BlockSpec((1,) + mk3.shape[1:], kv_map),
                  pl.BlockSpec((1,) + mv3.shape[1:], kv_map)],
        out_specs=pl.BlockSpec((1, rows, BR), lambda i: (i, 0, 0)),
        out_shape=jax.ShapeDtypeStruct((b, rows, BR), F32),
        compiler_params=_cparams(("arbitrary",)),
        name="mem_attn",
    )(q3, mk3, mv3)


def _matmul_kernel(x_ref, w_ref, o_ref):
    o_ref[...] = _dot(x_ref[...].astype(BF16), w_ref[...])


def _matmul(x, w):
    m, k = x.shape
    n = w.shape[1]
    return pl.pallas_call(
        _matmul_kernel,
        grid=(1,),
        in_specs=[pl.BlockSpec((m, k), lambda i: (0, 0)), pl.BlockSpec((k, n), lambda i: (0, 0))],
        out_specs=pl.BlockSpec((m, n), lambda i: (0, 0)),
        out_shape=jax.ShapeDtypeStruct((m, n), F32),
        compiler_params=_cparams(("arbitrary",)),
        name="mem_proj",
    )(x, w)


def _merge_kernel(x_ref, y0, y1, y2, y3, y4, gpre_ref, gpost_ref, wg_ref, wb_ref, wo_ref, o_ref):
    x = x_ref[...]
    d = x.shape[1]
    hn = _rms(x, gpre_ref[...]).astype(BF16)
    m = jnp.zeros(x.shape, F32)
    for i, y_ref in enumerate((y0, y1, y2, y3, y4)):
        gate = _sigmoid(_dot(hn, wg_ref[:, i * d:(i + 1) * d]))
        m = m + gate * _dot(y_ref[...].astype(BF16), wb_ref[i])
    o_ref[...] = x + _rms(_dot(m.astype(BF16), wo_ref[...]), gpost_ref[...])


def _merge_out(x, ys, lp, tm):
    n, d = x.shape
    row = lambda i: (i, 0)
    fixed = lambda i: (0, 0)
    return pl.pallas_call(
        _merge_kernel,
        grid=(n // tm,),
        in_specs=[pl.BlockSpec((tm, d), row)] + [pl.BlockSpec((tm, BR), row)] * N_BRANCH + [
            pl.BlockSpec((1, d), fixed), pl.BlockSpec((1, d), fixed),
            pl.BlockSpec(lp["w_gate"].shape, fixed),
            pl.BlockSpec(lp["w_branch"].shape, lambda i: (0, 0, 0)),
            pl.BlockSpec(lp["w_out"].shape, fixed)],
        out_specs=pl.BlockSpec((tm, d), row),
        out_shape=jax.ShapeDtypeStruct((n, d), F32),
        compiler_params=_cparams(("arbitrary",)),
        name="merge_out",
    )(x, *ys, lp["g_pre_mix"], lp["g_post_mix"], lp["w_gate"], lp["w_branch"], lp["w_out"])


FFN_CHUNK = 2 * LANES


def _ffn_kernel(x_ref, st_ref, gpre_ref, gpost_ref, wi_ref, cw_ref, cb_ref, wo_ref, o_ref, ns_ref,
                ext_ref, acc_ref, *, shift, pad, width):
    tm = x_ref.shape[0]

    @pl.when(pl.program_id(0) == 0)
    def _():
        ext_ref[0:pad, :] = st_ref[...]

    x = x_ref[...]
    hn = _rms(x, gpre_ref[...]).astype(BF16)
    for c0 in range(0, width, FFN_CHUNK):
        a = _dot(hn, wi_ref[:, c0:c0 + FFN_CHUNK])
        gate = _dot(hn, wi_ref[:, width + c0:width + c0 + FFN_CHUNK])
        ext_ref[pad:pad + tm, c0:c0 + FFN_CHUNK] = a
        conv = cb_ref[:, c0:c0 + FFN_CHUNK] + cw_ref[FFN_CONV - 1:FFN_CONV, c0:c0 + FFN_CHUNK] * a
        for j in range(FFN_CONV - 1):
            off = pad - (FFN_CONV - 1 - j) * shift
            conv = conv + cw_ref[j:j + 1, c0:c0 + FFN_CHUNK] * ext_ref[off:off + tm, c0:c0 + FFN_CHUNK]
        act = (_gelu_tanh(conv) * gate).astype(BF16)
        contrib = _dot(act, wo_ref[c0:c0 + FFN_CHUNK, :])
        if c0 == 0:
            acc_ref[...] = contrib
        else:
            acc_ref[...] = acc_ref[...] + contrib
    o_ref[...] = x + _rms(acc_ref[...], gpost_ref[...])
    tail = ext_ref[tm:tm + pad, :]
    ns_ref[...] = tail
    ext_ref[0:pad, :] = tail


def _ffn(x, state, lp, shift, tm):
    n, d = x.shape
    pad, width = state.shape
    assert tm >= pad and pad >= (FFN_CONV - 1) * shift and width % FFN_CHUNK == 0
    row = lambda i: (i, 0)
    fixed = lambda i: (0, 0)
    return pl.pallas_call(
        functools.partial(_ffn_kernel, shift=shift, pad=pad, width=width),
        grid=(n // tm,),
        in_specs=[pl.BlockSpec((tm, d), row), pl.BlockSpec((pad, width), fixed),
                  pl.BlockSpec((1, d), fixed), pl.BlockSpec((1, d), fixed),
                  pl.BlockSpec(lp["w_ffn_in"].shape, fixed),
                  pl.BlockSpec((FFN_CONV, width), fixed), pl.BlockSpec((1, width), fixed),
                  pl.BlockSpec(lp["w_ffn_out"].shape, fixed)],
        out_specs=[pl.BlockSpec((tm, d), row), pl.BlockSpec((pad, width), fixed)],
        out_shape=[jax.ShapeDtypeStruct((n, d), F32), jax.ShapeDtypeStruct((pad, width), F32)],
        scratch_shapes=[pltpu.VMEM((pad + tm, width), F32), pltpu.VMEM((tm, d), F32)],
        compiler_params=_cparams(("arbitrary",)),
        name="conv_ffn",
    )(x, state, lp["g_pre_ffn"], lp["g_post_ffn"], lp["w_ffn_in"], lp["ffn_conv_w"], lp["ffn_conv_b"],
      lp["w_ffn_out"])


def _rope_tables(pos):
    freqs = jnp.power(ROPE_THETA, -jnp.arange(ROPE_HALF, dtype=F32) / ROPE_HALF)
    ang = pos.astype(F32)[:, None] * freqs
    reps = LANES // ROPE_HALF
    return jnp.tile(jnp.cos(ang), (1, reps)), jnp.tile(jnp.sin(ang), (1, reps))


def _prep_layer(li, P):
    d_model = P["w_in"].shape[1]
    w_in = P["w_in"][li]
    bounds = np.cumsum((0,) + IN_WIDTHS)
    seg = [w_in[:, bounds[i]:bounds[i + 1]] for i in range(len(IN_WIDTHS))]
    grp = DIFF_HEADS // DIFF_KV
    seg[3] = seg[3].reshape(d_model, DIFF_KV, grp, 2 * DIFF_DIM).swapaxes(1, 2).reshape(d_model, -1)
    seg = [_pad_lanes(s, wd) for s, wd in zip(seg, PROJ_WIDTHS)]
    lp = {"w_cat": jnp.concatenate(seg, axis=1).astype(BF16),
          "w_gate": w_in[:, bounds[-1]:].astype(BF16)}
    for name in ("g_pre_mix", "g_post_mix", "g_pre_ffn", "g_post_ffn", "ssd_conv_b", "ssd_norm", "gla_bg",
                 "mla_q_norm", "mla_kv_norm", "ffn_conv_b"):
        lp[name] = P[name][li][None]
    lp["ssd_conv_w"] = P["ssd_conv_w"][li]
    lp["ffn_conv_w"] = P["ffn_conv_w"][li]
    prm = jnp.zeros((SUBLANES, LANES), F32)
    lp["ssd_prm"] = prm.at[0, :SSD_HEADS].set(P["ssd_dt_bias"][li]).at[1, :SSD_HEADS].set(-jnp.exp(P["ssd_a_log"][li]))
    lp["ssd_dskip"] = jnp.repeat(P["ssd_d"][li], SSD_HEAD_DIM)[None]
    lam_init = 0.8 - 0.6 * math.exp(-0.3 * li)
    lam = (jnp.exp(jnp.sum(P["diff_lq1"][li] * P["diff_lk1"][li]))
           - jnp.exp(jnp.sum(P["diff_lq2"][li] * P["diff_lk2"][li])) + lam_init)
    lp["diff_prm"] = prm.at[0].set(jnp.tile(P["diff_norm"][li], DIFF_KV) * (1.0 - lam_init)).at[1].set(lam)
    lp["gla_wg2"] = jnp.zeros((LANES, LANES), F32).at[:GLA_RANK].set(P["gla_wg2"][li]).astype(BF16)
    lp["gla_norm"] = jnp.tile(P["gla_norm"][li], GLA_HEADS)[None]
    wuq = P["mla_wuq"][li].reshape(MLA_QR, MLA_HEADS, MLA_NOPE + MLA_ROPE)
    lp["mla_wuq"] = jnp.concatenate(
        [wuq[:, :, :MLA_NOPE].reshape(MLA_QR, -1),
         wuq[:, :, MLA_NOPE:MLA_NOPE + ROPE_HALF].reshape(MLA_QR, -1),
         wuq[:, :, MLA_NOPE + ROPE_HALF:].reshape(MLA_QR, -1)], axis=1).astype(BF16)
    n_nope = MLA_HEADS * MLA_NOPE
    wcat = jnp.zeros((n_nope + LANES, MLA_HEADS * MLA_QW), F32)
    wuv = jnp.zeros((MLA_HEADS, MLA_KVR, BR), F32)
    eye = jnp.eye(ROPE_HALF, dtype=F32)
    for h in range(MLA_HEADS):
        wcat = wcat.at[h * MLA_NOPE:(h + 1) * MLA_NOPE, h * MLA_QW:h * MLA_QW + MLA_KVR].set(P["mla_wuk"][li][:, h, :].T)
        for half in range(2):
            r0 = n_nope + half * (LANES // 2) + h * ROPE_HALF
            c0 = h * MLA_QW + MLA_KVR + half * ROPE_HALF
            wcat = wcat.at[r0:r0 + ROPE_HALF, c0:c0 + ROPE_HALF].set(eye)
        wuv = wuv.at[h, :, h * MLA_V:(h + 1) * MLA_V].set(P["mla_wuv"][li][:, h, :])
    lp["mla_wcat"] = wcat.astype(BF16)
    lp["mla_wuv"] = wuv.astype(BF16)
    lp["w_mem_kv"] = jnp.concatenate([P["w_mem_k"][li], P["w_mem_v"][li]], axis=1).astype(BF16)
    wb = P["w_branch"][li]
    wb_diff = wb[1].reshape(DIFF_KV, grp, 2 * DIFF_DIM, d_model).swapaxes(0, 1).reshape(BR, d_model)
    lp["w_branch"] = wb.at[1].set(wb_diff).astype(BF16)
    lp["w_out"] = P["w_out"][li].astype(BF16)
    lp["w_ffn_in"] = P["w_ffn_in"][li].astype(BF16)
    lp["w_ffn_out"] = P["w_ffn_out"][li].astype(BF16)
    return lp


def _row_tile(n, want):
    return want if n % want == 0 else n


def _layer(x, lp, li, db, steps, st, past):
    n = x.shape[0]
    tm = _row_tile(n, 256)
    (z, xbc, dt, dq, dk, dv, gq, gk, gv, gg, gr, cq, ckv, kr, mq) = _proj_in(x, lp["g_pre_mix"], lp["w_cat"], tm)
    y_ssd, ssd_conv, ssd_h = _ssd_branch(z, xbc, dt, st["ssd_conv"], st["ssd_h"], lp, db, steps)
    y_gla, gla_s = _gla_branch(gq, gk, gv, gg, gr, st["gla_s"], lp, db, steps)
    past_len = 0 if past is None else past["page_table"].shape[1] * PAGE
    pos = jnp.repeat(past_len + jnp.arange(steps, dtype=jnp.int32), db)
    cos, sin = _rope_tables(pos)
    qcat, kcat, c_rows, kr_rows = _mla_prep(cq, ckv, kr, cos, sin, lp, tm)
    if past is None:
        tq, tk = _row_tile(n, 128), _row_tile(n, 512)
        y_diff = _diff_flash(dq, dk, dv, lp["diff_prm"], tq, tk)
        y_mla = _mla_flash(qcat, kcat, lp["mla_wuv"], tq, tk)
        y_mem = _mem_attn(mq.reshape(n // tm, tm, BR), st["mem_k"], st["mem_v"], key_minor=False).reshape(n, BR)
    else:
        seq = lambda a: _to_seq_major(a, db, steps, SUBLANES)
        pt = past["page_table"]
        g_pages = math.gcd(pt.shape[1], 16)
        y_diff = _to_time_major(_diff_decode(pt, seq(dq), seq(dk), seq(dv), lp["diff_prm"], past["diff_k"],
                                             past["diff_v"], li, steps, g_pages), steps)
        y_mla = _to_time_major(_mla_decode(pt, seq(qcat), seq(kcat), lp["mla_wuv"], past["mla_ckv"],
                                           past["mla_kr"], li, steps, g_pages), steps)
        y_mem = _to_time_major(_mem_attn(seq(mq), st["mem_k"], st["mem_v"], key_minor=True), steps)
    x = _merge_out(x, (y_ssd, y_diff, y_gla, y_mla, y_mem), lp, tm)
    x, ffn_conv = _ffn(x, st["ffn_conv"], lp, shift=db, tm=_row_tile(n, 256) if db == 1 else n)
    return x, (dk, dv, c_rows, kr_rows, ssd_h, ssd_conv, gla_s, ffn_conv)


def kernel(x_prompt, x_sample, cache_diff_k, cache_diff_v, cache_mla_ckv, cache_mla_krope, cache_mem_k, cache_mem_v, state_ssd, state_ssd_conv, state_gla, state_ffn_conv, page_table, mem_prompt, w_in, ssd_conv_w, ssd_conv_b, ssd_dt_bias, ssd_a_log, ssd_d, ssd_norm, diff_lq1, diff_lk1, diff_lq2, diff_lk2, diff_norm, gla_wg2, gla_bg, gla_norm, mla_q_norm, mla_kv_norm, mla_wuq, mla_wuk, mla_wuv, w_mem_k, w_mem_v, w_branch, w_out, g_pre_mix, g_post_mix, g_pre_ffn, g_post_ffn, w_ffn_in, ffn_conv_w, ffn_conv_b, w_ffn_out):
    P = dict(w_in=w_in, ssd_conv_w=ssd_conv_w, ssd_conv_b=ssd_conv_b, ssd_dt_bias=ssd_dt_bias,
             ssd_a_log=ssd_a_log, ssd_d=ssd_d, ssd_norm=ssd_norm, diff_lq1=diff_lq1, diff_lk1=diff_lk1,
             diff_lq2=diff_lq2, diff_lk2=diff_lk2, diff_norm=diff_norm, gla_wg2=gla_wg2, gla_bg=gla_bg,
             gla_norm=gla_norm, mla_q_norm=mla_q_norm, mla_kv_norm=mla_kv_norm, mla_wuq=mla_wuq,
             mla_wuk=mla_wuk, mla_wuv=mla_wuv, w_mem_k=w_mem_k, w_mem_v=w_mem_v, w_branch=w_branch,
             w_out=w_out, g_pre_mix=g_pre_mix, g_post_mix=g_post_mix, g_pre_ffn=g_pre_ffn,
             g_post_ffn=g_post_ffn, w_ffn_in=w_ffn_in, ffn_conv_w=ffn_conv_w, ffn_conv_b=ffn_conv_b,
             w_ffn_out=w_ffn_out)
    depth = w_in.shape[0]
    pb, seq_len, d_model = x_prompt.shape
    assert pb == 1
    db, steps, _ = x_sample.shape
    mem_len = mem_prompt.shape[1]
    ffn_dim = ffn_conv_w.shape[2]
    past = {"page_table": page_table,
            "diff_k": jnp.transpose(cache_diff_k, (0, 1, 3, 4, 2)),
            "diff_v": jnp.transpose(cache_diff_v, (0, 1, 3, 4, 2)),
            "mla_ckv": cache_mla_ckv,
            "mla_kr": jnp.transpose(cache_mla_krope, (0, 1, 3, 2))}
    mem_kt = jnp.transpose(cache_mem_k, (0, 1, 3, 4, 2)).reshape(depth, db, BR, mem_len)
    mem_vt = jnp.transpose(cache_mem_v, (0, 1, 3, 4, 2)).reshape(depth, db, BR, mem_len)
    xp = x_prompt[0]
    xs = x_sample.transpose(1, 0, 2).reshape(steps * db, d_model)
    new_p, new_s, mem_kp, mem_vp = [], [], [], []
    for li in range(depth):
        lp = _prep_layer(li, P)
        mkv = _matmul(mem_prompt[0], lp["w_mem_kv"])
        mk, mv = mkv[:, :BR], mkv[:, BR:]
        mem_kp.append(mk.reshape(1, mem_len, MEM_HEADS, MEM_DIM))
        mem_vp.append(mv.reshape(1, mem_len, MEM_HEADS, MEM_DIM))
        st_p = {"ssd_conv": jnp.zeros((SUBLANES, SSD_CONV_DIM), F32),
                "ssd_h": jnp.zeros((1, SSD_HEADS, SSD_HEAD_DIM, SSD_STATE), F32),
                "gla_s": jnp.zeros((1, GLA_HEADS, GLA_DK, GLA_DV), F32),
                "ffn_conv": jnp.zeros((SUBLANES, ffn_dim), F32),
                "mem_k": mk[None], "mem_v": mv[None]}
        xp, o = _layer(xp, lp, li, 1, seq_len, st_p, None)
        dk, dv, c_rows, kr_rows, ssd_h, ssd_conv, gla_s, ffn_conv = o
        new_p.append((dk.reshape(1, seq_len, DIFF_KV, 2 * DIFF_DIM), dv.reshape(1, seq_len, DIFF_KV, 2 * DIFF_DIM),
                      c_rows[None], kr_rows[None], ssd_h, ssd_conv[None, SUBLANES - (SSD_CONV - 1):],
                      gla_s, ffn_conv[None, SUBLANES - (FFN_CONV - 1):]))
        st_s = {"ssd_conv": state_ssd_conv[li].transpose(1, 0, 2).reshape((SSD_CONV - 1) * db, SSD_CONV_DIM),
                "ssd_h": state_ssd[li],
                "gla_s": state_gla[li],
                "ffn_conv": state_ffn_conv[li].transpose(1, 0, 2).reshape((FFN_CONV - 1) * db, ffn_dim),
                "mem_k": mem_kt[li], "mem_v": mem_vt[li]}
        xs, o = _layer(xs, lp, li, db, steps, st_s, past)
        dk, dv, c_rows, kr_rows, ssd_h, ssd_conv, gla_s, ffn_conv = o
        bm = lambda a: a.reshape(-1, db, a.shape[-1]).transpose(1, 0, 2)
        new_s.append((bm(dk).reshape(db, steps, DIFF_KV, 2 * DIFF_DIM), bm(dv).reshape(db, steps, DIFF_KV, 2 * DIFF_DIM),
                      bm(c_rows), bm(kr_rows), ssd_h, bm(ssd_conv), gla_s, bm(ffn_conv)))
    stk = lambda outs, j: jnp.stack([o[j] for o in outs])
    yp = xp[None]
    ys = xs.reshape(steps, db, d_model).transpose(1, 0, 2)
    return (yp, ys,
            stk(new_p, 0), stk(new_p, 1), stk(new_p, 2), stk(new_p, 3),
            jnp.stack(mem_kp), jnp.stack(mem_vp),
            stk(new_p, 4), stk(new_p, 5), stk(new_p, 6), stk(new_p, 7),
            stk(new_s, 0), stk(new_s, 1), stk(new_s, 2), stk(new_s, 3),
            stk(new_s, 4), stk(new_s, 5), stk(new_s, 6), stk(new_s, 7))
```

```python
import functools
import math

import numpy as np
import jax
import jax.numpy as jnp
from jax import lax
from jax.experimental import pallas as pl
from jax.experimental.pallas import tpu as pltpu

F32 = jnp.float32
BF16 = jnp.bfloat16

BR = 256
N_BRANCH = 5
SSD_HEADS, SSD_HEAD_DIM, SSD_GROUPS, SSD_STATE, SSD_CONV = 4, 64, 2, 128, 4
SSD_CONV_DIM = BR + 2 * SSD_GROUPS * SSD_STATE
DIFF_HEADS, DIFF_KV, DIFF_DIM = 4, 2, 32
GLA_HEADS, GLA_DK, GLA_DV, GLA_RANK, GLA_TAU = 4, 32, 64, 16, 16.0
MLA_HEADS, MLA_QR, MLA_KVR, MLA_NOPE, MLA_ROPE, MLA_V = 4, 256, 128, 64, 32, 64
ROPE_THETA = 10000.0
MEM_HEADS, MEM_DIM = 4, 64
FFN_CONV = 3
PAGE = 128
EPS = 1e-6
NEG = -1e30

IN_WIDTHS = (BR, SSD_CONV_DIM, SSD_HEADS, DIFF_HEADS * 2 * DIFF_DIM, DIFF_KV * 2 * DIFF_DIM,
             DIFF_KV * 2 * DIFF_DIM, GLA_HEADS * GLA_DK, GLA_HEADS * GLA_DK, GLA_HEADS * GLA_DV,
             GLA_RANK, GLA_HEADS * GLA_DV, MLA_QR, MLA_KVR, MLA_ROPE, MEM_HEADS * MEM_DIM)
PROJ_WIDTHS = (256, 768, 128, 256, 128, 128, 128, 128, 256, 128, 256, 256, 128, 128, 256)

LANES = 128
SUBLANES = 8
VMEM_LIMIT = 56 * 1024 * 1024


def _cparams(sem):
    return pltpu.CompilerParams(dimension_semantics=sem, vmem_limit_bytes=VMEM_LIMIT)


def _dot(a, b):
    return jnp.dot(a, b, preferred_element_type=F32)


def _dot_nt(a, b):
    return lax.dot_general(a, b, (((1,), (1,)), ((), ())), preferred_element_type=F32)


def _dot_tn(a, b):
    return lax.dot_general(a, b, (((0,), (0,)), ((), ())), preferred_element_type=F32)


def _split3(x):
    hi = x.astype(BF16)
    r = x - hi.astype(F32)
    mid = r.astype(BF16)
    lo = (r - mid.astype(F32)).astype(BF16)
    return hi, mid, lo


def _xdot_l(m01, x):
    hi, mid, lo = _split3(x)
    return _dot(m01, hi) + _dot(m01, mid) + _dot(m01, lo)


def _xdot_r(x, m01):
    hi, mid, lo = _split3(x)
    return _dot(hi, m01) + _dot(mid, m01) + _dot(lo, m01)


def _xdot_nt(m01, x):
    hi, mid, lo = _split3(x)
    return _dot_nt(m01, hi) + _dot_nt(m01, mid) + _dot_nt(m01, lo)


def _rms(x, g):
    return x * lax.rsqrt(jnp.mean(x * x, axis=-1, keepdims=True) + EPS) * g


def _silu(x):
    return x * (1.0 / (1.0 + jnp.exp(-x)))


def _sigmoid(x):
    return 1.0 / (1.0 + jnp.exp(-x))


def _softplus(x):
    return jnp.maximum(x, 0.0) + jnp.log(1.0 + jnp.exp(-jnp.abs(x)))


def _log_sigmoid(x):
    return -_softplus(-x)


def _gelu_tanh(x):
    c = math.sqrt(2.0 / math.pi)
    return 0.5 * x * (1.0 + jnp.tanh(c * (x + 0.044715 * (x * x * x))))


def _lane_iota(shape):
    return lax.broadcasted_iota(jnp.int32, shape, len(shape) - 1)


def _row_iota(shape):
    return lax.broadcasted_iota(jnp.int32, shape, len(shape) - 2)


def _proj_in_kernel(x_ref, g_ref, w_ref, *o_refs, transposed):
    xn = _rms(x_ref[...], g_ref[...]).astype(BF16)
    off = 0
    for i, (o_ref, wd) in enumerate(zip(o_refs, PROJ_WIDTHS)):
        w = w_ref[off:off + wd, :]
        o_ref[...] = _dot_nt(w, xn) if i in transposed else _dot_nt(xn, w)
        off += wd


def _proj_in(x, g, w, tm, transposed=()):
    n, d = x.shape
    spec = lambda i, wd: (pl.BlockSpec((wd, tm), lambda r: (0, r)) if i in transposed
                          else pl.BlockSpec((tm, wd), lambda r: (r, 0)))
    shape = lambda i, wd: jax.ShapeDtypeStruct((wd, n) if i in transposed else (n, wd), F32)
    return pl.pallas_call(
        functools.partial(_proj_in_kernel, transposed=transposed),
        grid=(n // tm,),
        in_specs=[pl.BlockSpec((tm, d), lambda i: (i, 0)),
                  pl.BlockSpec((1, d), lambda i: (0, 0)),
                  pl.BlockSpec(w.shape, lambda i: (0, 0))],
        out_specs=[spec(i, wd) for i, wd in enumerate(PROJ_WIDTHS)],
        out_shape=[shape(i, wd) for i, wd in enumerate(PROJ_WIDTHS)],
        compiler_params=_cparams(("arbitrary",)),
        name="proj_in",
    )(x, g, w)


def _conv_taps(ext_ref, cur, w_ref, b_ref, taps, shift, pad, tm):
    acc = b_ref[...] + w_ref[taps - 1:taps, :] * cur
    for j in range(taps - 1):
        off = pad - (taps - 1 - j) * shift
        acc = acc + w_ref[j:j + 1, :] * ext_ref[off:off + tm, :]
    return acc


def _conv_silu_kernel(x_ref, st_ref, w_ref, b_ref, y_ref, ns_ref, ext_ref, *, taps, shift, pad):
    tm = x_ref.shape[0]

    @pl.when(pl.program_id(0) == 0)
    def _():
        ext_ref[0:pad, :] = st_ref[...]

    cur = x_ref[...]
    ext_ref[pad:pad + tm, :] = cur
    y_ref[...] = _silu(_conv_taps(ext_ref, cur, w_ref, b_ref, taps, shift, pad, tm))
    tail = ext_ref[tm:tm + pad, :]
    ns_ref[...] = tail
    ext_ref[0:pad, :] = tail


def _conv_silu(x, state, w, b, shift, tm):
    n, c = x.shape
    taps = w.shape[0]
    pad = state.shape[0]
    assert tm >= pad and pad >= (taps - 1) * shift
    return pl.pallas_call(
        functools.partial(_conv_silu_kernel, taps=taps, shift=shift, pad=pad),
        grid=(n // tm,),
        in_specs=[pl.BlockSpec((tm, c), lambda i: (i, 0)),
                  pl.BlockSpec((pad, c), lambda i: (0, 0)),
                  pl.BlockSpec((taps, c), lambda i: (0, 0)),
                  pl.BlockSpec((1, c), lambda i: (0, 0))],
        out_specs=[pl.BlockSpec((tm, c), lambda i: (i, 0)),
                   pl.BlockSpec((pad, c), lambda i: (0, 0))],
        out_shape=[jax.ShapeDtypeStruct((n, c), F32), jax.ShapeDtypeStruct((pad, c), F32)],
        scratch_shapes=[pltpu.VMEM((pad + tm, c), F32)],
        compiler_params=_cparams(("arbitrary",)),
        name="conv_silu",
    )(x, state, w, b)


def _ssd_kernel(z_ref, xbc_ref, dt_ref, h0_ref, prm_ref, dskip_ref, ng_ref, tri_ref, exp_ref, eye_ref,
                y_ref, hout_ref, st_ref, *, chunk, t_valid, n_chunks):
    L = chunk
    c = pl.program_id(1)

    @pl.when(c == 0)
    def _():
        st_ref[...] = h0_ref[0].reshape(SSD_HEADS * SSD_HEAD_DIM, SSD_STATE).T

    xbc = xbc_ref[0]
    xs = xbc[:, 0:BR]
    bm = xbc[:, BR:2 * BR].astype(BF16)
    cm = xbc[:, 2 * BR:3 * BR].astype(BF16)
    dt = _softplus(dt_ref[0] + prm_ref[0:1, :])
    if t_valid < L:
        dt = jnp.where(_row_iota((L, LANES)) < t_valid, dt, 0.0)
    d_a = dt * prm_ref[1:2, :]
    tri = tri_ref[...]
    expand = exp_ref[...]
    cs = _xdot_l(tri, d_a)
    cs_exp = _xdot_r(cs, expand)
    dt_exp = _xdot_r(dt, expand)
    cs_t = _xdot_nt(eye_ref[...], cs)
    xdt = xs * dt_exp
    xdt_b = xdt.astype(BF16)
    causal = _row_iota((L, L)) >= _lane_iota((L, L))
    lane_head = _lane_iota((L, BR)) // SSD_HEAD_DIM
    st = st_ref[...]
    st_b = st.astype(BF16)
    y = xs * dskip_ref[...]
    for g in range(SSD_GROUPS):
        cg = cm[:, g * SSD_STATE:(g + 1) * SSD_STATE]
        bg = bm[:, g * SSD_STATE:(g + 1) * SSD_STATE]
        cb = _dot_nt(cg, bg)
        for hh in range(SSD_HEADS // SSD_GROUPS):
            h = g * (SSD_HEADS // SSD_GROUPS) + hh
            dec = jnp.exp(jnp.where(causal, cs[:, h:h + 1] - cs_t[h:h + 1, :], NEG))
            yd = _dot((cb * dec).astype(BF16), xdt_b)
            y = y + jnp.where(lane_head == h, yd, 0.0)
    half = _lane_iota((L, BR)) < (BR // 2)
    y_off = jnp.where(half, _dot(cm[:, 0:SSD_STATE], st_b), _dot(cm[:, SSD_STATE:2 * SSD_STATE], st_b))
    y = y + y_off * jnp.exp(cs_exp)
    cs_last = cs_exp[L - 1:L, :]
    xw = (xdt * jnp.exp(cs_last - cs_exp)).astype(BF16)
    half_s = _lane_iota((SSD_STATE, BR)) < (BR // 2)
    upd = jnp.where(half_s, _dot_tn(bm[:, 0:SSD_STATE], xw), _dot_tn(bm[:, SSD_STATE:2 * SSD_STATE], xw))
    st_new = st * jnp.exp(cs_last) + upd
    st_ref[...] = st_new
    y_ref[0] = _rms(y * _silu(z_ref[0]), ng_ref[...])

    @pl.when(c == n_chunks - 1)
    def _():
        hout_ref[0] = st_new.T.reshape(SSD_HEADS, SSD_HEAD_DIM, SSD_STATE)


def _ssd_scan(z, xbc, dt_raw, h0, prm, dskip, ng, chunk, t_valid):
    b, t, _ = z.shape
    n_chunks = t // chunk
    tri = jnp.asarray(np.tril(np.ones((chunk, chunk), np.float32)), BF16)
    expand = np.zeros((LANES, BR), np.float32)
    for h in range(SSD_HEADS):
        expand[h, h * SSD_HEAD_DIM:(h + 1) * SSD_HEAD_DIM] = 1.0
    eye = np.eye(SUBLANES, LANES, dtype=np.float32)
    row = lambda bi, ci: (bi, ci, 0)
    fixed2 = lambda bi, ci: (0, 0)
    return pl.pallas_call(
        functools.partial(_ssd_kernel, chunk=chunk, t_valid=t_valid, n_chunks=n_chunks),
        grid=(b, n_chunks),
        in_specs=[pl.BlockSpec((1, chunk, BR), row),
                  pl.BlockSpec((1, chunk, SSD_CONV_DIM), row),
                  pl.BlockSpec((1, chunk, LANES), row),
                  pl.BlockSpec((1, SSD_HEADS, SSD_HEAD_DIM, SSD_STATE), lambda bi, ci: (bi, 0, 0, 0)),
                  pl.BlockSpec((SUBLANES, LANES), fixed2),
                  pl.BlockSpec((1, BR), fixed2),
                  pl.BlockSpec((1, BR), fixed2),
                  pl.BlockSpec((chunk, chunk), fixed2),
                  pl.BlockSpec((LANES, BR), fixed2),
                  pl.BlockSpec((SUBLANES, LANES), fixed2)],
        out_specs=[pl.BlockSpec((1, chunk, BR), row),
                   pl.BlockSpec((1, SSD_HEADS, SSD_HEAD_DIM, SSD_STATE), lambda bi, ci: (bi, 0, 0, 0))],
        out_shape=[jax.ShapeDtypeStruct((b, t, BR), F32),
                   jax.ShapeDtypeStruct((b, SSD_HEADS, SSD_HEAD_DIM, SSD_STATE), F32)],
        scratch_shapes=[pltpu.VMEM((SSD_STATE, BR), F32)],
        compiler_params=_cparams(("arbitrary", "arbitrary")),
        name="ssd_scan",
    )(z, xbc, dt_raw, h0, prm, dskip, ng, tri, jnp.asarray(expand, BF16), jnp.asarray(eye, BF16))


def _pad_lanes(v, width):
    return jnp.pad(v, [(0, 0)] * (v.ndim - 1) + [(0, width - v.shape[-1])])


def _to_seq_major(a, db, steps, pad_to):
    c = a.shape[-1]
    a = a.reshape(steps, db, c).transpose(1, 0, 2)
    return jnp.pad(a, ((0, 0), (0, pad_to - steps), (0, 0)))


def _to_time_major(a, steps):
    db, _, c = a.shape
    return a[:, :steps].transpose(1, 0, 2).reshape(steps * db, c)


def _ssd_branch(z, xbc_raw, dt_raw, conv_state, h0, lp, db, steps):
    n = z.shape[0]
    tm = min(n, 512)
    xbc, conv_new = _conv_silu(xbc_raw, conv_state, lp["ssd_conv_w"], lp["ssd_conv_b"], shift=db, tm=tm)
    if db == 1:
        chunk = min(steps, 128)
        y, h_new = _ssd_scan(z[None], xbc[None], dt_raw[None], h0, lp["ssd_prm"], lp["ssd_dskip"],
                             lp["ssd_norm"], chunk, chunk)
        return y[0], conv_new, h_new
    zs, xs, ds = (_to_seq_major(a, db, steps, SUBLANES) for a in (z, xbc, dt_raw))
    y, h_new = _ssd_scan(zs, xs, ds, h0, lp["ssd_prm"], lp["ssd_dskip"], lp["ssd_norm"], SUBLANES, steps)
    return _to_time_major(y, steps), conv_new, h_new


def _gla_consts(chunk):
    L = chunk
    nl = int(math.log2(L))
    assert 2 ** nl == L
    t = np.arange(L)
    mats, masks = [], []
    for lvl in range(nl):
        b = L >> (lvl + 1)
        same = (t[:, None] // b) == (t[None, :] // b)
        mats.append((same & (t[None, :] <= t[:, None])).astype(np.float32))
    for lvl in range(nl):
        b = L >> (lvl + 1)
        same = (t[:, None] // b) == (t[None, :] // b)
        mats.append((same & (t[None, :] > t[:, None])).astype(np.float32))
    mats.append((t[None, :] <= t[:, None]).astype(np.float32))
    mats.append((t[None, :] > t[:, None]).astype(np.float32))
    for lvl in range(nl):
        b = L >> (lvl + 1)
        tb, sb = t[:, None] // b, t[None, :] // b
        masks.append(((tb % 2 == 1) & (sb == tb - 1)).astype(np.float32))
    masks.append((t[:, None] == t[None, :]).astype(np.float32))
    masks = np.stack([np.tile(m, (GLA_HEADS, 1)) for m in masks])
    ones = np.zeros((BR, BR), np.float32)
    for h in range(GLA_HEADS):
        ones[h * GLA_DV:(h + 1) * GLA_DV, h * GLA_DV:(h + 1) * GLA_DV] = 1.0
    return (jnp.asarray(np.concatenate(mats, 0), BF16), jnp.asarray(masks, F32), jnp.asarray(ones, BF16), nl)


def _gla_kernel(q_ref, k_ref, v_ref, gg_ref, r_ref, s0_ref, wg2_ref, bg_ref, ng_ref, mall_ref, masks_ref,
                ones_ref, y_ref, sout_ref, st_ref, *, chunk, t_valid, n_chunks, nl):
    L = chunk
    c = pl.program_id(1)

    @pl.when(c == 0)
    def _():
        st_ref[...] = s0_ref[0]

    q = q_ref[0] * (GLA_DK ** -0.5)
    k = k_ref[0]
    g = _log_sigmoid(_dot(gg_ref[0].astype(BF16), wg2_ref[...]) + bg_ref[...]) * (1.0 / GLA_TAU)
    if t_valid < L:
        live = _row_iota((L, LANES)) < t_valid
        g = jnp.where(live, g, 0.0)
        k = jnp.where(live, k, 0.0)
    v_b = v_ref[0].astype(BF16)
    gsum = _xdot_l(mall_ref[...], g)
    lane_head = _lane_iota((L, LANES)) // GLA_DK

    def heads_on_rows(x):
        return jnp.concatenate([jnp.where(lane_head == h, x, 0.0) for h in range(GLA_HEADS)], axis=0).astype(BF16)

    att = masks_ref[nl] * _dot_nt(heads_on_rows(q), k.astype(BF16))
    for lvl in range(nl):
        qt = q * jnp.exp(gsum[lvl * L:(lvl + 1) * L])
        kt = k * jnp.exp(gsum[(nl + lvl) * L:(nl + lvl + 1) * L])
        att = att + masks_ref[lvl] * _dot_nt(heads_on_rows(qt), kt.astype(BF16))
    bc = gsum[2 * nl * L:(2 * nl + 1) * L]
    tail = gsum[(2 * nl + 1) * L:(2 * nl + 2) * L]
    st = st_ref[...]
    res = _dot(att.astype(BF16), v_b) + _dot_nt(heads_on_rows(q * jnp.exp(bc)), st.astype(BF16))
    out_head = _lane_iota((L, BR)) // GLA_DV
    o = jnp.zeros((L, BR), F32)
    for h in range(GLA_HEADS):
        o = o + jnp.where(out_head == h, res[h * L:(h + 1) * L], 0.0)
    st_new = st * jnp.exp(bc[L - 1:L, :]) + _dot_tn(v_b, (k * jnp.exp(tail)).astype(BF16))
    st_ref[...] = st_new
    msq = _xdot_r(o * o, ones_ref[...]) * (1.0 / GLA_DV)
    y_ref[0] = o * lax.rsqrt(msq + EPS) * ng_ref[...] * _silu(r_ref[0])

    @pl.when(c == n_chunks - 1)
    def _():
        sout_ref[0] = st_new


def _gla_scan(q, k, v, gg, r, s0, wg2, bg, ng, chunk, t_valid):
    b, t, _ = q.shape
    n_chunks = t // chunk
    mall, masks, ones, nl = _gla_consts(chunk)
    row = lambda bi, ci: (bi, ci, 0)
    fixed2 = lambda bi, ci: (0, 0)
    return pl.pallas_call(
        functools.partial(_gla_kernel, chunk=chunk, t_valid=t_valid, n_chunks=n_chunks, nl=nl),
        grid=(b, n_chunks),
        in_specs=[pl.BlockSpec((1, chunk, LANES), row),
                  pl.BlockSpec((1, chunk, LANES), row),
                  pl.BlockSpec((1, chunk, BR), row),
                  pl.BlockSpec((1, chunk, LANES), row),
                  pl.BlockSpec((1, chunk, BR), row),
                  pl.BlockSpec((1, BR, LANES), lambda bi, ci: (bi, 0, 0)),
                  pl.BlockSpec((LANES, LANES), fixed2),
                  pl.BlockSpec((1, LANES), fixed2),
                  pl.BlockSpec((1, BR), fixed2),
                  pl.BlockSpec(mall.shape, fixed2),
                  pl.BlockSpec(masks.shape, lambda bi, ci: (0, 0, 0)),
                  pl.BlockSpec((BR, BR), fixed2)],
        out_specs=[pl.BlockSpec((1, chunk, BR), row),
                   pl.BlockSpec((1, BR, LANES), lambda bi, ci: (bi, 0, 0))],
        out_shape=[jax.ShapeDtypeStruct((b, t, BR), F32), jax.ShapeDtypeStruct((b, BR, LANES), F32)],
        scratch_shapes=[pltpu.VMEM((BR, LANES), F32)],
        compiler_params=_cparams(("arbitrary", "arbitrary")),
        name="gla_scan",
    )(q, k, v, gg, r, s0, wg2, bg, ng, mall, masks, ones)


def _gla_state_in(s):
    blocks = [jnp.pad(s[:, h].transpose(0, 2, 1), ((0, 0), (0, 0), (h * GLA_DK, LANES - (h + 1) * GLA_DK)))
              for h in range(GLA_HEADS)]
    return jnp.concatenate(blocks, axis=1)


def _gla_state_out(st):
    return jnp.stack([st[:, h * GLA_DV:(h + 1) * GLA_DV, h * GLA_DK:(h + 1) * GLA_DK].transpose(0, 2, 1)
                      for h in range(GLA_HEADS)], axis=1)


def _gla_branch(q, k, v, gg, r, s0, lp, db, steps):
    st0 = _gla_state_in(s0)
    args = (lp["gla_wg2"], lp["gla_bg"], lp["gla_norm"])
    if db == 1:
        chunk = min(steps, 64)
        y, st = _gla_scan(q[None], k[None], v[None], gg[None], r[None], st0, *args, chunk, chunk)
        return y[0], _gla_state_out(st)
    qs, ks, vs, gs, rs = (_to_seq_major(a, db, steps, SUBLANES) for a in (q, k, v, gg, r))
    y, st = _gla_scan(qs, ks, vs, gs, rs, st0, *args, SUBLANES, steps)
    return _to_time_major(y, steps), _gla_state_out(st)


DIFF_MAPS = DIFF_HEADS * 2
POS_SPLIT = 128


def _diff_stack_q(q, rows):
    lane = _lane_iota((rows, LANES))
    blocks = []
    for kv in range(DIFF_KV):
        for g in range(DIFF_HEADS // DIFF_KV):
            qg = q[:, g * LANES:(g + 1) * LANES]
            for i in range(2):
                lo = kv * 2 * DIFF_DIM + i * DIFF_DIM
                blocks.append(jnp.where((lane >= lo) & (lane < lo + DIFF_DIM), qg, 0.0))
    return jnp.concatenate(blocks, axis=0).astype(BF16)


LOG2E = math.log2(math.e)
DIFF_QSCALE = DIFF_DIM ** -0.5 * LOG2E


def _diff_slope(blk):
    head = blk // 2
    return 2.0 ** (-8.0 * (head + 1) / DIFF_HEADS) * LOG2E


def _bf16_parts(c):
    parts, r = [], np.float32(c)
    for _ in range(3):
        p = np.float32(np.asarray(r).astype(BF16))
        parts.append(float(p))
        r = np.float32(r - p)
    return parts


def _online_update(s, blk, rows, m_ref, l_ref, p_ref):
    r0 = blk * rows
    m_prev = m_ref[r0:r0 + rows, :]
    m_new = jnp.maximum(m_prev, jnp.max(s, axis=1, keepdims=True))
    alpha = jnp.exp2(m_prev - m_new)
    p = jnp.exp2(s - m_new[:, 0:1])
    l_ref[r0:r0 + rows, :] = alpha * l_ref[r0:r0 + rows, :] + jnp.sum(p, axis=1, keepdims=True)
    m_ref[r0:r0 + rows, :] = m_new
    p_ref[r0:r0 + rows, 0:s.shape[1]] = p.astype(p_ref.dtype)
    return alpha


def _diff_finish(acc_ref, l_ref, prm_ref, rows):
    lane = _lane_iota((rows, LANES))
    low = lane < 2 * DIFF_DIM
    lam = prm_ref[1:2, :]
    outs = []
    for g in range(DIFF_HEADS // DIFF_KV):
        parts = []
        for kv in range(DIFF_KV):
            b1 = ((kv * 2 + g) * 2) * rows
            b2 = b1 + rows
            o1 = acc_ref[b1:b1 + rows, :] / l_ref[b1:b1 + rows, :]
            o2 = acc_ref[b2:b2 + rows, :] / l_ref[b2:b2 + rows, :]
            parts.append(o1 - lam * o2)
        og = jnp.where(low, parts[0], parts[1])
        sq = og * og
        ss = jnp.where(low, jnp.sum(jnp.where(low, sq, 0.0), axis=1, keepdims=True),
                       jnp.sum(jnp.where(low, 0.0, sq), axis=1, keepdims=True))
        outs.append(og * lax.rsqrt(ss * (1.0 / (2 * DIFF_DIM)) + EPS) * prm_ref[0:1, :])
    return outs


def _diff_stack_q_alibi(q, rows):
    row = _row_iota((LANES, rows))
    blocks = []
    for kv in range(DIFF_KV):
        for g in range(DIFF_HEADS // DIFF_KV):
            qg = q[g * LANES:(g + 1) * LANES, :]
            for i in range(2):
                lo = kv * 2 * DIFF_DIM + i * DIFF_DIM
                parts = _bf16_parts(_diff_slope((kv * 2 + g) * 2 + i))
                qm = jnp.where((row >= lo) & (row < lo + DIFF_DIM), qg, 0.0)
                aug = jnp.zeros((LANES, rows), F32)
                for j, part in enumerate(parts):
                    aug = jnp.where(row == j, part * POS_SPLIT, jnp.where(row == len(parts) + j, part, aug))
                blocks.append(jnp.concatenate([qm, aug], axis=0))
    return jnp.concatenate(blocks, axis=1).astype(BF16)


FLASH_COLS = 2 * LANES


def _flash_tiles(k_ref, vt_ref, qs_ref, m_ref, acc_ref, q0, tq, tk):
    width = qs_ref.shape[1]

    def kv_tile(ki, masked):
        k = k_ref[pl.ds(pl.multiple_of(ki * tk, tk), tk), :]
        vt = vt_ref[ki]
        m_all = m_ref[...]
        ms, accs = [], []
        starts = list(range(0, width, FLASH_COLS))
        s_next = _dot(k, qs_ref[:, 0:FLASH_COLS])
        for idx, c0 in enumerate(starts):
            cols = slice(c0, c0 + FLASH_COLS)
            s = s_next
            if idx + 1 < len(starts):
                s_next = _dot(k, qs_ref[:, c0 + FLASH_COLS:c0 + 2 * FLASH_COLS])
            if masked:
                kpos = ki * tk + _row_iota((tk, FLASH_COLS))
                qpos = q0 + (c0 + _lane_iota((tk, FLASH_COLS))) % tq
                s = jnp.where(kpos <= qpos, s, NEG)
            m_prev = m_all[:, cols]
            m_new = jnp.maximum(m_prev, jnp.max(s, axis=0, keepdims=True))
            p = jnp.exp2(s - m_new).astype(BF16)
            ms.append(m_new)
            accs.append(jnp.exp2(m_prev - m_new) * acc_ref[:, cols] + _dot(vt, p))
        m_ref[...] = jnp.concatenate(ms, axis=1)
        acc_ref[...] = jnp.concatenate(accs, axis=1)

    m_ref[...] = jnp.full(m_ref.shape, NEG, F32)
    acc_ref[...] = jnp.zeros(acc_ref.shape, F32)
    n_full = q0 // tk

    def body(ki, carry):
        kv_tile(ki, False)
        return carry

    lax.fori_loop(0, n_full, body, 0)
    kv_tile(n_full, True)


def _diff_flash_kernel(q_ref, k_ref, vt_ref, lam_ref, gcol_ref, o_ref, qs_ref, m_ref, acc_ref, *, tq, tk):
    q0 = pl.program_id(0) * tq
    qs_ref[...] = _diff_stack_q_alibi(q_ref[...] * DIFF_QSCALE, tq)
    _flash_tiles(k_ref, vt_ref, qs_ref, m_ref, acc_ref, q0, tq, tk)
    row = _row_iota((LANES, tq))
    low = row < 2 * DIFF_DIM
    lam = lam_ref[...]
    for g in range(DIFF_HEADS // DIFF_KV):
        parts = []
        for kv in range(DIFF_KV):
            c1 = ((kv * 2 + g) * 2) * tq
            c2 = c1 + tq
            o1 = acc_ref[0:LANES, c1:c1 + tq] / acc_ref[LANES:LANES + 1, c1:c1 + tq]
            o2 = acc_ref[0:LANES, c2:c2 + tq] / acc_ref[LANES:LANES + 1, c2:c2 + tq]
            parts.append(o1 - lam * o2)
        og = jnp.where(low, parts[0], parts[1])
        sq = og * og
        ss = jnp.where(low, jnp.sum(jnp.where(low, sq, 0.0), axis=0, keepdims=True),
                       jnp.sum(jnp.where(low, 0.0, sq), axis=0, keepdims=True))
        o_ref[g * LANES:(g + 1) * LANES, :] = og * lax.rsqrt(ss * (1.0 / (2 * DIFF_DIM)) + EPS) * gcol_ref[...]


def _resident(shape):
    return pl.BlockSpec(shape, lambda *_: (0,) * len(shape), pipeline_mode=pl.Buffered(1))


def _value_tiles_t(v, tk):
    n, c = v.shape
    tiles = v.reshape(n // tk, tk, c).transpose(0, 2, 1)
    extra = jnp.zeros((n // tk, SUBLANES, tk), v.dtype).at[:, 0, :].set(1.0)
    return jnp.concatenate([tiles, extra], axis=1)


def _diff_flash(q_t, k_aug, v_t, lam_row, gcol, tq, tk):
    n = q_t.shape[1]
    assert tk % tq == 0 and n % tk == 0 and tq % FLASH_COLS == 0
    width = DIFF_MAPS * tq
    return pl.pallas_call(
        functools.partial(_diff_flash_kernel, tq=tq, tk=tk),
        grid=(n // tq,),
        in_specs=[pl.BlockSpec((BR, tq), lambda qi: (0, qi)),
                  _resident(k_aug.shape),
                  _resident(v_t.shape),
                  _resident((1, tq)),
                  _resident((LANES, tq))],
        out_specs=pl.BlockSpec((BR, tq), lambda qi: (0, qi)),
        out_shape=jax.ShapeDtypeStruct((BR, n), F32),
        scratch_shapes=[pltpu.VMEM((2 * LANES, width), BF16),
                        pltpu.VMEM((1, width), F32),
                        pltpu.VMEM((LANES + SUBLANES, width), F32)],
        compiler_params=_cparams(("arbitrary",)),
        name="diff_flash",
    )(q_t, k_aug, v_t, lam_row, gcol)


def _diff_keys_alibi(k, pos):
    assert k.shape[0] <= POS_SPLIT * 256
    hi, lo = pos // POS_SPLIT, pos % POS_SPLIT
    cols = jnp.stack([hi, hi, hi, lo, lo, lo], axis=1).astype(BF16)
    return jnp.concatenate([k.astype(BF16), _pad_lanes(cols, LANES)], axis=1)


def _page_copies(pt_ref, caches, bufs, sems, layer, b, c, slot, g, pages_per_step):
    page = pt_ref[b, c * pages_per_step + g]
    off = pl.multiple_of(g * PAGE, PAGE)
    copies = []
    for a, (cache, buf) in enumerate(zip(caches, bufs)):
        key_minor = buf.shape[2] == pages_per_step * PAGE
        dst = buf.at[slot, :, pl.ds(off, PAGE)] if key_minor else buf.at[slot, pl.ds(off, PAGE), :]
        copies.append(pltpu.make_async_copy(cache.at[layer, page], dst, sems.at[slot, a]))
    return copies


def _stream_pages(pt_ref, caches, bufs, sems, layer, pages_per_step, n_chunks):
    G = pages_per_step
    b, c = pl.program_id(0), pl.program_id(1)
    step = b * n_chunks + c
    last = step + 1 == pl.num_programs(0) * n_chunks
    slot = step % 2
    wrap = c + 1 == n_chunks
    nb = jnp.where(last, b, jnp.where(wrap, b + 1, b))
    nc = jnp.where(last, c, jnp.where(wrap, 0, c + 1))

    def for_pages(fn):
        def body(g, carry):
            fn(g)
            return carry
        lax.fori_loop(0, G, body, 0)

    def start(bb, cc, sl, g):
        for cp in _page_copies(pt_ref, caches, bufs, sems, layer, bb, cc, sl, g, G):
            cp.start()

    def wait(bb, cc, sl, g):
        for cp in _page_copies(pt_ref, caches, bufs, sems, layer, bb, cc, sl, g, G):
            cp.wait()

    @pl.when(step == 0)
    def _():
        for_pages(lambda g: start(b, c, slot, g))

    for_pages(lambda g: wait(b, c, slot, g))
    for_pages(lambda g: start(nb, nc, 1 - slot, g))

    def drain():
        @pl.when(last)
        def _():
            for_pages(lambda g: wait(nb, nc, 1 - slot, g))

    return slot, drain


def _diff_decode_kernel(pt_ref, q_ref, kn_ref, vn_ref, prm_ref, kc_ref, vc_ref, o_ref,
                        kbuf, vbuf, sems, qs_ref, m_ref, l_ref, acc_ref, p_ref,
                        *, layer, pages_per_step, n_chunks, steps, past_len):
    G = pages_per_step
    R8 = SUBLANES
    c = pl.program_id(1)
    slot, drain = _stream_pages(pt_ref, (kc_ref, vc_ref), (kbuf, vbuf), sems, layer, G, n_chunks)

    @pl.when(c == 0)
    def _():
        qs_ref[...] = _diff_stack_q(q_ref[0] * DIFF_QSCALE, R8)
        m_ref[...] = jnp.full(m_ref.shape, NEG, F32)
        l_ref[...] = jnp.zeros(l_ref.shape, F32)
        acc_ref[...] = jnp.zeros(acc_ref.shape, F32)

    qs = qs_ref[...]
    width = G * PAGE
    s = _dot(qs, kbuf[slot].astype(BF16))
    ndist = ((c * width + _lane_iota((R8, width))) - (past_len + _row_iota((R8, width)))).astype(F32)
    for blk in range(DIFF_MAPS):
        sb = s[blk * R8:(blk + 1) * R8] + _diff_slope(blk) * ndist
        alpha = _online_update(sb, blk, R8, m_ref, l_ref, p_ref)
        acc_ref[blk * R8:(blk + 1) * R8, :] = alpha * acc_ref[blk * R8:(blk + 1) * R8, :]
    acc_ref[...] = acc_ref[...] + _dot_nt(p_ref[...].astype(BF16), vbuf[slot].astype(BF16))
    drain()

    @pl.when(c == n_chunks - 1)
    def _():
        pad = jnp.zeros((PAGE - R8, LANES), F32)
        kn = jnp.concatenate([kn_ref[0], pad], axis=0).astype(BF16)
        vn = jnp.concatenate([vn_ref[0], pad], axis=0).astype(BF16)
        sn = _dot_nt(qs, kn)
        kt = _lane_iota((R8, PAGE))
        qt = _row_iota((R8, PAGE))
        vis = (kt <= qt) & (kt < steps)
        nd = (kt - qt).astype(F32)
        for blk in range(DIFF_MAPS):
            sb = jnp.where(vis, sn[blk * R8:(blk + 1) * R8] + _diff_slope(blk) * nd, NEG)
            alpha = _online_update(sb, blk, R8, m_ref, l_ref, p_ref)
            acc_ref[blk * R8:(blk + 1) * R8, :] = alpha * acc_ref[blk * R8:(blk + 1) * R8, :]
        acc_ref[...] = acc_ref[...] + _dot(p_ref[:, 0:PAGE].astype(BF16), vn)
        outs = _diff_finish(acc_ref, l_ref, prm_ref, R8)
        o_ref[0, :, 0:LANES] = outs[0]
        o_ref[0, :, LANES:2 * LANES] = outs[1]


def _diff_decode(page_table, q, k_new, v_new, prm, kc, vc, layer, steps, pages_per_step):
    db, n_pages = page_table.shape
    G = pages_per_step
    n_chunks = n_pages // G
    rows = DIFF_MAPS * SUBLANES
    seq3 = lambda b, c, pt: (b, 0, 0)
    grid_spec = pltpu.PrefetchScalarGridSpec(
        num_scalar_prefetch=1,
        grid=(db, n_chunks),
        in_specs=[pl.BlockSpec((1, SUBLANES, BR), seq3),
                  pl.BlockSpec((1, SUBLANES, LANES), seq3),
                  pl.BlockSpec((1, SUBLANES, LANES), seq3),
                  pl.BlockSpec((SUBLANES, LANES), lambda b, c, pt: (0, 0)),
                  pl.BlockSpec(memory_space=pl.ANY),
                  pl.BlockSpec(memory_space=pl.ANY)],
        out_specs=pl.BlockSpec((1, SUBLANES, BR), seq3),
        scratch_shapes=[pltpu.VMEM((2, LANES, G * PAGE), F32),
                        pltpu.VMEM((2, LANES, G * PAGE), F32),
                        pltpu.SemaphoreType.DMA((2, 2)),
                        pltpu.VMEM((rows, LANES), BF16),
                        pltpu.VMEM((rows, LANES), F32),
                        pltpu.VMEM((rows, LANES), F32),
                        pltpu.VMEM((rows, LANES), F32),
                        pltpu.VMEM((rows, G * PAGE), F32)])
    return pl.pallas_call(
        functools.partial(_diff_decode_kernel, layer=layer, pages_per_step=G, n_chunks=n_chunks,
                          steps=steps, past_len=n_pages * PAGE),
        grid_spec=grid_spec,
        out_shape=jax.ShapeDtypeStruct((db, SUBLANES, BR), F32),
        compiler_params=_cparams(("arbitrary", "arbitrary")),
        name="diff_decode",
    )(page_table, q, k_new, v_new, prm, kc, vc)


MLA_QW = 2 * LANES
MLA_SCALE = (MLA_NOPE + MLA_ROPE) ** -0.5 * LOG2E
ROPE_HALF = MLA_ROPE // 2


def _mla_prep_kernel(cq_ref, ckv_ref, kr_ref, cos_ref, sin_ref, qn_ref, kvn_ref, wuq_ref, wcat_ref,
                     qcat_ref, kcat_ref, c_ref, krout_ref, *, q_transposed):
    tm = cq_ref.shape[0]
    lane = _lane_iota((tm, LANES))
    cos = cos_ref[...]
    sin = sin_ref[...]
    q = _dot(_rms(cq_ref[...], qn_ref[...]).astype(BF16), wuq_ref[...])
    n_nope = MLA_HEADS * MLA_NOPE
    rq = q[:, n_nope:n_nope + LANES]
    rq = rq * cos + pltpu.roll(rq, LANES // 2, 1) * jnp.where(lane < LANES // 2, -sin, sin)
    q_in = (jnp.concatenate([q[:, 0:n_nope], rq], axis=1) * MLA_SCALE).astype(BF16)
    wcat_t = wcat_ref[...]
    qcat_ref[...] = _dot_nt(wcat_t, q_in) if q_transposed else _dot_nt(q_in, wcat_t)
    c = _rms(ckv_ref[...], kvn_ref[...])
    c_ref[...] = c
    kr = kr_ref[...]
    swapped = jnp.where(lane < ROPE_HALF, pltpu.roll(kr, LANES - ROPE_HALF, 1), pltpu.roll(kr, ROPE_HALF, 1))
    kro = kr * cos + swapped * jnp.where(lane < ROPE_HALF, -sin, sin)
    kro = jnp.where(lane < MLA_ROPE, kro, 0.0)
    krout_ref[...] = kro[:, 0:MLA_ROPE]
    kcat_ref[...] = jnp.concatenate([c, kro], axis=1)


def _mla_prep(cq, ckv, kr, cos, sin, lp, tm, q_transposed):
    n = cq.shape[0]
    qw = MLA_HEADS * MLA_QW
    row = lambda i: (i, 0)
    fixed = lambda i: (0, 0)
    q_spec = pl.BlockSpec((qw, tm), lambda i: (0, i)) if q_transposed else pl.BlockSpec((tm, qw), row)
    return pl.pallas_call(
        functools.partial(_mla_prep_kernel, q_transposed=q_transposed),
        grid=(n // tm,),
        in_specs=[pl.BlockSpec((tm, MLA_QR), row), pl.BlockSpec((tm, LANES), row), pl.BlockSpec((tm, LANES), row),
                  pl.BlockSpec((tm, LANES), row), pl.BlockSpec((tm, LANES), row),
                  pl.BlockSpec((1, MLA_QR), fixed), pl.BlockSpec((1, LANES), fixed),
                  pl.BlockSpec(lp["mla_wuq"].shape, fixed), pl.BlockSpec(lp["mla_wcat_t"].shape, fixed)],
        out_specs=[q_spec, pl.BlockSpec((tm, MLA_QW), row),
                   pl.BlockSpec((tm, LANES), row), pl.BlockSpec((tm, MLA_ROPE), row)],
        out_shape=[jax.ShapeDtypeStruct((qw, n) if q_transposed else (n, qw), F32),
                   jax.ShapeDtypeStruct((n, MLA_QW), F32),
                   jax.ShapeDtypeStruct((n, LANES), F32), jax.ShapeDtypeStruct((n, MLA_ROPE), F32)],
        compiler_params=_cparams(("arbitrary",)),
        name="mla_prep",
    )(cq, ckv, kr, cos, sin, lp["mla_q_norm"], lp["mla_kv_norm"], lp["mla_wuq"], lp["mla_wcat_t"])


def _mla_stack_q(qcat):
    return jnp.concatenate([qcat[:, h * MLA_QW:(h + 1) * MLA_QW] for h in range(MLA_HEADS)], axis=0).astype(BF16)


def _mla_finish(acc_ref, l_ref, wuv_ref, rows):
    y = jnp.zeros((rows, BR), F32)
    for h in range(MLA_HEADS):
        o_lat = acc_ref[h * rows:(h + 1) * rows, :] / l_ref[h * rows:(h + 1) * rows, :]
        y = y + _dot(o_lat.astype(BF16), wuv_ref[h])
    return y


def _mla_flash_kernel(q_ref, k_ref, ct_ref, wuvt_ref, o_ref, qs_ref, m_ref, acc_ref, *, tq, tk):
    q0 = pl.program_id(0) * tq
    qs_ref[...] = jnp.concatenate([q_ref[h * MLA_QW:(h + 1) * MLA_QW, :] for h in range(MLA_HEADS)],
                                  axis=1).astype(BF16)
    _flash_tiles(k_ref, ct_ref, qs_ref, m_ref, acc_ref, q0, tq, tk)
    y = jnp.zeros((BR, tq), F32)
    for h in range(MLA_HEADS):
        cols = slice(h * tq, (h + 1) * tq)
        o_lat = acc_ref[0:MLA_KVR, cols] / acc_ref[MLA_KVR:MLA_KVR + 1, cols]
        y = y + _dot(wuvt_ref[h], o_lat.astype(BF16))
    o_ref[...] = y


def _mla_flash(qcat_t, kcat, c_t, wuv_t, tq, tk):
    n = qcat_t.shape[1]
    assert tk % tq == 0 and n % tk == 0 and tq % FLASH_COLS == 0
    width = MLA_HEADS * tq
    return pl.pallas_call(
        functools.partial(_mla_flash_kernel, tq=tq, tk=tk),
        grid=(n // tq,),
        in_specs=[pl.BlockSpec((MLA_HEADS * MLA_QW, tq), lambda qi: (0, qi)),
                  _resident(kcat.shape),
                  _resident(c_t.shape),
                  _resident(wuv_t.shape)],
        out_specs=pl.BlockSpec((BR, tq), lambda qi: (0, qi)),
        out_shape=jax.ShapeDtypeStruct((BR, n), F32),
        scratch_shapes=[pltpu.VMEM((MLA_QW, width), BF16),
                        pltpu.VMEM((1, width), F32),
                        pltpu.VMEM((MLA_KVR + SUBLANES, width), F32)],
        compiler_params=_cparams(("arbitrary",)),
        name="mla_flash",
    )(qcat_t, kcat, c_t, wuv_t)


def _mla_decode_kernel(pt_ref, q_ref, kn_ref, wuv_ref, cc_ref, rc_ref, o_ref,
                       cbuf, rbuf, sems, qs_ref, m_ref, l_ref, acc_ref, p_ref,
                       *, layer, pages_per_step, n_chunks, steps):
    G = pages_per_step
    R8 = SUBLANES
    c = pl.program_id(1)
    slot, drain = _stream_pages(pt_ref, (cc_ref, rc_ref), (cbuf, rbuf), sems, layer, G, n_chunks)

    @pl.when(c == 0)
    def _():
        qs_ref[...] = _mla_stack_q(q_ref[0])
        m_ref[...] = jnp.full(m_ref.shape, NEG, F32)
        l_ref[...] = jnp.zeros(l_ref.shape, F32)
        acc_ref[...] = jnp.zeros(acc_ref.shape, F32)

    qs = qs_ref[...]
    q_lat = qs[:, 0:MLA_KVR]
    q_rope = qs[:, MLA_KVR:MLA_KVR + MLA_ROPE]

    cb = cbuf[slot].astype(BF16)
    s = _dot_nt(q_lat, cb) + _dot(q_rope, rbuf[slot].astype(BF16))
    for h in range(MLA_HEADS):
        alpha = _online_update(s[h * R8:(h + 1) * R8], h, R8, m_ref, l_ref, p_ref)
        acc_ref[h * R8:(h + 1) * R8, :] = alpha * acc_ref[h * R8:(h + 1) * R8, :]
    acc_ref[...] = acc_ref[...] + _dot(p_ref[...].astype(BF16), cb)
    drain()

    @pl.when(c == n_chunks - 1)
    def _():
        kn = jnp.concatenate([kn_ref[0], jnp.zeros((PAGE - R8, MLA_QW), F32)], axis=0).astype(BF16)
        sn = _dot_nt(qs, kn)
        kt = _lane_iota((R8, PAGE))
        vis = (kt <= _row_iota((R8, PAGE))) & (kt < steps)
        for h in range(MLA_HEADS):
            alpha = _online_update(jnp.where(vis, sn[h * R8:(h + 1) * R8], NEG), h, R8, m_ref, l_ref, p_ref)
            acc_ref[h * R8:(h + 1) * R8, :] = alpha * acc_ref[h * R8:(h + 1) * R8, :]
        acc_ref[...] = acc_ref[...] + _dot(p_ref[:, 0:PAGE].astype(BF16), kn[:, 0:MLA_KVR])
        o_ref[0] = _mla_finish(acc_ref, l_ref, wuv_ref, R8)


def _mla_decode(page_table, qcat, kcat_new, wuv, cc, rc, layer, steps, pages_per_step):
    db, n_pages = page_table.shape
    G = pages_per_step
    n_chunks = n_pages // G
    rows = MLA_HEADS * SUBLANES
    seq3 = lambda b, c, pt: (b, 0, 0)
    grid_spec = pltpu.PrefetchScalarGridSpec(
        num_scalar_prefetch=1,
        grid=(db, n_chunks),
        in_specs=[pl.BlockSpec((1, SUBLANES, MLA_HEADS * MLA_QW), seq3),
                  pl.BlockSpec((1, SUBLANES, MLA_QW), seq3),
                  pl.BlockSpec(wuv.shape, lambda b, c, pt: (0, 0, 0)),
                  pl.BlockSpec(memory_space=pl.ANY),
                  pl.BlockSpec(memory_space=pl.ANY)],
        out_specs=pl.BlockSpec((1, SUBLANES, BR), seq3),
        scratch_shapes=[pltpu.VMEM((2, G * PAGE, MLA_KVR), F32),
                        pltpu.VMEM((2, MLA_ROPE, G * PAGE), F32),
                        pltpu.SemaphoreType.DMA((2, 2)),
                        pltpu.VMEM((rows, MLA_QW), BF16),
                        pltpu.VMEM((rows, LANES), F32),
                        pltpu.VMEM((rows, LANES), F32),
                        pltpu.VMEM((rows, LANES), F32),
                        pltpu.VMEM((rows, G * PAGE), F32)])
    return pl.pallas_call(
        functools.partial(_mla_decode_kernel, layer=layer, pages_per_step=G, n_chunks=n_chunks, steps=steps),
        grid_spec=grid_spec,
        out_shape=jax.ShapeDtypeStruct((db, SUBLANES, BR), F32),
        compiler_params=_cparams(("arbitrary", "arbitrary")),
        name="mla_decode",
    )(page_table, qcat, kcat_new, wuv, cc, rc)


def _mem_attn_kernel(q_ref, mk_ref, mv_ref, o_ref, *, key_minor):
    rows = q_ref.shape[1]
    lane_head = _lane_iota((rows, BR)) // MEM_DIM
    q = q_ref[0] * (MEM_DIM ** -0.5)
    qs = jnp.concatenate([jnp.where(lane_head == h, q, 0.0) for h in range(MEM_HEADS)], axis=0).astype(BF16)
    mk = mk_ref[0].astype(BF16)
    mv = mv_ref[0].astype(BF16)
    s = _dot(qs, mk) if key_minor else _dot_nt(qs, mk)
    p = jnp.exp(s - jnp.max(s, axis=1, keepdims=True))
    p = (p / jnp.sum(p, axis=1, keepdims=True)).astype(BF16)
    r = _dot_nt(p, mv) if key_minor else _dot(p, mv)
    y = jnp.zeros((rows, BR), F32)
    for h in range(MEM_HEADS):
        y = y + jnp.where(lane_head == h, r[h * rows:(h + 1) * rows], 0.0)
    o_ref[0] = y


def _mem_attn(q3, mk3, mv3, key_minor):
    b, rows, _ = q3.shape
    per_seq = mk3.shape[0] == b and b > 1
    kv_map = (lambda i: (i, 0, 0)) if per_seq else (lambda i: (0, 0, 0))
    return pl.pallas_call(
        functools.partial(_mem_attn_kernel, key_minor=key_minor),
        grid=(b,),
        in_specs=[pl.BlockSpec((1, rows, BR), lambda i: (i, 0, 0)),
                  pl.BlockSpec((1,) + mk3.shape[1:], kv_map),
                  pl.BlockSpec((1,) + mv3.shape[1:], kv_map)],
        out_specs=pl.BlockSpec((1, rows, BR), lambda i: (i, 0, 0)),
        out_shape=jax.ShapeDtypeStruct((b, rows, BR), F32),
        compiler_params=_cparams(("arbitrary",)),
        name="mem_attn",
    )(q3, mk3, mv3)


def _matmul_kernel(x_ref, w_ref, o_ref):
    o_ref[...] = _dot(x_ref[...].astype(BF16), w_ref[...])


def _matmul(x, w):
    m, k = x.shape
    n = w.shape[1]
    return pl.pallas_call(
        _matmul_kernel,
        grid=(1,),
        in_specs=[pl.BlockSpec((m, k), lambda i: (0, 0)), pl.BlockSpec((k, n), lambda i: (0, 0))],
        out_specs=pl.BlockSpec((m, n), lambda i: (0, 0)),
        out_shape=jax.ShapeDtypeStruct((m, n), F32),
        compiler_params=_cparams(("arbitrary",)),
        name="mem_proj",
    )(x, w)


def _merge_kernel(x_ref, y0, y1, y2, y3, y4, gpre_ref, gpost_ref, wg_ref, wb_ref, wo_ref, o_ref, *, transposed):
    x = x_ref[...]
    d = x.shape[1]
    hn = _rms(x, gpre_ref[...]).astype(BF16)
    m = jnp.zeros(x.shape, F32)
    for i, y_ref in enumerate((y0, y1, y2, y3, y4)):
        gate = _sigmoid(_dot_nt(hn, wg_ref[i * d:(i + 1) * d, :]))
        y = y_ref[...].astype(BF16)
        m = m + gate * (_dot_tn(y, wb_ref[i]) if i in transposed else _dot(y, wb_ref[i]))
    o_ref[...] = x + _rms(_dot(m.astype(BF16), wo_ref[...]), gpost_ref[...])


def _merge_out(x, ys, lp, tm, transposed=()):
    n, d = x.shape
    row = lambda i: (i, 0)
    fixed = lambda i: (0, 0)
    y_specs = [pl.BlockSpec((BR, tm), lambda i: (0, i)) if j in transposed else pl.BlockSpec((tm, BR), row)
               for j in range(N_BRANCH)]
    return pl.pallas_call(
        functools.partial(_merge_kernel, transposed=transposed),
        grid=(n // tm,),
        in_specs=[pl.BlockSpec((tm, d), row)] + y_specs + [
            pl.BlockSpec((1, d), fixed), pl.BlockSpec((1, d), fixed),
            pl.BlockSpec(lp["w_gate"].shape, fixed),
            pl.BlockSpec(lp["w_branch"].shape, lambda i: (0, 0, 0)),
            pl.BlockSpec(lp["w_out"].shape, fixed)],
        out_specs=pl.BlockSpec((tm, d), row),
        out_shape=jax.ShapeDtypeStruct((n, d), F32),
        compiler_params=_cparams(("arbitrary",)),
        name="merge_out",
    )(x, *ys, lp["g_pre_mix"], lp["g_post_mix"], lp["w_gate"], lp["w_branch"], lp["w_out"])


FFN_CHUNK = 2 * LANES


def _ffn_kernel(x_ref, st_ref, gpre_ref, gpost_ref, wi_ref, cw_ref, cb_ref, wo_ref, o_ref, ns_ref,
                ext_ref, acc_ref, *, shift, pad, width):
    tm = x_ref.shape[0]

    @pl.when(pl.program_id(0) == 0)
    def _():
        ext_ref[0:pad, :] = st_ref[...]

    x = x_ref[...]
    hn = _rms(x, gpre_ref[...]).astype(BF16)
    for c0 in range(0, width, FFN_CHUNK):
        a = _dot(hn, wi_ref[:, c0:c0 + FFN_CHUNK])
        gate = _dot(hn, wi_ref[:, width + c0:width + c0 + FFN_CHUNK])
        ext_ref[pad:pad + tm, c0:c0 + FFN_CHUNK] = a
        conv = cb_ref[:, c0:c0 + FFN_CHUNK] + cw_ref[FFN_CONV - 1:FFN_CONV, c0:c0 + FFN_CHUNK] * a
        for j in range(FFN_CONV - 1):
            off = pad - (FFN_CONV - 1 - j) * shift
            conv = conv + cw_ref[j:j + 1, c0:c0 + FFN_CHUNK] * ext_ref[off:off + tm, c0:c0 + FFN_CHUNK]
        act = (_gelu_tanh(conv) * gate).astype(BF16)
        contrib = _dot(act, wo_ref[c0:c0 + FFN_CHUNK, :])
        if c0 == 0:
            acc_ref[...] = contrib
        else:
            acc_ref[...] = acc_ref[...] + contrib
    o_ref[...] = x + _rms(acc_ref[...], gpost_ref[...])
    tail = ext_ref[tm:tm + pad, :]
    ns_ref[...] = tail
    ext_ref[0:pad, :] = tail


def _ffn(x, state, lp, shift, tm):
    n, d = x.shape
    pad, width = state.shape
    assert tm >= pad and pad >= (FFN_CONV - 1) * shift and width % FFN_CHUNK == 0
    row = lambda i: (i, 0)
    fixed = lambda i: (0, 0)
    return pl.pallas_call(
        functools.partial(_ffn_kernel, shift=shift, pad=pad, width=width),
        grid=(n // tm,),
        in_specs=[pl.BlockSpec((tm, d), row), pl.BlockSpec((pad, width), fixed),
                  pl.BlockSpec((1, d), fixed), pl.BlockSpec((1, d), fixed),
                  pl.BlockSpec(lp["w_ffn_in"].shape, fixed),
                  pl.BlockSpec((FFN_CONV, width), fixed), pl.BlockSpec((1, width), fixed),
                  pl.BlockSpec(lp["w_ffn_out"].shape, fixed)],
        out_specs=[pl.BlockSpec((tm, d), row), pl.BlockSpec((pad, width), fixed)],
        out_shape=[jax.ShapeDtypeStruct((n, d), F32), jax.ShapeDtypeStruct((pad, width), F32)],
        scratch_shapes=[pltpu.VMEM((pad + tm, width), F32), pltpu.VMEM((tm, d), F32)],
        compiler_params=_cparams(("arbitrary",)),
        name="conv_ffn",
    )(x, state, lp["g_pre_ffn"], lp["g_post_ffn"], lp["w_ffn_in"], lp["ffn_conv_w"], lp["ffn_conv_b"],
      lp["w_ffn_out"])


def _rope_tables(pos):
    freqs = jnp.power(ROPE_THETA, -jnp.arange(ROPE_HALF, dtype=F32) / ROPE_HALF)
    ang = pos.astype(F32)[:, None] * freqs
    reps = LANES // ROPE_HALF
    return jnp.tile(jnp.cos(ang), (1, reps)), jnp.tile(jnp.sin(ang), (1, reps))


def _prep_layer(li, P):
    d_model = P["w_in"].shape[1]
    w_in_t = jnp.transpose(P["w_in"], (2, 0, 1))[:, li, :]
    bounds = np.cumsum((0,) + IN_WIDTHS)
    seg = [w_in_t[bounds[i]:bounds[i + 1]] for i in range(len(IN_WIDTHS))]
    grp = DIFF_HEADS // DIFF_KV
    seg[3] = seg[3].reshape(DIFF_KV, grp, 2 * DIFF_DIM, d_model).swapaxes(0, 1).reshape(-1, d_model)
    seg = [jnp.pad(s, ((0, wd - s.shape[0]), (0, 0))) for s, wd in zip(seg, PROJ_WIDTHS)]
    lp = {"w_cat": jnp.concatenate(seg, axis=0).astype(BF16),
          "w_gate": w_in_t[bounds[-1]:].astype(BF16)}
    for name in ("g_pre_mix", "g_post_mix", "g_pre_ffn", "g_post_ffn", "ssd_conv_b", "ssd_norm", "gla_bg",
                 "mla_q_norm", "mla_kv_norm", "ffn_conv_b"):
        lp[name] = P[name][li][None]
    lp["ssd_conv_w"] = P["ssd_conv_w"][li]
    lp["ffn_conv_w"] = P["ffn_conv_w"][li]
    prm = jnp.zeros((SUBLANES, LANES), F32)
    lp["ssd_prm"] = prm.at[0, :SSD_HEADS].set(P["ssd_dt_bias"][li]).at[1, :SSD_HEADS].set(-jnp.exp(P["ssd_a_log"][li]))
    lp["ssd_dskip"] = jnp.repeat(P["ssd_d"][li], SSD_HEAD_DIM)[None]
    lam_init = 0.8 - 0.6 * math.exp(-0.3 * li)
    lam = (jnp.exp(jnp.sum(P["diff_lq1"][li] * P["diff_lk1"][li]))
           - jnp.exp(jnp.sum(P["diff_lq2"][li] * P["diff_lk2"][li])) + lam_init)
    lp["diff_prm"] = prm.at[0].set(jnp.tile(P["diff_norm"][li], DIFF_KV) * (1.0 - lam_init)).at[1].set(lam)
    lp["gla_wg2"] = jnp.zeros((LANES, LANES), F32).at[:GLA_RANK].set(P["gla_wg2"][li]).astype(BF16)
    lp["gla_norm"] = jnp.tile(P["gla_norm"][li], GLA_HEADS)[None]
    wuq = P["mla_wuq"][li].reshape(MLA_QR, MLA_HEADS, MLA_NOPE + MLA_ROPE)
    lp["mla_wuq"] = jnp.concatenate(
        [wuq[:, :, :MLA_NOPE].reshape(MLA_QR, -1),
         wuq[:, :, MLA_NOPE:MLA_NOPE + ROPE_HALF].reshape(MLA_QR, -1),
         wuq[:, :, MLA_NOPE + ROPE_HALF:].reshape(MLA_QR, -1)], axis=1).astype(BF16)
    n_nope = MLA_HEADS * MLA_NOPE
    wcat = jnp.zeros((n_nope + LANES, MLA_HEADS * MLA_QW), F32)
    wuv = jnp.zeros((MLA_HEADS, MLA_KVR, BR), F32)
    eye = jnp.eye(ROPE_HALF, dtype=F32)
    for h in range(MLA_HEADS):
        wcat = wcat.at[h * MLA_NOPE:(h + 1) * MLA_NOPE, h * MLA_QW:h * MLA_QW + MLA_KVR].set(P["mla_wuk"][li][:, h, :].T)
        for half in range(2):
            r0 = n_nope + half * (LANES // 2) + h * ROPE_HALF
            c0 = h * MLA_QW + MLA_KVR + half * ROPE_HALF
            wcat = wcat.at[r0:r0 + ROPE_HALF, c0:c0 + ROPE_HALF].set(eye)
        wuv = wuv.at[h, :, h * MLA_V:(h + 1) * MLA_V].set(P["mla_wuv"][li][:, h, :])
    lp["mla_wcat_t"] = wcat.T.astype(BF16)
    lp["mla_wuv"] = wuv.astype(BF16)
    lp["mla_wuv_t"] = wuv.transpose(0, 2, 1).astype(BF16)
    lp["w_mem_kv"] = jnp.concatenate([P["w_mem_k"][li], P["w_mem_v"][li]], axis=1).astype(BF16)
    wb = P["w_branch"][li]
    wb_diff = wb[1].reshape(DIFF_KV, grp, 2 * DIFF_DIM, d_model).swapaxes(0, 1).reshape(BR, d_model)
    lp["w_branch"] = wb.at[1].set(wb_diff).astype(BF16)
    lp["w_out"] = P["w_out"][li].astype(BF16)
    lp["w_ffn_in"] = P["w_ffn_in"][li].astype(BF16)
    lp["w_ffn_out"] = P["w_ffn_out"][li].astype(BF16)
    return lp


def _row_tile(n, want):
    return want if n % want == 0 else n


def _layer(x, lp, li, db, steps, st, past):
    n = x.shape[0]
    tm = _row_tile(n, 256)
    prompt = past is None
    (z, xbc, dt, dq, dk, dv, gq, gk, gv, gg, gr, cq, ckv, kr, mq) = _proj_in(
        x, lp["g_pre_mix"], lp["w_cat"], _row_tile(n, 512), transposed=(3,) if prompt else ())
    y_ssd, ssd_conv, ssd_h = _ssd_branch(z, xbc, dt, st["ssd_conv"], st["ssd_h"], lp, db, steps)
    y_gla, gla_s = _gla_branch(gq, gk, gv, gg, gr, st["gla_s"], lp, db, steps)
    past_len = 0 if prompt else past["page_table"].shape[1] * PAGE
    pos = jnp.repeat(past_len + jnp.arange(steps, dtype=jnp.int32), db)
    cos, sin = _rope_tables(pos)
    qcat, kcat, c_rows, kr_rows = _mla_prep(cq, ckv, kr, cos, sin, lp, tm, q_transposed=prompt)
    if prompt:
        tk = _row_tile(n, 512)
        tq_diff, tq_mla = _row_tile(n, 256), _row_tile(n, 512)
        gcol = jnp.broadcast_to(lp["diff_prm"][0][:, None], (LANES, tq_diff))
        lam_row = jnp.broadcast_to(lp["diff_prm"][1, 0], (1, tq_diff))
        y_diff = _diff_flash(dq, _diff_keys_alibi(dk, pos), _value_tiles_t(dv.astype(BF16), tk),
                             lam_row, gcol, tq_diff, tk)
        kb = kcat.astype(BF16)
        y_mla = _mla_flash(qcat, kb, _value_tiles_t(kb[:, :MLA_KVR], tk), lp["mla_wuv_t"], tq_mla, tk)
        y_mem = _mem_attn(mq.reshape(n // tm, tm, BR), st["mem_k"], st["mem_v"], key_minor=False).reshape(n, BR)
    else:
        seq = lambda a: _to_seq_major(a, db, steps, SUBLANES)
        pt = past["page_table"]
        g_pages = math.gcd(pt.shape[1], 64)
        y_diff = _to_time_major(_diff_decode(pt, seq(dq), seq(dk), seq(dv), lp["diff_prm"], past["diff_k"],
                                             past["diff_v"], li, steps, g_pages), steps)
        y_mla = _to_time_major(_mla_decode(pt, seq(qcat), seq(kcat), lp["mla_wuv"], past["mla_ckv"],
                                           past["mla_kr"], li, steps, g_pages), steps)
        y_mem = _to_time_major(_mem_attn(seq(mq), st["mem_k"], st["mem_v"], key_minor=True), steps)
    x = _merge_out(x, (y_ssd, y_diff, y_gla, y_mla, y_mem), lp, tm, transposed=(1, 3) if prompt else ())
    x, ffn_conv = _ffn(x, st["ffn_conv"], lp, shift=db, tm=_row_tile(n, 256) if db == 1 else n)
    return x, (dk, dv, c_rows, kr_rows, ssd_h, ssd_conv, gla_s, ffn_conv)


def kernel(x_prompt, x_sample, cache_diff_k, cache_diff_v, cache_mla_ckv, cache_mla_krope, cache_mem_k, cache_mem_v, state_ssd, state_ssd_conv, state_gla, state_ffn_conv, page_table, mem_prompt, w_in, ssd_conv_w, ssd_conv_b, ssd_dt_bias, ssd_a_log, ssd_d, ssd_norm, diff_lq1, diff_lk1, diff_lq2, diff_lk2, diff_norm, gla_wg2, gla_bg, gla_norm, mla_q_norm, mla_kv_norm, mla_wuq, mla_wuk, mla_wuv, w_mem_k, w_mem_v, w_branch, w_out, g_pre_mix, g_post_mix, g_pre_ffn, g_post_ffn, w_ffn_in, ffn_conv_w, ffn_conv_b, w_ffn_out):
    P = dict(w_in=w_in, ssd_conv_w=ssd_conv_w, ssd_conv_b=ssd_conv_b, ssd_dt_bias=ssd_dt_bias,
             ssd_a_log=ssd_a_log, ssd_d=ssd_d, ssd_norm=ssd_norm, diff_lq1=diff_lq1, diff_lk1=diff_lk1,
             diff_lq2=diff_lq2, diff_lk2=diff_lk2, diff_norm=diff_norm, gla_wg2=gla_wg2, gla_bg=gla_bg,
             gla_norm=gla_norm, mla_q_norm=mla_q_norm, mla_kv_norm=mla_kv_norm, mla_wuq=mla_wuq,
             mla_wuk=mla_wuk, mla_wuv=mla_wuv, w_mem_k=w_mem_k, w_mem_v=w_mem_v, w_branch=w_branch,
             w_out=w_out, g_pre_mix=g_pre_mix, g_post_mix=g_post_mix, g_pre_ffn=g_pre_ffn,
             g_post_ffn=g_post_ffn, w_ffn_in=w_ffn_in, ffn_conv_w=ffn_conv_w, ffn_conv_b=ffn_conv_b,
             w_ffn_out=w_ffn_out)
    depth = w_in.shape[0]
    pb, seq_len, d_model = x_prompt.shape
    assert pb == 1
    db, steps, _ = x_sample.shape
    mem_len = mem_prompt.shape[1]
    ffn_dim = ffn_conv_w.shape[2]
    past = {"page_table": page_table,
            "diff_k": jnp.transpose(cache_diff_k, (0, 1, 3, 4, 2)).reshape(depth, -1, LANES, PAGE),
            "diff_v": jnp.transpose(cache_diff_v, (0, 1, 3, 4, 2)).reshape(depth, -1, LANES, PAGE),
            "mla_ckv": cache_mla_ckv,
            "mla_kr": jnp.transpose(cache_mla_krope, (0, 1, 3, 2))}
    mem_kt = jnp.transpose(cache_mem_k, (0, 1, 3, 4, 2)).reshape(depth, db, BR, mem_len)
    mem_vt = jnp.transpose(cache_mem_v, (0, 1, 3, 4, 2)).reshape(depth, db, BR, mem_len)
    xp = x_prompt[0]
    xs = x_sample.transpose(1, 0, 2).reshape(steps * db, d_model)
    new_p, new_s, mem_kp, mem_vp = [], [], [], []
    for li in range(depth):
        lp = _prep_layer(li, P)
        mkv = _matmul(mem_prompt[0], lp["w_mem_kv"])
        mk, mv = mkv[:, :BR], mkv[:, BR:]
        mem_kp.append(mk.reshape(1, mem_len, MEM_HEADS, MEM_DIM))
        mem_vp.append(mv.reshape(1, mem_len, MEM_HEADS, MEM_DIM))
        st_p = {"ssd_conv": jnp.zeros((SUBLANES, SSD_CONV_DIM), F32),
                "ssd_h": jnp.zeros((1, SSD_HEADS, SSD_HEAD_DIM, SSD_STATE), F32),
                "gla_s": jnp.zeros((1, GLA_HEADS, GLA_DK, GLA_DV), F32),
                "ffn_conv": jnp.zeros((SUBLANES, ffn_dim), F32),
                "mem_k": mk[None], "mem_v": mv[None]}
        xp, o = _layer(xp, lp, li, 1, seq_len, st_p, None)
        dk, dv, c_rows, kr_rows, ssd_h, ssd_conv, gla_s, ffn_conv = o
        new_p.append((dk.reshape(1, seq_len, DIFF_KV, 2 * DIFF_DIM), dv.reshape(1, seq_len, DIFF_KV, 2 * DIFF_DIM),
                      c_rows[None], kr_rows[None], ssd_h, ssd_conv[None, SUBLANES - (SSD_CONV - 1):],
                      gla_s, ffn_conv[None, SUBLANES - (FFN_CONV - 1):]))
        st_s = {"ssd_conv": state_ssd_conv[li].transpose(1, 0, 2).reshape((SSD_CONV - 1) * db, SSD_CONV_DIM),
                "ssd_h": state_ssd[li],
                "gla_s": state_gla[li],
                "ffn_conv": state_ffn_conv[li].transpose(1, 0, 2).reshape((FFN_CONV - 1) * db, ffn_dim),
                "mem_k": mem_kt[li], "mem_v": mem_vt[li]}
        xs, o = _layer(xs, lp, li, db, steps, st_s, past)
        dk, dv, c_rows, kr_rows, ssd_h, ssd_conv, gla_s, ffn_conv = o
        bm = lambda a: a.reshape(-1, db, a.shape[-1]).transpose(1, 0, 2)
        new_s.append((bm(dk).reshape(db, steps, DIFF_KV, 2 * DIFF_DIM), bm(dv).reshape(db, steps, DIFF_KV, 2 * DIFF_DIM),
                      bm(c_rows), bm(kr_rows), ssd_h, bm(ssd_conv), gla_s, bm(ffn_conv)))
    stk = lambda outs, j: jnp.stack([o[j] for o in outs])
    yp = xp[None]
    ys = xs.reshape(steps, db, d_model).transpose(1, 0, 2)
    return (yp, ys,
            stk(new_p, 0), stk(new_p, 1), stk(new_p, 2), stk(new_p, 3),
            jnp.stack(mem_kp), jnp.stack(mem_vp),
            stk(new_p, 4), stk(new_p, 5), stk(new_p, 6), stk(new_p, 7),
            stk(new_s, 0), stk(new_s, 1), stk(new_s, 2), stk(new_s, 3),
            stk(new_s, 4), stk(new_s, 5), stk(new_s, 6), stk(new_s, 7))
```

```python
import functools
import math

import numpy as np
import jax
import jax.numpy as jnp
from jax import lax
from jax.experimental import pallas as pl
from jax.experimental.pallas import tpu as pltpu

F32 = jnp.float32
BF16 = jnp.bfloat16

BR = 256
N_BRANCH = 5
SSD_HEADS, SSD_HEAD_DIM, SSD_GROUPS, SSD_STATE, SSD_CONV = 4, 64, 2, 128, 4
SSD_CONV_DIM = BR + 2 * SSD_GROUPS * SSD_STATE
DIFF_HEADS, DIFF_KV, DIFF_DIM = 4, 2, 32
GLA_HEADS, GLA_DK, GLA_DV, GLA_RANK, GLA_TAU = 4, 32, 64, 16, 16.0
MLA_HEADS, MLA_QR, MLA_KVR, MLA_NOPE, MLA_ROPE, MLA_V = 4, 256, 128, 64, 32, 64
ROPE_THETA = 10000.0
MEM_HEADS, MEM_DIM = 4, 64
FFN_CONV = 3
PAGE = 128
EPS = 1e-6
NEG = -1e30

IN_WIDTHS = (BR, SSD_CONV_DIM, SSD_HEADS, DIFF_HEADS * 2 * DIFF_DIM, DIFF_KV * 2 * DIFF_DIM,
             DIFF_KV * 2 * DIFF_DIM, GLA_HEADS * GLA_DK, GLA_HEADS * GLA_DK, GLA_HEADS * GLA_DV,
             GLA_RANK, GLA_HEADS * GLA_DV, MLA_QR, MLA_KVR, MLA_ROPE, MEM_HEADS * MEM_DIM)
PROJ_WIDTHS = (256, 768, 128, 256, 128, 128, 128, 128, 256, 128, 256, 256, 128, 128, 256)

LANES = 128
SUBLANES = 8
VMEM_LIMIT = 56 * 1024 * 1024


def _cparams(sem):
    return pltpu.CompilerParams(dimension_semantics=sem, vmem_limit_bytes=VMEM_LIMIT)


def _dot(a, b):
    return jnp.dot(a, b, preferred_element_type=F32)


def _dot_nt(a, b):
    return lax.dot_general(a, b, (((1,), (1,)), ((), ())), preferred_element_type=F32)


def _dot_tn(a, b):
    return lax.dot_general(a, b, (((0,), (0,)), ((), ())), preferred_element_type=F32)


def _split3(x):
    hi = x.astype(BF16)
    r = x - hi.astype(F32)
    mid = r.astype(BF16)
    lo = (r - mid.astype(F32)).astype(BF16)
    return hi, mid, lo


def _xdot_l(m01, x):
    hi, mid, lo = _split3(x)
    return _dot(m01, hi) + _dot(m01, mid) + _dot(m01, lo)


def _xdot_r(x, m01):
    hi, mid, lo = _split3(x)
    return _dot(hi, m01) + _dot(mid, m01) + _dot(lo, m01)


def _xdot_nt(m01, x):
    hi, mid, lo = _split3(x)
    return _dot_nt(m01, hi) + _dot_nt(m01, mid) + _dot_nt(m01, lo)


def _rms(x, g):
    return x * lax.rsqrt(jnp.mean(x * x, axis=-1, keepdims=True) + EPS) * g


def _silu(x):
    return x * (1.0 / (1.0 + jnp.exp(-x)))


def _sigmoid(x):
    return 1.0 / (1.0 + jnp.exp(-x))


def _softplus(x):
    return jnp.maximum(x, 0.0) + jnp.log(1.0 + jnp.exp(-jnp.abs(x)))


def _log_sigmoid(x):
    return -_softplus(-x)


def _gelu_tanh(x):
    c = math.sqrt(2.0 / math.pi)
    return 0.5 * x * (1.0 + jnp.tanh(c * (x + 0.044715 * (x * x * x))))


def _lane_iota(shape):
    return lax.broadcasted_iota(jnp.int32, shape, len(shape) - 1)


def _row_iota(shape):
    return lax.broadcasted_iota(jnp.int32, shape, len(shape) - 2)


def _proj_in_kernel(x_ref, g_ref, w_ref, *o_refs, transposed):
    xn = _rms(x_ref[...], g_ref[...]).astype(BF16)
    off = 0
    for i, (o_ref, wd) in enumerate(zip(o_refs, PROJ_WIDTHS)):
        w = w_ref[off:off + wd, :]
        o_ref[...] = _dot_nt(w, xn) if i in transposed else _dot_nt(xn, w)
        off += wd


def _proj_in(x, g, w, tm, transposed=()):
    n, d = x.shape
    spec = lambda i, wd: (pl.BlockSpec((wd, tm), lambda r: (0, r)) if i in transposed
                          else pl.BlockSpec((tm, wd), lambda r: (r, 0)))
    shape = lambda i, wd: jax.ShapeDtypeStruct((wd, n) if i in transposed else (n, wd), F32)
    return pl.pallas_call(
        functools.partial(_proj_in_kernel, transposed=transposed),
        grid=(n // tm,),
        in_specs=[pl.BlockSpec((tm, d), lambda i: (i, 0)),
                  pl.BlockSpec((1, d), lambda i: (0, 0)),
                  _resident(w.shape)],
        out_specs=[spec(i, wd) for i, wd in enumerate(PROJ_WIDTHS)],
        out_shape=[shape(i, wd) for i, wd in enumerate(PROJ_WIDTHS)],
        compiler_params=_cparams(("arbitrary",)),
        name="proj_in",
    )(x, g, w)


def _conv_taps(ext_ref, cur, w_ref, b_ref, taps, shift, pad, tm):
    acc = b_ref[...] + w_ref[taps - 1:taps, :] * cur
    for j in range(taps - 1):
        off = pad - (taps - 1 - j) * shift
        acc = acc + w_ref[j:j + 1, :] * ext_ref[off:off + tm, :]
    return acc


def _conv_silu_kernel(x_ref, st_ref, w_ref, b_ref, y_ref, ns_ref, ext_ref, *, taps, shift, pad):
    tm = x_ref.shape[0]

    @pl.when(pl.program_id(0) == 0)
    def _():
        ext_ref[0:pad, :] = st_ref[...]

    cur = x_ref[...]
    ext_ref[pad:pad + tm, :] = cur
    y_ref[...] = _silu(_conv_taps(ext_ref, cur, w_ref, b_ref, taps, shift, pad, tm))
    tail = ext_ref[tm:tm + pad, :]
    ns_ref[...] = tail
    ext_ref[0:pad, :] = tail


def _conv_silu(x, state, w, b, shift, tm):
    n, c = x.shape
    taps = w.shape[0]
    pad = state.shape[0]
    assert tm >= pad and pad >= (taps - 1) * shift
    return pl.pallas_call(
        functools.partial(_conv_silu_kernel, taps=taps, shift=shift, pad=pad),
        grid=(n // tm,),
        in_specs=[pl.BlockSpec((tm, c), lambda i: (i, 0)),
                  pl.BlockSpec((pad, c), lambda i: (0, 0)),
                  pl.BlockSpec((taps, c), lambda i: (0, 0)),
                  pl.BlockSpec((1, c), lambda i: (0, 0))],
        out_specs=[pl.BlockSpec((tm, c), lambda i: (i, 0)),
                   pl.BlockSpec((pad, c), lambda i: (0, 0))],
        out_shape=[jax.ShapeDtypeStruct((n, c), F32), jax.ShapeDtypeStruct((pad, c), F32)],
        scratch_shapes=[pltpu.VMEM((pad + tm, c), F32)],
        compiler_params=_cparams(("arbitrary",)),
        name="conv_silu",
    )(x, state, w, b)


def _ssd_kernel(z_ref, xbc_ref, dt_ref, h0_ref, prm_ref, dskip_ref, ng_ref, tri_ref, exp_ref, eye_ref,
                y_ref, hout_ref, st_ref, *, chunk, t_valid, n_chunks):
    L = chunk
    c = pl.program_id(1)

    @pl.when(c == 0)
    def _():
        st_ref[...] = h0_ref[0].reshape(SSD_HEADS * SSD_HEAD_DIM, SSD_STATE).T

    xbc = xbc_ref[0]
    xs = xbc[:, 0:BR]
    bm = xbc[:, BR:2 * BR].astype(BF16)
    cm = xbc[:, 2 * BR:3 * BR].astype(BF16)
    dt = _softplus(dt_ref[0] + prm_ref[0:1, :])
    if t_valid < L:
        dt = jnp.where(_row_iota((L, LANES)) < t_valid, dt, 0.0)
    d_a = dt * prm_ref[1:2, :]
    tri = tri_ref[...]
    expand = exp_ref[...]
    cs = _xdot_l(tri, d_a)
    cs_exp = _xdot_r(cs, expand)
    dt_exp = _xdot_r(dt, expand)
    cs_t = _xdot_nt(eye_ref[...], cs)
    xdt = xs * dt_exp
    xdt_b = xdt.astype(BF16)
    causal = _row_iota((L, L)) >= _lane_iota((L, L))
    lane_head = _lane_iota((L, BR)) // SSD_HEAD_DIM
    st = st_ref[...]
    st_b = st.astype(BF16)
    y = xs * dskip_ref[...]
    for g in range(SSD_GROUPS):
        cg = cm[:, g * SSD_STATE:(g + 1) * SSD_STATE]
        bg = bm[:, g * SSD_STATE:(g + 1) * SSD_STATE]
        cb = _dot_nt(cg, bg)
        for hh in range(SSD_HEADS // SSD_GROUPS):
            h = g * (SSD_HEADS // SSD_GROUPS) + hh
            dec = jnp.exp(jnp.where(causal, cs[:, h:h + 1] - cs_t[h:h + 1, :], NEG))
            yd = _dot((cb * dec).astype(BF16), xdt_b)
            y = y + jnp.where(lane_head == h, yd, 0.0)
    half = _lane_iota((L, BR)) < (BR // 2)
    y_off = jnp.where(half, _dot(cm[:, 0:SSD_STATE], st_b), _dot(cm[:, SSD_STATE:2 * SSD_STATE], st_b))
    y = y + y_off * jnp.exp(cs_exp)
    cs_last = cs_exp[L - 1:L, :]
    xw = (xdt * jnp.exp(cs_last - cs_exp)).astype(BF16)
    half_s = _lane_iota((SSD_STATE, BR)) < (BR // 2)
    upd = jnp.where(half_s, _dot_tn(bm[:, 0:SSD_STATE], xw), _dot_tn(bm[:, SSD_STATE:2 * SSD_STATE], xw))
    st_new = st * jnp.exp(cs_last) + upd
    st_ref[...] = st_new
    y_ref[0] = _rms(y * _silu(z_ref[0]), ng_ref[...])

    @pl.when(c == n_chunks - 1)
    def _():
        hout_ref[0] = st_new.T.reshape(SSD_HEADS, SSD_HEAD_DIM, SSD_STATE)


def _ssd_scan(z, xbc, dt_raw, h0, prm, dskip, ng, chunk, t_valid):
    b, t, _ = z.shape
    n_chunks = t // chunk
    tri = jnp.asarray(np.tril(np.ones((chunk, chunk), np.float32)), BF16)
    expand = np.zeros((LANES, BR), np.float32)
    for h in range(SSD_HEADS):
        expand[h, h * SSD_HEAD_DIM:(h + 1) * SSD_HEAD_DIM] = 1.0
    eye = np.eye(SUBLANES, LANES, dtype=np.float32)
    row = lambda bi, ci: (bi, ci, 0)
    fixed2 = lambda bi, ci: (0, 0)
    return pl.pallas_call(
        functools.partial(_ssd_kernel, chunk=chunk, t_valid=t_valid, n_chunks=n_chunks),
        grid=(b, n_chunks),
        in_specs=[pl.BlockSpec((1, chunk, BR), row),
                  pl.BlockSpec((1, chunk, SSD_CONV_DIM), row),
                  pl.BlockSpec((1, chunk, LANES), row),
                  pl.BlockSpec((1, SSD_HEADS, SSD_HEAD_DIM, SSD_STATE), lambda bi, ci: (bi, 0, 0, 0)),
                  pl.BlockSpec((SUBLANES, LANES), fixed2),
                  pl.BlockSpec((1, BR), fixed2),
                  pl.BlockSpec((1, BR), fixed2),
                  pl.BlockSpec((chunk, chunk), fixed2),
                  pl.BlockSpec((LANES, BR), fixed2),
                  pl.BlockSpec((SUBLANES, LANES), fixed2)],
        out_specs=[pl.BlockSpec((1, chunk, BR), row),
                   pl.BlockSpec((1, SSD_HEADS, SSD_HEAD_DIM, SSD_STATE), lambda bi, ci: (bi, 0, 0, 0))],
        out_shape=[jax.ShapeDtypeStruct((b, t, BR), F32),
                   jax.ShapeDtypeStruct((b, SSD_HEADS, SSD_HEAD_DIM, SSD_STATE), F32)],
        scratch_shapes=[pltpu.VMEM((SSD_STATE, BR), F32)],
        compiler_params=_cparams(("arbitrary", "arbitrary")),
        name="ssd_scan",
    )(z, xbc, dt_raw, h0, prm, dskip, ng, tri, jnp.asarray(expand, BF16), jnp.asarray(eye, BF16))


def _pad_lanes(v, width):
    return jnp.pad(v, [(0, 0)] * (v.ndim - 1) + [(0, width - v.shape[-1])])


def _to_seq_major(a, db, steps, pad_to):
    c = a.shape[-1]
    a = a.reshape(steps, db, c).transpose(1, 0, 2)
    return jnp.pad(a, ((0, 0), (0, pad_to - steps), (0, 0)))


def _to_time_major(a, steps):
    db, _, c = a.shape
    return a[:, :steps].transpose(1, 0, 2).reshape(steps * db, c)


def _ssd_branch(z, xbc_raw, dt_raw, conv_state, h0, lp, db, steps):
    n = z.shape[0]
    tm = min(n, 512)
    xbc, conv_new = _conv_silu(xbc_raw, conv_state, lp["ssd_conv_w"], lp["ssd_conv_b"], shift=db, tm=tm)
    if db == 1:
        chunk = min(steps, 128)
        y, h_new = _ssd_scan(z[None], xbc[None], dt_raw[None], h0, lp["ssd_prm"], lp["ssd_dskip"],
                             lp["ssd_norm"], chunk, chunk)
        return y[0], conv_new, h_new
    zs, xs, ds = (_to_seq_major(a, db, steps, SUBLANES) for a in (z, xbc, dt_raw))
    y, h_new = _ssd_scan(zs, xs, ds, h0, lp["ssd_prm"], lp["ssd_dskip"], lp["ssd_norm"], SUBLANES, steps)
    return _to_time_major(y, steps), conv_new, h_new


def _gla_consts(chunk):
    L = chunk
    nl = int(math.log2(L))
    assert 2 ** nl == L
    t = np.arange(L)
    mats, masks = [], []
    for lvl in range(nl):
        b = L >> (lvl + 1)
        same = (t[:, None] // b) == (t[None, :] // b)
        mats.append((same & (t[None, :] <= t[:, None])).astype(np.float32))
    for lvl in range(nl):
        b = L >> (lvl + 1)
        same = (t[:, None] // b) == (t[None, :] // b)
        mats.append((same & (t[None, :] > t[:, None])).astype(np.float32))
    mats.append((t[None, :] <= t[:, None]).astype(np.float32))
    mats.append((t[None, :] > t[:, None]).astype(np.float32))
    for lvl in range(nl):
        b = L >> (lvl + 1)
        tb, sb = t[:, None] // b, t[None, :] // b
        masks.append(((tb % 2 == 1) & (sb == tb - 1)).astype(np.float32))
    masks.append((t[:, None] == t[None, :]).astype(np.float32))
    masks = np.stack([np.tile(m, (GLA_HEADS, 1)) for m in masks])
    ones = np.zeros((BR, BR), np.float32)
    for h in range(GLA_HEADS):
        ones[h * GLA_DV:(h + 1) * GLA_DV, h * GLA_DV:(h + 1) * GLA_DV] = 1.0
    return (jnp.asarray(np.concatenate(mats, 0), BF16), jnp.asarray(masks, F32), jnp.asarray(ones, BF16), nl)


def _gla_kernel(q_ref, k_ref, v_ref, gg_ref, r_ref, s0_ref, wg2_ref, bg_ref, ng_ref, mall_ref, masks_ref,
                ones_ref, y_ref, sout_ref, st_ref, *, chunk, t_valid, n_chunks, nl):
    L = chunk
    c = pl.program_id(1)

    @pl.when(c == 0)
    def _():
        st_ref[...] = s0_ref[0]

    q = q_ref[0] * (GLA_DK ** -0.5)
    k = k_ref[0]
    g = _log_sigmoid(_dot(gg_ref[0].astype(BF16), wg2_ref[...]) + bg_ref[...]) * (1.0 / GLA_TAU)
    if t_valid < L:
        live = _row_iota((L, LANES)) < t_valid
        g = jnp.where(live, g, 0.0)
        k = jnp.where(live, k, 0.0)
    v_b = v_ref[0].astype(BF16)
    gsum = _xdot_l(mall_ref[...], g)
    lane_head = _lane_iota((L, LANES)) // GLA_DK

    def heads_on_rows(x):
        return jnp.concatenate([jnp.where(lane_head == h, x, 0.0) for h in range(GLA_HEADS)], axis=0).astype(BF16)

    att = masks_ref[nl] * _dot_nt(heads_on_rows(q), k.astype(BF16))
    for lvl in range(nl):
        qt = q * jnp.exp(gsum[lvl * L:(lvl + 1) * L])
        kt = k * jnp.exp(gsum[(nl + lvl) * L:(nl + lvl + 1) * L])
        att = att + masks_ref[lvl] * _dot_nt(heads_on_rows(qt), kt.astype(BF16))
    bc = gsum[2 * nl * L:(2 * nl + 1) * L]
    tail = gsum[(2 * nl + 1) * L:(2 * nl + 2) * L]
    st = st_ref[...]
    res = _dot(att.astype(BF16), v_b) + _dot_nt(heads_on_rows(q * jnp.exp(bc)), st.astype(BF16))
    out_head = _lane_iota((L, BR)) // GLA_DV
    o = jnp.zeros((L, BR), F32)
    for h in range(GLA_HEADS):
        o = o + jnp.where(out_head == h, res[h * L:(h + 1) * L], 0.0)
    st_new = st * jnp.exp(bc[L - 1:L, :]) + _dot_tn(v_b, (k * jnp.exp(tail)).astype(BF16))
    st_ref[...] = st_new
    msq = _xdot_r(o * o, ones_ref[...]) * (1.0 / GLA_DV)
    y_ref[0] = o * lax.rsqrt(msq + EPS) * ng_ref[...] * _silu(r_ref[0])

    @pl.when(c == n_chunks - 1)
    def _():
        sout_ref[0] = st_new


def _gla_scan(q, k, v, gg, r, s0, wg2, bg, ng, chunk, t_valid):
    b, t, _ = q.shape
    n_chunks = t // chunk
    mall, masks, ones, nl = _gla_consts(chunk)
    row = lambda bi, ci: (bi, ci, 0)
    fixed2 = lambda bi, ci: (0, 0)
    return pl.pallas_call(
        functools.partial(_gla_kernel, chunk=chunk, t_valid=t_valid, n_chunks=n_chunks, nl=nl),
        grid=(b, n_chunks),
        in_specs=[pl.BlockSpec((1, chunk, LANES), row),
                  pl.BlockSpec((1, chunk, LANES), row),
                  pl.BlockSpec((1, chunk, BR), row),
                  pl.BlockSpec((1, chunk, LANES), row),
                  pl.BlockSpec((1, chunk, BR), row),
                  pl.BlockSpec((1, BR, LANES), lambda bi, ci: (bi, 0, 0)),
                  pl.BlockSpec((LANES, LANES), fixed2),
                  pl.BlockSpec((1, LANES), fixed2),
                  pl.BlockSpec((1, BR), fixed2),
                  pl.BlockSpec(mall.shape, fixed2),
                  pl.BlockSpec(masks.shape, lambda bi, ci: (0, 0, 0)),
                  pl.BlockSpec((BR, BR), fixed2)],
        out_specs=[pl.BlockSpec((1, chunk, BR), row),
                   pl.BlockSpec((1, BR, LANES), lambda bi, ci: (bi, 0, 0))],
        out_shape=[jax.ShapeDtypeStruct((b, t, BR), F32), jax.ShapeDtypeStruct((b, BR, LANES), F32)],
        scratch_shapes=[pltpu.VMEM((BR, LANES), F32)],
        compiler_params=_cparams(("arbitrary", "arbitrary")),
        name="gla_scan",
    )(q, k, v, gg, r, s0, wg2, bg, ng, mall, masks, ones)


def _gla_state_in(s):
    blocks = [jnp.pad(s[:, h].transpose(0, 2, 1), ((0, 0), (0, 0), (h * GLA_DK, LANES - (h + 1) * GLA_DK)))
              for h in range(GLA_HEADS)]
    return jnp.concatenate(blocks, axis=1)


def _gla_state_out(st):
    return jnp.stack([st[:, h * GLA_DV:(h + 1) * GLA_DV, h * GLA_DK:(h + 1) * GLA_DK].transpose(0, 2, 1)
                      for h in range(GLA_HEADS)], axis=1)


def _gla_branch(q, k, v, gg, r, s0, lp, db, steps):
    st0 = _gla_state_in(s0)
    args = (lp["gla_wg2"], lp["gla_bg"], lp["gla_norm"])
    if db == 1:
        chunk = min(steps, 128)
        y, st = _gla_scan(q[None], k[None], v[None], gg[None], r[None], st0, *args, chunk, chunk)
        return y[0], _gla_state_out(st)
    qs, ks, vs, gs, rs = (_to_seq_major(a, db, steps, SUBLANES) for a in (q, k, v, gg, r))
    y, st = _gla_scan(qs, ks, vs, gs, rs, st0, *args, SUBLANES, steps)
    return _to_time_major(y, steps), _gla_state_out(st)


DIFF_MAPS = DIFF_HEADS * 2
POS_SPLIT = 128


def _diff_stack_q(q, rows):
    lane = _lane_iota((rows, LANES))
    blocks = []
    for kv in range(DIFF_KV):
        for g in range(DIFF_HEADS // DIFF_KV):
            qg = q[:, g * LANES:(g + 1) * LANES]
            for i in range(2):
                lo = kv * 2 * DIFF_DIM + i * DIFF_DIM
                blocks.append(jnp.where((lane >= lo) & (lane < lo + DIFF_DIM), qg, 0.0))
    return jnp.concatenate(blocks, axis=0).astype(BF16)


LOG2E = math.log2(math.e)
DIFF_QSCALE = DIFF_DIM ** -0.5 * LOG2E


def _diff_slope(blk):
    head = blk // 2
    return 2.0 ** (-8.0 * (head + 1) / DIFF_HEADS) * LOG2E


def _bf16_parts(c):
    parts, r = [], np.float32(c)
    for _ in range(3):
        p = np.float32(np.asarray(r).astype(BF16))
        parts.append(float(p))
        r = np.float32(r - p)
    return parts


def _online_update(s, blk, rows, m_ref, l_ref, p_ref):
    r0 = blk * rows
    m_prev = m_ref[r0:r0 + rows, :]
    m_new = jnp.maximum(m_prev, jnp.max(s, axis=1, keepdims=True))
    alpha = jnp.exp2(m_prev - m_new)
    p = jnp.exp2(s - m_new[:, 0:1])
    l_ref[r0:r0 + rows, :] = alpha * l_ref[r0:r0 + rows, :] + jnp.sum(p, axis=1, keepdims=True)
    m_ref[r0:r0 + rows, :] = m_new
    p_ref[r0:r0 + rows, 0:s.shape[1]] = p.astype(p_ref.dtype)
    return alpha


def _diff_finish(acc_ref, l_ref, prm_ref, rows):
    lane = _lane_iota((rows, LANES))
    low = lane < 2 * DIFF_DIM
    lam = prm_ref[1:2, :]
    outs = []
    for g in range(DIFF_HEADS // DIFF_KV):
        parts = []
        for kv in range(DIFF_KV):
            b1 = ((kv * 2 + g) * 2) * rows
            b2 = b1 + rows
            o1 = acc_ref[b1:b1 + rows, :] / l_ref[b1:b1 + rows, :]
            o2 = acc_ref[b2:b2 + rows, :] / l_ref[b2:b2 + rows, :]
            parts.append(o1 - lam * o2)
        og = jnp.where(low, parts[0], parts[1])
        sq = og * og
        ss = jnp.where(low, jnp.sum(jnp.where(low, sq, 0.0), axis=1, keepdims=True),
                       jnp.sum(jnp.where(low, 0.0, sq), axis=1, keepdims=True))
        outs.append(og * lax.rsqrt(ss * (1.0 / (2 * DIFF_DIM)) + EPS) * prm_ref[0:1, :])
    return outs


def _diff_stack_q_alibi(q, rows):
    row = _row_iota((LANES, rows))
    blocks = []
    for kv in range(DIFF_KV):
        for g in range(DIFF_HEADS // DIFF_KV):
            qg = q[g * LANES:(g + 1) * LANES, :]
            for i in range(2):
                lo = kv * 2 * DIFF_DIM + i * DIFF_DIM
                parts = _bf16_parts(_diff_slope((kv * 2 + g) * 2 + i))
                qm = jnp.where((row >= lo) & (row < lo + DIFF_DIM), qg, 0.0)
                aug = jnp.zeros((LANES, rows), F32)
                for j, part in enumerate(parts):
                    aug = jnp.where(row == j, part * POS_SPLIT, jnp.where(row == len(parts) + j, part, aug))
                blocks.append(jnp.concatenate([qm, aug], axis=0))
    return jnp.concatenate(blocks, axis=1).astype(BF16)


FLASH_COLS = 2 * LANES
FLASH_AHEAD = 4


def _flash_tiles(k_ref, vt_ref, qs_ref, m_ref, acc_ref, q0, tq, tk):
    width = qs_ref.shape[1]

    def kv_tile(ki, masked):
        k = k_ref[pl.ds(pl.multiple_of(ki * tk, tk), tk), :]
        vt = vt_ref[ki]
        m_all = m_ref[...]
        ms, accs = [], []
        starts = list(range(0, width, FLASH_COLS))
        pending = [_dot(k, qs_ref[:, c:c + FLASH_COLS]) for c in starts[:FLASH_AHEAD]]
        for idx, c0 in enumerate(starts):
            cols = slice(c0, c0 + FLASH_COLS)
            s = pending.pop(0)
            if idx + FLASH_AHEAD < len(starts):
                c_next = starts[idx + FLASH_AHEAD]
                pending.append(_dot(k, qs_ref[:, c_next:c_next + FLASH_COLS]))
            if masked:
                kpos = ki * tk + _row_iota((tk, FLASH_COLS))
                qpos = q0 + (c0 + _lane_iota((tk, FLASH_COLS))) % tq
                s = jnp.where(kpos <= qpos, s, NEG)
            m_prev = m_all[:, cols]
            m_new = jnp.maximum(m_prev, jnp.max(s, axis=0, keepdims=True))
            p = jnp.exp2(s - m_new).astype(BF16)
            ms.append(m_new)
            accs.append(jnp.exp2(m_prev - m_new) * acc_ref[:, cols] + _dot(vt, p))
        m_ref[...] = jnp.concatenate(ms, axis=1)
        acc_ref[...] = jnp.concatenate(accs, axis=1)

    m_ref[...] = jnp.full(m_ref.shape, NEG, F32)
    acc_ref[...] = jnp.zeros(acc_ref.shape, F32)
    n_full = q0 // tk

    def body(ki, carry):
        kv_tile(ki, False)
        return carry

    lax.fori_loop(0, n_full, body, 0)
    kv_tile(n_full, True)


def _diff_flash_kernel(q_ref, k_ref, vt_ref, lam_ref, gcol_ref, o_ref, qs_ref, m_ref, acc_ref, *, tq, tk):
    q0 = pl.program_id(0) * tq
    qs_ref[...] = _diff_stack_q_alibi(q_ref[...] * DIFF_QSCALE, tq)
    _flash_tiles(k_ref, vt_ref, qs_ref, m_ref, acc_ref, q0, tq, tk)
    row = _row_iota((LANES, tq))
    low = row < 2 * DIFF_DIM
    lam = lam_ref[...]
    for g in range(DIFF_HEADS // DIFF_KV):
        parts = []
        for kv in range(DIFF_KV):
            c1 = ((kv * 2 + g) * 2) * tq
            c2 = c1 + tq
            o1 = acc_ref[0:LANES, c1:c1 + tq] / acc_ref[LANES:LANES + 1, c1:c1 + tq]
            o2 = acc_ref[0:LANES, c2:c2 + tq] / acc_ref[LANES:LANES + 1, c2:c2 + tq]
            parts.append(o1 - lam * o2)
        og = jnp.where(low, parts[0], parts[1])
        sq = og * og
        ss = jnp.where(low, jnp.sum(jnp.where(low, sq, 0.0), axis=0, keepdims=True),
                       jnp.sum(jnp.where(low, 0.0, sq), axis=0, keepdims=True))
        o_ref[g * LANES:(g + 1) * LANES, :] = og * lax.rsqrt(ss * (1.0 / (2 * DIFF_DIM)) + EPS) * gcol_ref[...]


def _resident(shape):
    return pl.BlockSpec(shape, lambda *_: (0,) * len(shape), pipeline_mode=pl.Buffered(1))


def _value_tiles_t(v, tk):
    n, c = v.shape
    tiles = v.reshape(n // tk, tk, c).transpose(0, 2, 1)
    extra = jnp.zeros((n // tk, SUBLANES, tk), v.dtype).at[:, 0, :].set(1.0)
    return jnp.concatenate([tiles, extra], axis=1)


def _diff_flash(q_t, k_aug, v_t, lam_row, gcol, tq, tk):
    n = q_t.shape[1]
    assert tk % tq == 0 and n % tk == 0 and tq % FLASH_COLS == 0
    width = DIFF_MAPS * tq
    return pl.pallas_call(
        functools.partial(_diff_flash_kernel, tq=tq, tk=tk),
        grid=(n // tq,),
        in_specs=[pl.BlockSpec((BR, tq), lambda qi: (0, qi)),
                  _resident(k_aug.shape),
                  _resident(v_t.shape),
                  _resident((1, tq)),
                  _resident((LANES, tq))],
        out_specs=pl.BlockSpec((BR, tq), lambda qi: (0, qi)),
        out_shape=jax.ShapeDtypeStruct((BR, n), F32),
        scratch_shapes=[pltpu.VMEM((2 * LANES, width), BF16),
                        pltpu.VMEM((1, width), F32),
                        pltpu.VMEM((LANES + SUBLANES, width), F32)],
        compiler_params=_cparams(("arbitrary",)),
        name="diff_flash",
    )(q_t, k_aug, v_t, lam_row, gcol)


def _diff_keys_alibi(k, pos):
    assert k.shape[0] <= POS_SPLIT * 256
    hi, lo = pos // POS_SPLIT, pos % POS_SPLIT
    cols = jnp.stack([hi, hi, hi, lo, lo, lo], axis=1).astype(BF16)
    return jnp.concatenate([k.astype(BF16), _pad_lanes(cols, LANES)], axis=1)


def _page_copies(pt_ref, caches, bufs, sems, layer, b, c, slot, g, pages_per_step):
    page = pt_ref[b, c * pages_per_step + g]
    off = pl.multiple_of(g * PAGE, PAGE)
    copies = []
    for a, (cache, buf) in enumerate(zip(caches, bufs)):
        key_minor = buf.shape[2] == pages_per_step * PAGE
        dst = buf.at[slot, :, pl.ds(off, PAGE)] if key_minor else buf.at[slot, pl.ds(off, PAGE), :]
        copies.append(pltpu.make_async_copy(cache.at[layer, page], dst, sems.at[slot, a]))
    return copies


def _stream_pages(pt_ref, caches, bufs, sems, layer, pages_per_step, n_chunks):
    G = pages_per_step
    b, c = pl.program_id(0), pl.program_id(1)
    step = b * n_chunks + c
    last = step + 1 == pl.num_programs(0) * n_chunks
    slot = step % 2
    wrap = c + 1 == n_chunks
    nb = jnp.where(last, b, jnp.where(wrap, b + 1, b))
    nc = jnp.where(last, c, jnp.where(wrap, 0, c + 1))

    def for_pages(fn):
        def body(g, carry):
            fn(g)
            return carry
        lax.fori_loop(0, G, body, 0)

    def start(bb, cc, sl, g):
        for cp in _page_copies(pt_ref, caches, bufs, sems, layer, bb, cc, sl, g, G):
            cp.start()

    def wait(bb, cc, sl, g):
        for cp in _page_copies(pt_ref, caches, bufs, sems, layer, bb, cc, sl, g, G):
            cp.wait()

    @pl.when(step == 0)
    def _():
        for_pages(lambda g: start(b, c, slot, g))

    for g in range(G):
        wait(b, c, slot, g)

    def start_next():
        for g in range(G):
            start(nb, nc, 1 - slot, g)

    def drain():
        @pl.when(last)
        def _():
            for_pages(lambda g: wait(nb, nc, 1 - slot, g))

    return slot, start_next, drain


def _diff_decode_kernel(pt_ref, q_ref, kn_ref, vn_ref, prm_ref, kc_ref, vc_ref, o_ref,
                        kbuf, vbuf, sems, qs_ref, m_ref, l_ref, acc_ref, p_ref,
                        *, layer, pages_per_step, n_chunks, steps, past_len):
    G = pages_per_step
    R8 = SUBLANES
    c = pl.program_id(1)
    slot, start_next, drain = _stream_pages(pt_ref, (kc_ref, vc_ref), (kbuf, vbuf), sems, layer, G, n_chunks)

    @pl.when(c == 0)
    def _():
        qs_ref[...] = _diff_stack_q(q_ref[0] * DIFF_QSCALE, R8)
        m_ref[...] = jnp.full(m_ref.shape, NEG, F32)
        l_ref[...] = jnp.zeros(l_ref.shape, F32)
        acc_ref[...] = jnp.zeros(acc_ref.shape, F32)

    qs = qs_ref[...]
    width = G * PAGE
    s = _dot(qs, kbuf[slot].astype(BF16))
    start_next()
    ndist = ((c * width + _lane_iota((R8, width))) - (past_len + _row_iota((R8, width)))).astype(F32)
    for blk in range(DIFF_MAPS):
        sb = s[blk * R8:(blk + 1) * R8] + _diff_slope(blk) * ndist
        alpha = _online_update(sb, blk, R8, m_ref, l_ref, p_ref)
        acc_ref[blk * R8:(blk + 1) * R8, :] = alpha * acc_ref[blk * R8:(blk + 1) * R8, :]
    acc_ref[...] = acc_ref[...] + _dot_nt(p_ref[...].astype(BF16), vbuf[slot].astype(BF16))
    drain()

    @pl.when(c == n_chunks - 1)
    def _():
        pad = jnp.zeros((PAGE - R8, LANES), F32)
        kn = jnp.concatenate([kn_ref[0], pad], axis=0).astype(BF16)
        vn = jnp.concatenate([vn_ref[0], pad], axis=0).astype(BF16)
        sn = _dot_nt(qs, kn)
        kt = _lane_iota((R8, PAGE))
        qt = _row_iota((R8, PAGE))
        vis = (kt <= qt) & (kt < steps)
        nd = (kt - qt).astype(F32)
        for blk in range(DIFF_MAPS):
            sb = jnp.where(vis, sn[blk * R8:(blk + 1) * R8] + _diff_slope(blk) * nd, NEG)
            alpha = _online_update(sb, blk, R8, m_ref, l_ref, p_ref)
            acc_ref[blk * R8:(blk + 1) * R8, :] = alpha * acc_ref[blk * R8:(blk + 1) * R8, :]
        acc_ref[...] = acc_ref[...] + _dot(p_ref[:, 0:PAGE].astype(BF16), vn)
        outs = _diff_finish(acc_ref, l_ref, prm_ref, R8)
        o_ref[0, :, 0:LANES] = outs[0]
        o_ref[0, :, LANES:2 * LANES] = outs[1]


def _diff_decode(page_table, q, k_new, v_new, prm, kc, vc, layer, steps, pages_per_step):
    db, n_pages = page_table.shape
    G = pages_per_step
    n_chunks = n_pages // G
    rows = DIFF_MAPS * SUBLANES
    seq3 = lambda b, c, pt: (b, 0, 0)
    grid_spec = pltpu.PrefetchScalarGridSpec(
        num_scalar_prefetch=1,
        grid=(db, n_chunks),
        in_specs=[pl.BlockSpec((1, SUBLANES, BR), seq3),
                  pl.BlockSpec((1, SUBLANES, LANES), seq3),
                  pl.BlockSpec((1, SUBLANES, LANES), seq3),
                  pl.BlockSpec((SUBLANES, LANES), lambda b, c, pt: (0, 0)),
                  pl.BlockSpec(memory_space=pl.ANY),
                  pl.BlockSpec(memory_space=pl.ANY)],
        out_specs=pl.BlockSpec((1, SUBLANES, BR), seq3),
        scratch_shapes=[pltpu.VMEM((2, LANES, G * PAGE), F32),
                        pltpu.VMEM((2, LANES, G * PAGE), F32),
                        pltpu.SemaphoreType.DMA((2, 2)),
                        pltpu.VMEM((rows, LANES), BF16),
                        pltpu.VMEM((rows, LANES), F32),
                        pltpu.VMEM((rows, LANES), F32),
                        pltpu.VMEM((rows, LANES), F32),
                        pltpu.VMEM((rows, G * PAGE), F32)])
    return pl.pallas_call(
        functools.partial(_diff_decode_kernel, layer=layer, pages_per_step=G, n_chunks=n_chunks,
                          steps=steps, past_len=n_pages * PAGE),
        grid_spec=grid_spec,
        out_shape=jax.ShapeDtypeStruct((db, SUBLANES, BR), F32),
        compiler_params=_cparams(("arbitrary", "arbitrary")),
        name="diff_decode",
    )(page_table, q, k_new, v_new, prm, kc, vc)


MLA_QW = 2 * LANES
MLA_SCALE = (MLA_NOPE + MLA_ROPE) ** -0.5 * LOG2E
ROPE_HALF = MLA_ROPE // 2


def _mla_prep_kernel(cq_ref, ckv_ref, kr_ref, cos_ref, sin_ref, qn_ref, kvn_ref, wuq_ref, wcat_ref,
                     qcat_ref, kcat_ref, c_ref, krout_ref, *, q_transposed):
    tm = cq_ref.shape[0]
    lane = _lane_iota((tm, LANES))
    cos = cos_ref[...]
    sin = sin_ref[...]
    q = _dot(_rms(cq_ref[...], qn_ref[...]).astype(BF16), wuq_ref[...])
    n_nope = MLA_HEADS * MLA_NOPE
    rq = q[:, n_nope:n_nope + LANES]
    rq = rq * cos + pltpu.roll(rq, LANES // 2, 1) * jnp.where(lane < LANES // 2, -sin, sin)
    q_in = (jnp.concatenate([q[:, 0:n_nope], rq], axis=1) * MLA_SCALE).astype(BF16)
    wcat_t = wcat_ref[...]
    qcat_ref[...] = _dot_nt(wcat_t, q_in) if q_transposed else _dot_nt(q_in, wcat_t)
    c = _rms(ckv_ref[...], kvn_ref[...])
    c_ref[...] = c
    kr = kr_ref[...]
    swapped = jnp.where(lane < ROPE_HALF, pltpu.roll(kr, LANES - ROPE_HALF, 1), pltpu.roll(kr, ROPE_HALF, 1))
    kro = kr * cos + swapped * jnp.where(lane < ROPE_HALF, -sin, sin)
    kro = jnp.where(lane < MLA_ROPE, kro, 0.0)
    krout_ref[...] = kro[:, 0:MLA_ROPE]
    kcat_ref[...] = jnp.concatenate([c, kro], axis=1)


def _mla_prep(cq, ckv, kr, cos, sin, lp, tm, q_transposed):
    n = cq.shape[0]
    qw = MLA_HEADS * MLA_QW
    row = lambda i: (i, 0)
    fixed = lambda i: (0, 0)
    q_spec = pl.BlockSpec((qw, tm), lambda i: (0, i)) if q_transposed else pl.BlockSpec((tm, qw), row)
    return pl.pallas_call(
        functools.partial(_mla_prep_kernel, q_transposed=q_transposed),
        grid=(n // tm,),
        in_specs=[pl.BlockSpec((tm, MLA_QR), row), pl.BlockSpec((tm, LANES), row), pl.BlockSpec((tm, LANES), row),
                  pl.BlockSpec((tm, LANES), row), pl.BlockSpec((tm, LANES), row),
                  pl.BlockSpec((1, MLA_QR), fixed), pl.BlockSpec((1, LANES), fixed),
                  pl.BlockSpec(lp["mla_wuq"].shape, fixed), pl.BlockSpec(lp["mla_wcat_t"].shape, fixed)],
        out_specs=[q_spec, pl.BlockSpec((tm, MLA_QW), row),
                   pl.BlockSpec((tm, LANES), row), pl.BlockSpec((tm, MLA_ROPE), row)],
        out_shape=[jax.ShapeDtypeStruct((qw, n) if q_transposed else (n, qw), F32),
                   jax.ShapeDtypeStruct((n, MLA_QW), F32),
                   jax.ShapeDtypeStruct((n, LANES), F32), jax.ShapeDtypeStruct((n, MLA_ROPE), F32)],
        compiler_params=_cparams(("arbitrary",)),
        name="mla_prep",
    )(cq, ckv, kr, cos, sin, lp["mla_q_norm"], lp["mla_kv_norm"], lp["mla_wuq"], lp["mla_wcat_t"])


def _mla_stack_q(qcat):
    return jnp.concatenate([qcat[:, h * MLA_QW:(h + 1) * MLA_QW] for h in range(MLA_HEADS)], axis=0).astype(BF16)


def _mla_finish(acc_ref, l_ref, wuv_ref, rows):
    y = jnp.zeros((rows, BR), F32)
    for h in range(MLA_HEADS):
        o_lat = acc_ref[h * rows:(h + 1) * rows, :] / l_ref[h * rows:(h + 1) * rows, :]
        y = y + _dot(o_lat.astype(BF16), wuv_ref[h])
    return y


def _mla_flash_kernel(q_ref, k_ref, ct_ref, wuvt_ref, o_ref, qs_ref, m_ref, acc_ref, *, tq, tk):
    q0 = pl.program_id(0) * tq
    qs_ref[...] = jnp.concatenate([q_ref[h * MLA_QW:(h + 1) * MLA_QW, :] for h in range(MLA_HEADS)],
                                  axis=1).astype(BF16)
    _flash_tiles(k_ref, ct_ref, qs_ref, m_ref, acc_ref, q0, tq, tk)
    y = jnp.zeros((BR, tq), F32)
    for h in range(MLA_HEADS):
        cols = slice(h * tq, (h + 1) * tq)
        o_lat = acc_ref[0:MLA_KVR, cols] / acc_ref[MLA_KVR:MLA_KVR + 1, cols]
        y = y + _dot(wuvt_ref[h], o_lat.astype(BF16))
    o_ref[...] = y


def _mla_flash(qcat_t, kcat, c_t, wuv_t, tq, tk):
    n = qcat_t.shape[1]
    assert tk % tq == 0 and n % tk == 0 and tq % FLASH_COLS == 0
    width = MLA_HEADS * tq
    return pl.pallas_call(
        functools.partial(_mla_flash_kernel, tq=tq, tk=tk),
        grid=(n // tq,),
        in_specs=[pl.BlockSpec((MLA_HEADS * MLA_QW, tq), lambda qi: (0, qi)),
                  _resident(kcat.shape),
                  _resident(c_t.shape),
                  _resident(wuv_t.shape)],
        out_specs=pl.BlockSpec((BR, tq), lambda qi: (0, qi)),
        out_shape=jax.ShapeDtypeStruct((BR, n), F32),
        scratch_shapes=[pltpu.VMEM((MLA_QW, width), BF16),
                        pltpu.VMEM((1, width), F32),
                        pltpu.VMEM((MLA_KVR + SUBLANES, width), F32)],
        compiler_params=_cparams(("arbitrary",)),
        name="mla_flash",
    )(qcat_t, kcat, c_t, wuv_t)


def _mla_decode_kernel(pt_ref, q_ref, kn_ref, wuv_ref, cc_ref, rc_ref, o_ref,
                       cbuf, rbuf, sems, qs_ref, m_ref, l_ref, acc_ref, p_ref,
                       *, layer, pages_per_step, n_chunks, steps):
    G = pages_per_step
    R8 = SUBLANES
    c = pl.program_id(1)
    slot, start_next, drain = _stream_pages(pt_ref, (cc_ref, rc_ref), (cbuf, rbuf), sems, layer, G, n_chunks)

    @pl.when(c == 0)
    def _():
        qs_ref[...] = _mla_stack_q(q_ref[0])
        m_ref[...] = jnp.full(m_ref.shape, NEG, F32)
        l_ref[...] = jnp.zeros(l_ref.shape, F32)
        acc_ref[...] = jnp.zeros(acc_ref.shape, F32)

    qs = qs_ref[...]
    q_lat = qs[:, 0:MLA_KVR]
    q_rope = qs[:, MLA_KVR:MLA_KVR + MLA_ROPE]

    cb = cbuf[slot].astype(BF16)
    s = _dot_nt(q_lat, cb) + _dot(q_rope, rbuf[slot].astype(BF16))
    start_next()
    for h in range(MLA_HEADS):
        alpha = _online_update(s[h * R8:(h + 1) * R8], h, R8, m_ref, l_ref, p_ref)
        acc_ref[h * R8:(h + 1) * R8, :] = alpha * acc_ref[h * R8:(h + 1) * R8, :]
    acc_ref[...] = acc_ref[...] + _dot(p_ref[...].astype(BF16), cb)
    drain()

    @pl.when(c == n_chunks - 1)
    def _():
        kn = jnp.concatenate([kn_ref[0], jnp.zeros((PAGE - R8, MLA_QW), F32)], axis=0).astype(BF16)
        sn = _dot_nt(qs, kn)
        kt = _lane_iota((R8, PAGE))
        vis = (kt <= _row_iota((R8, PAGE))) & (kt < steps)
        for h in range(MLA_HEADS):
            alpha = _online_update(jnp.where(vis, sn[h * R8:(h + 1) * R8], NEG), h, R8, m_ref, l_ref, p_ref)
            acc_ref[h * R8:(h + 1) * R8, :] = alpha * acc_ref[h * R8:(h + 1) * R8, :]
        acc_ref[...] = acc_ref[...] + _dot(p_ref[:, 0:PAGE].astype(BF16), kn[:, 0:MLA_KVR])
        o_ref[0] = _mla_finish(acc_ref, l_ref, wuv_ref, R8)


def _mla_decode(page_table, qcat, kcat_new, wuv, cc, rc, layer, steps, pages_per_step):
    db, n_pages = page_table.shape
    G = pages_per_step
    n_chunks = n_pages // G
    rows = MLA_HEADS * SUBLANES
    seq3 = lambda b, c, pt: (b, 0, 0)
    grid_spec = pltpu.PrefetchScalarGridSpec(
        num_scalar_prefetch=1,
        grid=(db, n_chunks),
        in_specs=[pl.BlockSpec((1, SUBLANES, MLA_HEADS * MLA_QW), seq3),
                  pl.BlockSpec((1, SUBLANES, MLA_QW), seq3),
                  pl.BlockSpec(wuv.shape, lambda b, c, pt: (0, 0, 0)),
                  pl.BlockSpec(memory_space=pl.ANY),
                  pl.BlockSpec(memory_space=pl.ANY)],
        out_specs=pl.BlockSpec((1, SUBLANES, BR), seq3),
        scratch_shapes=[pltpu.VMEM((2, G * PAGE, MLA_KVR), F32),
                        pltpu.VMEM((2, MLA_ROPE, G * PAGE), F32),
                        pltpu.SemaphoreType.DMA((2, 2)),
                        pltpu.VMEM((rows, MLA_QW), BF16),
                        pltpu.VMEM((rows, LANES), F32),
                        pltpu.VMEM((rows, LANES), F32),
                        pltpu.VMEM((rows, LANES), F32),
                        pltpu.VMEM((rows, G * PAGE), F32)])
    return pl.pallas_call(
        functools.partial(_mla_decode_kernel, layer=layer, pages_per_step=G, n_chunks=n_chunks, steps=steps),
        grid_spec=grid_spec,
        out_shape=jax.ShapeDtypeStruct((db, SUBLANES, BR), F32),
        compiler_params=_cparams(("arbitrary", "arbitrary")),
        name="mla_decode",
    )(page_table, qcat, kcat_new, wuv, cc, rc)


def _mem_attn_kernel(q_ref, mk_ref, mv_ref, o_ref, *, key_minor):
    rows = q_ref.shape[1]
    lane_head = _lane_iota((rows, BR)) // MEM_DIM
    q = q_ref[0] * (MEM_DIM ** -0.5)
    qs = jnp.concatenate([jnp.where(lane_head == h, q, 0.0) for h in range(MEM_HEADS)], axis=0).astype(BF16)
    mk = mk_ref[0].astype(BF16)
    mv = mv_ref[0].astype(BF16)
    s = _dot(qs, mk) if key_minor else _dot_nt(qs, mk)
    p = jnp.exp(s - jnp.max(s, axis=1, keepdims=True))
    p = (p / jnp.sum(p, axis=1, keepdims=True)).astype(BF16)
    r = _dot_nt(p, mv) if key_minor else _dot(p, mv)
    y = jnp.zeros((rows, BR), F32)
    for h in range(MEM_HEADS):
        y = y + jnp.where(lane_head == h, r[h * rows:(h + 1) * rows], 0.0)
    o_ref[0] = y


def _mem_attn(q3, mk3, mv3, key_minor):
    b, rows, _ = q3.shape
    per_seq = mk3.shape[0] == b and b > 1
    kv_map = (lambda i: (i, 0, 0)) if per_seq else (lambda i: (0, 0, 0))
    return pl.pallas_call(
        functools.partial(_mem_attn_kernel, key_minor=key_minor),
        grid=(b,),
        in_specs=[pl.BlockSpec((1, rows, BR), lambda i: (i, 0, 0)),
                  pl.BlockSpec((1,) + mk3.shape[1:], kv_map),
                  pl.BlockSpec((1,) + mv3.shape[1:], kv_map)],
        out_specs=pl.BlockSpec((1, rows, BR), lambda i: (i, 0, 0)),
        out_shape=jax.ShapeDtypeStruct((b, rows, BR), F32),
        compiler_params=_cparams(("arbitrary",)),
        name="mem_attn",
    )(q3, mk3, mv3)


def _matmul_kernel(x_ref, w_ref, o_ref):
    o_ref[...] = _dot(x_ref[...].astype(BF16), w_ref[...])


def _matmul(x, w):
    m, k = x.shape
    n = w.shape[1]
    return pl.pallas_call(
        _matmul_kernel,
        grid=(1,),
        in_specs=[pl.BlockSpec((m, k), lambda i: (0, 0)), pl.BlockSpec((k, n), lambda i: (0, 0))],
        out_specs=pl.BlockSpec((m, n), lambda i: (0, 0)),
        out_shape=jax.ShapeDtypeStruct((m, n), F32),
        compiler_params=_cparams(("arbitrary",)),
        name="mem_proj",
    )(x, w)


def _merge_kernel(x_ref, y0, y1, y2, y3, y4, gpre_ref, gpost_ref, wg_ref, wb_ref, wo_ref, o_ref, *, transposed):
    x = x_ref[...]
    d = x.shape[1]
    hn = _rms(x, gpre_ref[...]).astype(BF16)
    m = jnp.zeros(x.shape, F32)
    for i, y_ref in enumerate((y0, y1, y2, y3, y4)):
        gate = _sigmoid(_dot_nt(hn, wg_ref[i * d:(i + 1) * d, :]))
        y = y_ref[...].astype(BF16)
        m = m + gate * (_dot_tn(y, wb_ref[i]) if i in transposed else _dot(y, wb_ref[i]))
    o_ref[...] = x + _rms(_dot(m.astype(BF16), wo_ref[...]), gpost_ref[...])


def _merge_out(x, ys, lp, tm, transposed=()):
    n, d = x.shape
    row = lambda i: (i, 0)
    fixed = lambda i: (0, 0)
    y_specs = [pl.BlockSpec((BR, tm), lambda i: (0, i)) if j in transposed else pl.BlockSpec((tm, BR), row)
               for j in range(N_BRANCH)]
    return pl.pallas_call(
        functools.partial(_merge_kernel, transposed=transposed),
        grid=(n // tm,),
        in_specs=[pl.BlockSpec((tm, d), row)] + y_specs + [
            pl.BlockSpec((1, d), fixed), pl.BlockSpec((1, d), fixed),
            _resident(lp["w_gate"].shape), _resident(lp["w_branch"].shape), _resident(lp["w_out"].shape)],
        out_specs=pl.BlockSpec((tm, d), row),
        out_shape=jax.ShapeDtypeStruct((n, d), F32),
        compiler_params=_cparams(("arbitrary",)),
        name="merge_out",
    )(x, *ys, lp["g_pre_mix"], lp["g_post_mix"], lp["w_gate"], lp["w_branch"], lp["w_out"])


FFN_CHUNK = 2 * LANES


def _ffn_kernel(x_ref, st_ref, gpre_ref, gpost_ref, wi_ref, cw_ref, cb_ref, wo_ref, o_ref, ns_ref,
                ext_ref, acc_ref, *, shift, pad, width):
    tm = x_ref.shape[0]

    @pl.when(pl.program_id(0) == 0)
    def _():
        ext_ref[0:pad, :] = st_ref[...]

    x = x_ref[...]
    hn = _rms(x, gpre_ref[...]).astype(BF16)
    for c0 in range(0, width, FFN_CHUNK):
        a = _dot(hn, wi_ref[:, c0:c0 + FFN_CHUNK])
        gate = _dot(hn, wi_ref[:, width + c0:width + c0 + FFN_CHUNK])
        ext_ref[pad:pad + tm, c0:c0 + FFN_CHUNK] = a
        conv = cb_ref[:, c0:c0 + FFN_CHUNK] + cw_ref[FFN_CONV - 1:FFN_CONV, c0:c0 + FFN_CHUNK] * a
        for j in range(FFN_CONV - 1):
            off = pad - (FFN_CONV - 1 - j) * shift
            conv = conv + cw_ref[j:j + 1, c0:c0 + FFN_CHUNK] * ext_ref[off:off + tm, c0:c0 + FFN_CHUNK]
        act = (_gelu_tanh(conv) * gate).astype(BF16)
        contrib = _dot(act, wo_ref[c0:c0 + FFN_CHUNK, :])
        if c0 == 0:
            acc_ref[...] = contrib
        else:
            acc_ref[...] = acc_ref[...] + contrib
    o_ref[...] = x + _rms(acc_ref[...], gpost_ref[...])
    tail = ext_ref[tm:tm + pad, :]
    ns_ref[...] = tail
    ext_ref[0:pad, :] = tail


def _ffn(x, state, lp, shift, tm):
    n, d = x.shape
    pad, width = state.shape
    assert tm >= pad and pad >= (FFN_CONV - 1) * shift and width % FFN_CHUNK == 0
    row = lambda i: (i, 0)
    fixed = lambda i: (0, 0)
    return pl.pallas_call(
        functools.partial(_ffn_kernel, shift=shift, pad=pad, width=width),
        grid=(n // tm,),
        in_specs=[pl.BlockSpec((tm, d), row), pl.BlockSpec((pad, width), fixed),
                  pl.BlockSpec((1, d), fixed), pl.BlockSpec((1, d), fixed),
                  _resident(lp["w_ffn_in"].shape),
                  pl.BlockSpec((FFN_CONV, width), fixed), pl.BlockSpec((1, width), fixed),
                  _resident(lp["w_ffn_out"].shape)],
        out_specs=[pl.BlockSpec((tm, d), row), pl.BlockSpec((pad, width), fixed)],
        out_shape=[jax.ShapeDtypeStruct((n, d), F32), jax.ShapeDtypeStruct((pad, width), F32)],
        scratch_shapes=[pltpu.VMEM((pad + tm, width), F32), pltpu.VMEM((tm, d), F32)],
        compiler_params=_cparams(("arbitrary",)),
        name="conv_ffn",
    )(x, state, lp["g_pre_ffn"], lp["g_post_ffn"], lp["w_ffn_in"], lp["ffn_conv_w"], lp["ffn_conv_b"],
      lp["w_ffn_out"])


def _rope_tables(pos):
    freqs = jnp.power(ROPE_THETA, -jnp.arange(ROPE_HALF, dtype=F32) / ROPE_HALF)
    ang = pos.astype(F32)[:, None] * freqs
    reps = LANES // ROPE_HALF
    return jnp.tile(jnp.cos(ang), (1, reps)), jnp.tile(jnp.sin(ang), (1, reps))


def _prep_layer(li, P):
    d_model = P["w_in"].shape[1]
    w_in_t = jnp.transpose(P["w_in"], (2, 0, 1))[:, li, :]
    bounds = np.cumsum((0,) + IN_WIDTHS)
    seg = [w_in_t[bounds[i]:bounds[i + 1]] for i in range(len(IN_WIDTHS))]
    grp = DIFF_HEADS // DIFF_KV
    seg[3] = seg[3].reshape(DIFF_KV, grp, 2 * DIFF_DIM, d_model).swapaxes(0, 1).reshape(-1, d_model)
    seg = [jnp.pad(s, ((0, wd - s.shape[0]), (0, 0))) for s, wd in zip(seg, PROJ_WIDTHS)]
    lp = {"w_cat": jnp.concatenate(seg, axis=0).astype(BF16),
          "w_gate": w_in_t[bounds[-1]:].astype(BF16)}
    for name in ("g_pre_mix", "g_post_mix", "g_pre_ffn", "g_post_ffn", "ssd_conv_b", "ssd_norm", "gla_bg",
                 "mla_q_norm", "mla_kv_norm", "ffn_conv_b"):
        lp[name] = P[name][li][None]
    lp["ssd_conv_w"] = P["ssd_conv_w"][li]
    lp["ffn_conv_w"] = P["ffn_conv_w"][li]
    prm = jnp.zeros((SUBLANES, LANES), F32)
    lp["ssd_prm"] = prm.at[0, :SSD_HEADS].set(P["ssd_dt_bias"][li]).at[1, :SSD_HEADS].set(-jnp.exp(P["ssd_a_log"][li]))
    lp["ssd_dskip"] = jnp.repeat(P["ssd_d"][li], SSD_HEAD_DIM)[None]
    lam_init = 0.8 - 0.6 * math.exp(-0.3 * li)
    lam = (jnp.exp(jnp.sum(P["diff_lq1"][li] * P["diff_lk1"][li]))
           - jnp.exp(jnp.sum(P["diff_lq2"][li] * P["diff_lk2"][li])) + lam_init)
    lp["diff_prm"] = prm.at[0].set(jnp.tile(P["diff_norm"][li], DIFF_KV) * (1.0 - lam_init)).at[1].set(lam)
    lp["gla_wg2"] = jnp.zeros((LANES, LANES), F32).at[:GLA_RANK].set(P["gla_wg2"][li]).astype(BF16)
    lp["gla_norm"] = jnp.tile(P["gla_norm"][li], GLA_HEADS)[None]
    wuq = P["mla_wuq"][li].reshape(MLA_QR, MLA_HEADS, MLA_NOPE + MLA_ROPE)
    lp["mla_wuq"] = jnp.concatenate(
        [wuq[:, :, :MLA_NOPE].reshape(MLA_QR, -1),
         wuq[:, :, MLA_NOPE:MLA_NOPE + ROPE_HALF].reshape(MLA_QR, -1),
         wuq[:, :, MLA_NOPE + ROPE_HALF:].reshape(MLA_QR, -1)], axis=1).astype(BF16)
    n_nope = MLA_HEADS * MLA_NOPE
    wcat = jnp.zeros((n_nope + LANES, MLA_HEADS * MLA_QW), F32)
    wuv = jnp.zeros((MLA_HEADS, MLA_KVR, BR), F32)
    eye = jnp.eye(ROPE_HALF, dtype=F32)
    for h in range(MLA_HEADS):
        wcat = wcat.at[h * MLA_NOPE:(h + 1) * MLA_NOPE, h * MLA_QW:h * MLA_QW + MLA_KVR].set(P["mla_wuk"][li][:, h, :].T)
        for half in range(2):
            r0 = n_nope + half * (LANES // 2) + h * ROPE_HALF
            c0 = h * MLA_QW + MLA_KVR + half * ROPE_HALF
            wcat = wcat.at[r0:r0 + ROPE_HALF, c0:c0 + ROPE_HALF].set(eye)
        wuv = wuv.at[h, :, h * MLA_V:(h + 1) * MLA_V].set(P["mla_wuv"][li][:, h, :])
    lp["mla_wcat_t"] = wcat.T.astype(BF16)
    lp["mla_wuv"] = wuv.astype(BF16)
    lp["mla_wuv_t"] = wuv.transpose(0, 2, 1).astype(BF16)
    lp["w_mem_kv"] = jnp.concatenate([P["w_mem_k"][li], P["w_mem_v"][li]], axis=1).astype(BF16)
    wb = P["w_branch"][li]
    wb_diff = wb[1].reshape(DIFF_KV, grp, 2 * DIFF_DIM, d_model).swapaxes(0, 1).reshape(BR, d_model)
    lp["w_branch"] = wb.at[1].set(wb_diff).astype(BF16)
    lp["w_out"] = P["w_out"][li].astype(BF16)
    lp["w_ffn_in"] = P["w_ffn_in"][li].astype(BF16)
    lp["w_ffn_out"] = P["w_ffn_out"][li].astype(BF16)
    return lp


def _row_tile(n, want):
    return want if n % want == 0 else n


def _layer(x, lp, li, db, steps, st, past):
    n = x.shape[0]
    tm = _row_tile(n, 256)
    prompt = past is None
    (z, xbc, dt, dq, dk, dv, gq, gk, gv, gg, gr, cq, ckv, kr, mq) = _proj_in(
        x, lp["g_pre_mix"], lp["w_cat"], _row_tile(n, 512), transposed=(3,) if prompt else ())
    y_ssd, ssd_conv, ssd_h = _ssd_branch(z, xbc, dt, st["ssd_conv"], st["ssd_h"], lp, db, steps)
    y_gla, gla_s = _gla_branch(gq, gk, gv, gg, gr, st["gla_s"], lp, db, steps)
    past_len = 0 if prompt else past["page_table"].shape[1] * PAGE
    pos = jnp.repeat(past_len + jnp.arange(steps, dtype=jnp.int32), db)
    cos, sin = _rope_tables(pos)
    qcat, kcat, c_rows, kr_rows = _mla_prep(cq, ckv, kr, cos, sin, lp, tm, q_transposed=prompt)
    if prompt:
        tk = _row_tile(n, 512)
        tq_diff, tq_mla = _row_tile(n, 256), _row_tile(n, 512)
        gcol = jnp.broadcast_to(lp["diff_prm"][0][:, None], (LANES, tq_diff))
        lam_row = jnp.broadcast_to(lp["diff_prm"][1, 0], (1, tq_diff))
        y_diff = _diff_flash(dq, _diff_keys_alibi(dk, pos), _value_tiles_t(dv.astype(BF16), tk),
                             lam_row, gcol, tq_diff, tk)
        kb = kcat.astype(BF16)
        y_mla = _mla_flash(qcat, kb, _value_tiles_t(kb[:, :MLA_KVR], tk), lp["mla_wuv_t"], tq_mla, tk)
        y_mem = _mem_attn(mq.reshape(n // tm, tm, BR), st["mem_k"], st["mem_v"], key_minor=False).reshape(n, BR)
    else:
        seq = lambda a: _to_seq_major(a, db, steps, SUBLANES)
        pt = past["page_table"]
        g_pages = math.gcd(pt.shape[1], 64)
        y_diff = _to_time_major(_diff_decode(pt, seq(dq), seq(dk), seq(dv), lp["diff_prm"], past["diff_k"],
                                             past["diff_v"], li, steps, g_pages), steps)
        y_mla = _to_time_major(_mla_decode(pt, seq(qcat), seq(kcat), lp["mla_wuv"], past["mla_ckv"],
                                           past["mla_kr"], li, steps, g_pages), steps)
        y_mem = _to_time_major(_mem_attn(seq(mq), st["mem_k"], st["mem_v"], key_minor=True), steps)
    x = _merge_out(x, (y_ssd, y_diff, y_gla, y_mla, y_mem), lp, tm, transposed=(1, 3) if prompt else ())
    x, ffn_conv = _ffn(x, st["ffn_conv"], lp, shift=db, tm=_row_tile(n, 512) if db == 1 else n)
    return x, (dk, dv, c_rows, kr_rows, ssd_h, ssd_conv, gla_s, ffn_conv)


def kernel(x_prompt, x_sample, cache_diff_k, cache_diff_v, cache_mla_ckv, cache_mla_krope, cache_mem_k, cache_mem_v, state_ssd, state_ssd_conv, state_gla, state_ffn_conv, page_table, mem_prompt, w_in, ssd_conv_w, ssd_conv_b, ssd_dt_bias, ssd_a_log, ssd_d, ssd_norm, diff_lq1, diff_lk1, diff_lq2, diff_lk2, diff_norm, gla_wg2, gla_bg, gla_norm, mla_q_norm, mla_kv_norm, mla_wuq, mla_wuk, mla_wuv, w_mem_k, w_mem_v, w_branch, w_out, g_pre_mix, g_post_mix, g_pre_ffn, g_post_ffn, w_ffn_in, ffn_conv_w, ffn_conv_b, w_ffn_out):
    P = dict(w_in=w_in, ssd_conv_w=ssd_conv_w, ssd_conv_b=ssd_conv_b, ssd_dt_bias=ssd_dt_bias,
             ssd_a_log=ssd_a_log, ssd_d=ssd_d, ssd_norm=ssd_norm, diff_lq1=diff_lq1, diff_lk1=diff_lk1,
             diff_lq2=diff_lq2, diff_lk2=diff_lk2, diff_norm=diff_norm, gla_wg2=gla_wg2, gla_bg=gla_bg,
             gla_norm=gla_norm, mla_q_norm=mla_q_norm, mla_kv_norm=mla_kv_norm, mla_wuq=mla_wuq,
             mla_wuk=mla_wuk, mla_wuv=mla_wuv, w_mem_k=w_mem_k, w_mem_v=w_mem_v, w_branch=w_branch,
             w_out=w_out, g_pre_mix=g_pre_mix, g_post_mix=g_post_mix, g_pre_ffn=g_pre_ffn,
             g_post_ffn=g_post_ffn, w_ffn_in=w_ffn_in, ffn_conv_w=ffn_conv_w, ffn_conv_b=ffn_conv_b,
             w_ffn_out=w_ffn_out)
    depth = w_in.shape[0]
    pb, seq_len, d_model = x_prompt.shape
    assert pb == 1
    db, steps, _ = x_sample.shape
    mem_len = mem_prompt.shape[1]
    ffn_dim = ffn_conv_w.shape[2]
    past = {"page_table": page_table,
            "diff_k": jnp.transpose(cache_diff_k, (0, 1, 3, 4, 2)).reshape(depth, -1, LANES, PAGE),
            "diff_v": jnp.transpose(cache_diff_v, (0, 1, 3, 4, 2)).reshape(depth, -1, LANES, PAGE),
            "mla_ckv": cache_mla_ckv,
            "mla_kr": jnp.transpose(cache_mla_krope, (0, 1, 3, 2))}
    mem_kt = jnp.transpose(cache_mem_k, (0, 1, 3, 4, 2)).reshape(depth, db, BR, mem_len)
    mem_vt = jnp.transpose(cache_mem_v, (0, 1, 3, 4, 2)).reshape(depth, db, BR, mem_len)
    xp = x_prompt[0]
    xs = x_sample.transpose(1, 0, 2).reshape(steps * db, d_model)
    new_p, new_s, mem_kp, mem_vp = [], [], [], []
    for li in range(depth):
        lp = _prep_layer(li, P)
        mkv = _matmul(mem_prompt[0], lp["w_mem_kv"])
        mk, mv = mkv[:, :BR], mkv[:, BR:]
        mem_kp.append(mk.reshape(1, mem_len, MEM_HEADS, MEM_DIM))
        mem_vp.append(mv.reshape(1, mem_len, MEM_HEADS, MEM_DIM))
        st_p = {"ssd_conv": jnp.zeros((SUBLANES, SSD_CONV_DIM), F32),
                "ssd_h": jnp.zeros((1, SSD_HEADS, SSD_HEAD_DIM, SSD_STATE), F32),
                "gla_s": jnp.zeros((1, GLA_HEADS, GLA_DK, GLA_DV), F32),
                "ffn_conv": jnp.zeros((SUBLANES, ffn_dim), F32),
                "mem_k": mk[None], "mem_v": mv[None]}
        xp, o = _layer(xp, lp, li, 1, seq_len, st_p, None)
        dk, dv, c_rows, kr_rows, ssd_h, ssd_conv, gla_s, ffn_conv = o
        new_p.append((dk.reshape(1, seq_len, DIFF_KV, 2 * DIFF_DIM), dv.reshape(1, seq_len, DIFF_KV, 2 * DIFF_DIM),
                      c_rows[None], kr_rows[None], ssd_h, ssd_conv[None, SUBLANES - (SSD_CONV - 1):],
                      gla_s, ffn_conv[None, SUBLANES - (FFN_CONV - 1):]))
        st_s = {"ssd_conv": state_ssd_conv[li].transpose(1, 0, 2).reshape((SSD_CONV - 1) * db, SSD_CONV_DIM),
                "ssd_h": state_ssd[li],
                "gla_s": state_gla[li],
                "ffn_conv": state_ffn_conv[li].transpose(1, 0, 2).reshape((FFN_CONV - 1) * db, ffn_dim),
                "mem_k": mem_kt[li], "mem_v": mem_vt[li]}
        xs, o = _layer(xs, lp, li, db, steps, st_s, past)
        dk, dv, c_rows, kr_rows, ssd_h, ssd_conv, gla_s, ffn_conv = o
        bm = lambda a: a.reshape(-1, db, a.shape[-1]).transpose(1, 0, 2)
        new_s.append((bm(dk).reshape(db, steps, DIFF_KV, 2 * DIFF_DIM), bm(dv).reshape(db, steps, DIFF_KV, 2 * DIFF_DIM),
                      bm(c_rows), bm(kr_rows), ssd_h, bm(ssd_conv), gla_s, bm(ffn_conv)))
    stk = lambda outs, j: jnp.stack([o[j] for o in outs])
    yp = xp[None]
    ys = xs.reshape(steps, db, d_model).transpose(1, 0, 2)
    return (yp, ys,
            stk(new_p, 0), stk(new_p, 1), stk(new_p, 2), stk(new_p, 3),
            jnp.stack(mem_kp), jnp.stack(mem_vp),
            stk(new_p, 4), stk(new_p, 5), stk(new_p, 6), stk(new_p, 7),
            stk(new_s, 0), stk(new_s, 1), stk(new_s, 2), stk(new_s, 3),
            stk(new_s, 4), stk(new_s, 5), stk(new_s, 6), stk(new_s, 7))
```

```python
import functools
import math

import numpy as np
import jax
import jax.numpy as jnp
from jax import lax
from jax.experimental import pallas as pl
from jax.experimental.pallas import tpu as pltpu

F32 = jnp.float32
BF16 = jnp.bfloat16

BR = 256
N_BRANCH = 5
SSD_HEADS, SSD_HEAD_DIM, SSD_GROUPS, SSD_STATE, SSD_CONV = 4, 64, 2, 128, 4
SSD_CONV_DIM = BR + 2 * SSD_GROUPS * SSD_STATE
DIFF_HEADS, DIFF_KV, DIFF_DIM = 4, 2, 32
GLA_HEADS, GLA_DK, GLA_DV, GLA_RANK, GLA_TAU = 4, 32, 64, 16, 16.0
MLA_HEADS, MLA_QR, MLA_KVR, MLA_NOPE, MLA_ROPE, MLA_V = 4, 256, 128, 64, 32, 64
ROPE_THETA = 10000.0
MEM_HEADS, MEM_DIM = 4, 64
FFN_CONV = 3
PAGE = 128
EPS = 1e-6
NEG = -1e30

IN_WIDTHS = (BR, SSD_CONV_DIM, SSD_HEADS, DIFF_HEADS * 2 * DIFF_DIM, DIFF_KV * 2 * DIFF_DIM,
             DIFF_KV * 2 * DIFF_DIM, GLA_HEADS * GLA_DK, GLA_HEADS * GLA_DK, GLA_HEADS * GLA_DV,
             GLA_RANK, GLA_HEADS * GLA_DV, MLA_QR, MLA_KVR, MLA_ROPE, MEM_HEADS * MEM_DIM)
PROJ_WIDTHS = (256, 768, 128, 256, 128, 128, 128, 128, 256, 128, 256, 256, 128, 128, 256)

LANES = 128
SUBLANES = 8
VMEM_LIMIT = 56 * 1024 * 1024


def _cparams(sem):
    return pltpu.CompilerParams(dimension_semantics=sem, vmem_limit_bytes=VMEM_LIMIT)


def _dot(a, b):
    return jnp.dot(a, b, preferred_element_type=F32)


def _dot_nt(a, b):
    return lax.dot_general(a, b, (((1,), (1,)), ((), ())), preferred_element_type=F32)


def _dot_tn(a, b):
    return lax.dot_general(a, b, (((0,), (0,)), ((), ())), preferred_element_type=F32)


def _split3(x):
    hi = x.astype(BF16)
    r = x - hi.astype(F32)
    mid = r.astype(BF16)
    lo = (r - mid.astype(F32)).astype(BF16)
    return hi, mid, lo


def _xdot_l(m01, x):
    hi, mid, lo = _split3(x)
    return _dot(m01, hi) + _dot(m01, mid) + _dot(m01, lo)


def _xdot_r(x, m01):
    hi, mid, lo = _split3(x)
    return _dot(hi, m01) + _dot(mid, m01) + _dot(lo, m01)


def _xdot_nt(m01, x):
    hi, mid, lo = _split3(x)
    return _dot_nt(m01, hi) + _dot_nt(m01, mid) + _dot_nt(m01, lo)


def _rms(x, g):
    return x * lax.rsqrt(jnp.mean(x * x, axis=-1, keepdims=True) + EPS) * g


def _silu(x):
    return x * (1.0 / (1.0 + jnp.exp(-x)))


def _sigmoid(x):
    return 1.0 / (1.0 + jnp.exp(-x))


def _softplus(x):
    return jnp.maximum(x, 0.0) + jnp.log(1.0 + jnp.exp(-jnp.abs(x)))


def _log_sigmoid(x):
    return -_softplus(-x)


def _gelu_tanh(x):
    c = math.sqrt(2.0 / math.pi)
    return 0.5 * x * (1.0 + jnp.tanh(c * (x + 0.044715 * (x * x * x))))


def _lane_iota(shape):
    return lax.broadcasted_iota(jnp.int32, shape, len(shape) - 1)


def _row_iota(shape):
    return lax.broadcasted_iota(jnp.int32, shape, len(shape) - 2)


def _proj_in_kernel(x_ref, g_ref, w_ref, *o_refs, transposed):
    xn = _rms(x_ref[...], g_ref[...]).astype(BF16)
    off = 0
    for i, (o_ref, wd) in enumerate(zip(o_refs, PROJ_WIDTHS)):
        w = w_ref[off:off + wd, :]
        o_ref[...] = _dot_nt(w, xn) if i in transposed else _dot_nt(xn, w)
        off += wd


def _proj_in(x, g, w, tm, transposed=()):
    n, d = x.shape
    spec = lambda i, wd: (pl.BlockSpec((wd, tm), lambda r: (0, r)) if i in transposed
                          else pl.BlockSpec((tm, wd), lambda r: (r, 0)))
    shape = lambda i, wd: jax.ShapeDtypeStruct((wd, n) if i in transposed else (n, wd), F32)
    return pl.pallas_call(
        functools.partial(_proj_in_kernel, transposed=transposed),
        grid=(n // tm,),
        in_specs=[pl.BlockSpec((tm, d), lambda i: (i, 0)),
                  pl.BlockSpec((1, d), lambda i: (0, 0)),
                  _resident(w.shape)],
        out_specs=[spec(i, wd) for i, wd in enumerate(PROJ_WIDTHS)],
        out_shape=[shape(i, wd) for i, wd in enumerate(PROJ_WIDTHS)],
        compiler_params=_cparams(("arbitrary",)),
        name="proj_in",
    )(x, g, w)


def _conv_taps(ext_ref, cur, w_ref, b_ref, taps, shift, pad, tm):
    acc = b_ref[...] + w_ref[taps - 1:taps, :] * cur
    for j in range(taps - 1):
        off = pad - (taps - 1 - j) * shift
        acc = acc + w_ref[j:j + 1, :] * ext_ref[off:off + tm, :]
    return acc


def _conv_silu_kernel(x_ref, st_ref, w_ref, b_ref, y_ref, ns_ref, ext_ref, *, taps, shift, pad):
    tm = x_ref.shape[0]

    @pl.when(pl.program_id(0) == 0)
    def _():
        ext_ref[0:pad, :] = st_ref[...]

    cur = x_ref[...]
    ext_ref[pad:pad + tm, :] = cur
    y_ref[...] = _silu(_conv_taps(ext_ref, cur, w_ref, b_ref, taps, shift, pad, tm))
    tail = ext_ref[tm:tm + pad, :]
    ns_ref[...] = tail
    ext_ref[0:pad, :] = tail


def _conv_silu(x, state, w, b, shift, tm):
    n, c = x.shape
    taps = w.shape[0]
    pad = state.shape[0]
    assert tm >= pad and pad >= (taps - 1) * shift
    return pl.pallas_call(
        functools.partial(_conv_silu_kernel, taps=taps, shift=shift, pad=pad),
        grid=(n // tm,),
        in_specs=[pl.BlockSpec((tm, c), lambda i: (i, 0)),
                  pl.BlockSpec((pad, c), lambda i: (0, 0)),
                  pl.BlockSpec((taps, c), lambda i: (0, 0)),
                  pl.BlockSpec((1, c), lambda i: (0, 0))],
        out_specs=[pl.BlockSpec((tm, c), lambda i: (i, 0)),
                   pl.BlockSpec((pad, c), lambda i: (0, 0))],
        out_shape=[jax.ShapeDtypeStruct((n, c), F32), jax.ShapeDtypeStruct((pad, c), F32)],
        scratch_shapes=[pltpu.VMEM((pad + tm, c), F32)],
        compiler_params=_cparams(("arbitrary",)),
        name="conv_silu",
    )(x, state, w, b)


def _ssd_kernel(z_ref, xbc_ref, dt_ref, h0_ref, prm_ref, dskip_ref, ng_ref, tri_ref, exp_ref, eye_ref,
                y_ref, hout_ref, st_ref, *, chunk, t_valid, n_chunks):
    L = chunk
    c = pl.program_id(1)

    @pl.when(c == 0)
    def _():
        st_ref[...] = h0_ref[0].reshape(SSD_HEADS * SSD_HEAD_DIM, SSD_STATE).T

    xbc = xbc_ref[0]
    xs = xbc[:, 0:BR]
    bm = xbc[:, BR:2 * BR].astype(BF16)
    cm = xbc[:, 2 * BR:3 * BR].astype(BF16)
    dt = _softplus(dt_ref[0] + prm_ref[0:1, :])
    if t_valid < L:
        dt = jnp.where(_row_iota((L, LANES)) < t_valid, dt, 0.0)
    d_a = dt * prm_ref[1:2, :]
    tri = tri_ref[...]
    expand = exp_ref[...]
    cs = _xdot_l(tri, d_a)
    cs_exp = _xdot_r(cs, expand)
    dt_exp = _xdot_r(dt, expand)
    cs_t = _xdot_nt(eye_ref[...], cs)
    xdt = xs * dt_exp
    xdt_b = xdt.astype(BF16)
    causal = _row_iota((L, L)) >= _lane_iota((L, L))
    lane_head = _lane_iota((L, BR)) // SSD_HEAD_DIM
    st = st_ref[...]
    st_b = st.astype(BF16)
    y = xs * dskip_ref[...]
    for g in range(SSD_GROUPS):
        cg = cm[:, g * SSD_STATE:(g + 1) * SSD_STATE]
        bg = bm[:, g * SSD_STATE:(g + 1) * SSD_STATE]
        cb = _dot_nt(cg, bg)
        for hh in range(SSD_HEADS // SSD_GROUPS):
            h = g * (SSD_HEADS // SSD_GROUPS) + hh
            dec = jnp.exp(jnp.where(causal, cs[:, h:h + 1] - cs_t[h:h + 1, :], NEG))
            yd = _dot((cb * dec).astype(BF16), xdt_b)
            y = y + jnp.where(lane_head == h, yd, 0.0)
    half = _lane_iota((L, BR)) < (BR // 2)
    y_off = jnp.where(half, _dot(cm[:, 0:SSD_STATE], st_b), _dot(cm[:, SSD_STATE:2 * SSD_STATE], st_b))
    y = y + y_off * jnp.exp(cs_exp)
    cs_last = cs_exp[L - 1:L, :]
    xw = (xdt * jnp.exp(cs_last - cs_exp)).astype(BF16)
    half_s = _lane_iota((SSD_STATE, BR)) < (BR // 2)
    upd = jnp.where(half_s, _dot_tn(bm[:, 0:SSD_STATE], xw), _dot_tn(bm[:, SSD_STATE:2 * SSD_STATE], xw))
    st_new = st * jnp.exp(cs_last) + upd
    st_ref[...] = st_new
    y_ref[0] = _rms(y * _silu(z_ref[0]), ng_ref[...])

    @pl.when(c == n_chunks - 1)
    def _():
        hout_ref[0] = st_new.T.reshape(SSD_HEADS, SSD_HEAD_DIM, SSD_STATE)


def _ssd_scan(z, xbc, dt_raw, h0, prm, dskip, ng, chunk, t_valid):
    b, t, _ = z.shape
    n_chunks = t // chunk
    tri = jnp.asarray(np.tril(np.ones((chunk, chunk), np.float32)), BF16)
    expand = np.zeros((LANES, BR), np.float32)
    for h in range(SSD_HEADS):
        expand[h, h * SSD_HEAD_DIM:(h + 1) * SSD_HEAD_DIM] = 1.0
    eye = np.eye(SUBLANES, LANES, dtype=np.float32)
    row = lambda bi, ci: (bi, ci, 0)
    fixed2 = lambda bi, ci: (0, 0)
    return pl.pallas_call(
        functools.partial(_ssd_kernel, chunk=chunk, t_valid=t_valid, n_chunks=n_chunks),
        grid=(b, n_chunks),
        in_specs=[pl.BlockSpec((1, chunk, BR), row),
                  pl.BlockSpec((1, chunk, SSD_CONV_DIM), row),
                  pl.BlockSpec((1, chunk, LANES), row),
                  pl.BlockSpec((1, SSD_HEADS, SSD_HEAD_DIM, SSD_STATE), lambda bi, ci: (bi, 0, 0, 0)),
                  pl.BlockSpec((SUBLANES, LANES), fixed2),
                  pl.BlockSpec((1, BR), fixed2),
                  pl.BlockSpec((1, BR), fixed2),
                  pl.BlockSpec((chunk, chunk), fixed2),
                  pl.BlockSpec((LANES, BR), fixed2),
                  pl.BlockSpec((SUBLANES, LANES), fixed2)],
        out_specs=[pl.BlockSpec((1, chunk, BR), row),
                   pl.BlockSpec((1, SSD_HEADS, SSD_HEAD_DIM, SSD_STATE), lambda bi, ci: (bi, 0, 0, 0))],
        out_shape=[jax.ShapeDtypeStruct((b, t, BR), F32),
                   jax.ShapeDtypeStruct((b, SSD_HEADS, SSD_HEAD_DIM, SSD_STATE), F32)],
        scratch_shapes=[pltpu.VMEM((SSD_STATE, BR), F32)],
        compiler_params=_cparams(("arbitrary", "arbitrary")),
        name="ssd_scan",
    )(z, xbc, dt_raw, h0, prm, dskip, ng, tri, jnp.asarray(expand, BF16), jnp.asarray(eye, BF16))


def _pad_lanes(v, width):
    return jnp.pad(v, [(0, 0)] * (v.ndim - 1) + [(0, width - v.shape[-1])])


def _to_seq_major(a, db, steps, pad_to):
    c = a.shape[-1]
    a = a.reshape(steps, db, c).transpose(1, 0, 2)
    return jnp.pad(a, ((0, 0), (0, pad_to - steps), (0, 0)))


def _to_time_major(a, steps):
    db, _, c = a.shape
    return a[:, :steps].transpose(1, 0, 2).reshape(steps * db, c)


def _ssd_branch(z, xbc_raw, dt_raw, conv_state, h0, lp, db, steps):
    n = z.shape[0]
    tm = min(n, 512)
    xbc, conv_new = _conv_silu(xbc_raw, conv_state, lp["ssd_conv_w"], lp["ssd_conv_b"], shift=db, tm=tm)
    if db == 1:
        chunk = min(steps, 128)
        y, h_new = _ssd_scan(z[None], xbc[None], dt_raw[None], h0, lp["ssd_prm"], lp["ssd_dskip"],
                             lp["ssd_norm"], chunk, chunk)
        return y[0], conv_new, h_new
    zs, xs, ds = (_to_seq_major(a, db, steps, SUBLANES) for a in (z, xbc, dt_raw))
    y, h_new = _ssd_scan(zs, xs, ds, h0, lp["ssd_prm"], lp["ssd_dskip"], lp["ssd_norm"], SUBLANES, steps)
    return _to_time_major(y, steps), conv_new, h_new


def _gla_consts(chunk):
    L = chunk
    nl = int(math.log2(L))
    assert 2 ** nl == L
    t = np.arange(L)
    mats, masks = [], []
    for lvl in range(nl):
        b = L >> (lvl + 1)
        same = (t[:, None] // b) == (t[None, :] // b)
        mats.append((same & (t[None, :] <= t[:, None])).astype(np.float32))
    for lvl in range(nl):
        b = L >> (lvl + 1)
        same = (t[:, None] // b) == (t[None, :] // b)
        mats.append((same & (t[None, :] > t[:, None])).astype(np.float32))
    mats.append((t[None, :] <= t[:, None]).astype(np.float32))
    mats.append((t[None, :] > t[:, None]).astype(np.float32))
    for lvl in range(nl):
        b = L >> (lvl + 1)
        tb, sb = t[:, None] // b, t[None, :] // b
        masks.append(((tb % 2 == 1) & (sb == tb - 1)).astype(np.float32))
    masks.append((t[:, None] == t[None, :]).astype(np.float32))
    masks = np.stack([np.tile(m, (GLA_HEADS, 1)) for m in masks])
    ones = np.zeros((BR, BR), np.float32)
    for h in range(GLA_HEADS):
        ones[h * GLA_DV:(h + 1) * GLA_DV, h * GLA_DV:(h + 1) * GLA_DV] = 1.0
    return (jnp.asarray(np.concatenate(mats, 0), BF16), jnp.asarray(masks, F32), jnp.asarray(ones, BF16), nl)


def _gla_kernel(q_ref, k_ref, v_ref, gg_ref, r_ref, s0_ref, wg2_ref, bg_ref, ng_ref, mall_ref, masks_ref,
                ones_ref, y_ref, sout_ref, st_ref, *, chunk, t_valid, n_chunks, nl):
    L = chunk
    c = pl.program_id(1)

    @pl.when(c == 0)
    def _():
        st_ref[...] = s0_ref[0]

    q = q_ref[0] * (GLA_DK ** -0.5)
    k = k_ref[0]
    g = _log_sigmoid(_dot(gg_ref[0].astype(BF16), wg2_ref[...]) + bg_ref[...]) * (1.0 / GLA_TAU)
    if t_valid < L:
        live = _row_iota((L, LANES)) < t_valid
        g = jnp.where(live, g, 0.0)
        k = jnp.where(live, k, 0.0)
    v_b = v_ref[0].astype(BF16)
    gsum = _xdot_l(mall_ref[...], g)
    lane_head = _lane_iota((L, LANES)) // GLA_DK

    def heads_on_rows(x):
        return jnp.concatenate([jnp.where(lane_head == h, x, 0.0) for h in range(GLA_HEADS)], axis=0).astype(BF16)

    att = masks_ref[nl] * _dot_nt(heads_on_rows(q), k.astype(BF16))
    for lvl in range(nl):
        qt = q * jnp.exp(gsum[lvl * L:(lvl + 1) * L])
        kt = k * jnp.exp(gsum[(nl + lvl) * L:(nl + lvl + 1) * L])
        att = att + masks_ref[lvl] * _dot_nt(heads_on_rows(qt), kt.astype(BF16))
    bc = gsum[2 * nl * L:(2 * nl + 1) * L]
    tail = gsum[(2 * nl + 1) * L:(2 * nl + 2) * L]
    st = st_ref[...]
    res = _dot(att.astype(BF16), v_b) + _dot_nt(heads_on_rows(q * jnp.exp(bc)), st.astype(BF16))
    out_head = _lane_iota((L, BR)) // GLA_DV
    o = jnp.zeros((L, BR), F32)
    for h in range(GLA_HEADS):
        o = o + jnp.where(out_head == h, res[h * L:(h + 1) * L], 0.0)
    st_new = st * jnp.exp(bc[L - 1:L, :]) + _dot_tn(v_b, (k * jnp.exp(tail)).astype(BF16))
    st_ref[...] = st_new
    msq = _xdot_r(o * o, ones_ref[...]) * (1.0 / GLA_DV)
    y_ref[0] = o * lax.rsqrt(msq + EPS) * ng_ref[...] * _silu(r_ref[0])

    @pl.when(c == n_chunks - 1)
    def _():
        sout_ref[0] = st_new


def _gla_scan(q, k, v, gg, r, s0, wg2, bg, ng, chunk, t_valid):
    b, t, _ = q.shape
    n_chunks = t // chunk
    mall, masks, ones, nl = _gla_consts(chunk)
    row = lambda bi, ci: (bi, ci, 0)
    fixed2 = lambda bi, ci: (0, 0)
    return pl.pallas_call(
        functools.partial(_gla_kernel, chunk=chunk, t_valid=t_valid, n_chunks=n_chunks, nl=nl),
        grid=(b, n_chunks),
        in_specs=[pl.BlockSpec((1, chunk, LANES), row),
                  pl.BlockSpec((1, chunk, LANES), row),
                  pl.BlockSpec((1, chunk, BR), row),
                  pl.BlockSpec((1, chunk, LANES), row),
                  pl.BlockSpec((1, chunk, BR), row),
                  pl.BlockSpec((1, BR, LANES), lambda bi, ci: (bi, 0, 0)),
                  pl.BlockSpec((LANES, LANES), fixed2),
                  pl.BlockSpec((1, LANES), fixed2),
                  pl.BlockSpec((1, BR), fixed2),
                  pl.BlockSpec(mall.shape, fixed2),
                  pl.BlockSpec(masks.shape, lambda bi, ci: (0, 0, 0)),
                  pl.BlockSpec((BR, BR), fixed2)],
        out_specs=[pl.BlockSpec((1, chunk, BR), row),
                   pl.BlockSpec((1, BR, LANES), lambda bi, ci: (bi, 0, 0))],
        out_shape=[jax.ShapeDtypeStruct((b, t, BR), F32), jax.ShapeDtypeStruct((b, BR, LANES), F32)],
        scratch_shapes=[pltpu.VMEM((BR, LANES), F32)],
        compiler_params=_cparams(("arbitrary", "arbitrary")),
        name="gla_scan",
    )(q, k, v, gg, r, s0, wg2, bg, ng, mall, masks, ones)


def _gla_state_in(s):
    blocks = [jnp.pad(s[:, h].transpose(0, 2, 1), ((0, 0), (0, 0), (h * GLA_DK, LANES - (h + 1) * GLA_DK)))
              for h in range(GLA_HEADS)]
    return jnp.concatenate(blocks, axis=1)


def _gla_state_out(st):
    return jnp.stack([st[:, h * GLA_DV:(h + 1) * GLA_DV, h * GLA_DK:(h + 1) * GLA_DK].transpose(0, 2, 1)
                      for h in range(GLA_HEADS)], axis=1)


def _gla_branch(q, k, v, gg, r, s0, lp, db, steps):
    st0 = _gla_state_in(s0)
    args = (lp["gla_wg2"], lp["gla_bg"], lp["gla_norm"])
    if db == 1:
        chunk = min(steps, 128)
        y, st = _gla_scan(q[None], k[None], v[None], gg[None], r[None], st0, *args, chunk, chunk)
        return y[0], _gla_state_out(st)
    qs, ks, vs, gs, rs = (_to_seq_major(a, db, steps, SUBLANES) for a in (q, k, v, gg, r))
    y, st = _gla_scan(qs, ks, vs, gs, rs, st0, *args, SUBLANES, steps)
    return _to_time_major(y, steps), _gla_state_out(st)


DIFF_MAPS = DIFF_HEADS * 2
POS_SPLIT = 128


def _diff_stack_q(q, rows):
    lane = _lane_iota((rows, LANES))
    blocks = []
    for kv in range(DIFF_KV):
        for g in range(DIFF_HEADS // DIFF_KV):
            qg = q[:, g * LANES:(g + 1) * LANES]
            for i in range(2):
                lo = kv * 2 * DIFF_DIM + i * DIFF_DIM
                blocks.append(jnp.where((lane >= lo) & (lane < lo + DIFF_DIM), qg, 0.0))
    return jnp.concatenate(blocks, axis=0).astype(BF16)


LOG2E = math.log2(math.e)
DIFF_QSCALE = DIFF_DIM ** -0.5 * LOG2E


def _diff_slope(blk):
    head = blk // 2
    return 2.0 ** (-8.0 * (head + 1) / DIFF_HEADS) * LOG2E


def _bf16_parts(c):
    parts, r = [], np.float32(c)
    for _ in range(3):
        p = np.float32(np.asarray(r).astype(BF16))
        parts.append(float(p))
        r = np.float32(r - p)
    return parts


def _online_update(s, blk, rows, m_ref, l_ref, p_ref):
    r0 = blk * rows
    m_prev = m_ref[r0:r0 + rows, :]
    m_new = jnp.maximum(m_prev, jnp.max(s, axis=1, keepdims=True))
    alpha = jnp.exp2(m_prev - m_new)
    p = jnp.exp2(s - m_new[:, 0:1])
    l_ref[r0:r0 + rows, :] = alpha * l_ref[r0:r0 + rows, :] + jnp.sum(p, axis=1, keepdims=True)
    m_ref[r0:r0 + rows, :] = m_new
    p_ref[r0:r0 + rows, 0:s.shape[1]] = p.astype(p_ref.dtype)
    return alpha


def _diff_finish(acc_ref, l_ref, prm_ref, rows):
    lane = _lane_iota((rows, LANES))
    low = lane < 2 * DIFF_DIM
    lam = prm_ref[1:2, :]
    outs = []
    for g in range(DIFF_HEADS // DIFF_KV):
        parts = []
        for kv in range(DIFF_KV):
            b1 = ((kv * 2 + g) * 2) * rows
            b2 = b1 + rows
            o1 = acc_ref[b1:b1 + rows, :] / l_ref[b1:b1 + rows, :]
            o2 = acc_ref[b2:b2 + rows, :] / l_ref[b2:b2 + rows, :]
            parts.append(o1 - lam * o2)
        og = jnp.where(low, parts[0], parts[1])
        sq = og * og
        ss = jnp.where(low, jnp.sum(jnp.where(low, sq, 0.0), axis=1, keepdims=True),
                       jnp.sum(jnp.where(low, 0.0, sq), axis=1, keepdims=True))
        outs.append(og * lax.rsqrt(ss * (1.0 / (2 * DIFF_DIM)) + EPS) * prm_ref[0:1, :])
    return outs


def _diff_stack_q_alibi(q, rows):
    row = _row_iota((LANES, rows))
    blocks = []
    for kv in range(DIFF_KV):
        for g in range(DIFF_HEADS // DIFF_KV):
            qg = q[g * LANES:(g + 1) * LANES, :]
            for i in range(2):
                lo = kv * 2 * DIFF_DIM + i * DIFF_DIM
                parts = _bf16_parts(_diff_slope((kv * 2 + g) * 2 + i))
                qm = jnp.where((row >= lo) & (row < lo + DIFF_DIM), qg, 0.0)
                aug = jnp.zeros((LANES, rows), F32)
                for j, part in enumerate(parts):
                    aug = jnp.where(row == j, part * POS_SPLIT, jnp.where(row == len(parts) + j, part, aug))
                blocks.append(jnp.concatenate([qm, aug], axis=0))
    return jnp.concatenate(blocks, axis=1).astype(BF16)


FLASH_COLS = 2 * LANES
FLASH_AHEAD = 4


def _flash_tiles(k_ref, vt_ref, qs_ref, m_ref, acc_ref, q0, tq, tk):
    width = qs_ref.shape[1]

    def kv_tile(ki, masked):
        k = k_ref[pl.ds(pl.multiple_of(ki * tk, tk), tk), :]
        vt = vt_ref[ki]
        m_all = m_ref[...]
        ms, accs = [], []
        starts = list(range(0, width, FLASH_COLS))
        pending = [_dot(k, qs_ref[:, c:c + FLASH_COLS]) for c in starts[:FLASH_AHEAD]]
        for idx, c0 in enumerate(starts):
            cols = slice(c0, c0 + FLASH_COLS)
            s = pending.pop(0)
            if idx + FLASH_AHEAD < len(starts):
                c_next = starts[idx + FLASH_AHEAD]
                pending.append(_dot(k, qs_ref[:, c_next:c_next + FLASH_COLS]))
            if masked:
                kpos = ki * tk + _row_iota((tk, FLASH_COLS))
                qpos = q0 + (c0 + _lane_iota((tk, FLASH_COLS))) % tq
                s = jnp.where(kpos <= qpos, s, NEG)
            m_prev = m_all[:, cols]
            m_new = jnp.maximum(m_prev, jnp.max(s, axis=0, keepdims=True))
            p = jnp.exp2(s - m_new).astype(BF16)
            ms.append(m_new)
            accs.append(jnp.exp2(m_prev - m_new) * acc_ref[:, cols] + _dot(vt, p))
        m_ref[...] = jnp.concatenate(ms, axis=1)
        acc_ref[...] = jnp.concatenate(accs, axis=1)

    m_ref[...] = jnp.full(m_ref.shape, NEG, F32)
    acc_ref[...] = jnp.zeros(acc_ref.shape, F32)
    n_full = q0 // tk

    def body(ki, carry):
        kv_tile(ki, False)
        return carry

    lax.fori_loop(0, n_full, body, 0)
    for j in range(max(tq // tk, 1)):
        kv_tile(n_full + j, True)


def _diff_flash_kernel(q_ref, k_ref, vt_ref, lam_ref, gcol_ref, o_ref, qs_ref, m_ref, acc_ref, *, tq, tk):
    q0 = pl.program_id(0) * tq
    qs_ref[...] = _diff_stack_q_alibi(q_ref[...] * DIFF_QSCALE, tq)
    _flash_tiles(k_ref, vt_ref, qs_ref, m_ref, acc_ref, q0, tq, tk)
    row = _row_iota((LANES, tq))
    low = row < 2 * DIFF_DIM
    lam = lam_ref[...]
    for g in range(DIFF_HEADS // DIFF_KV):
        parts = []
        for kv in range(DIFF_KV):
            c1 = ((kv * 2 + g) * 2) * tq
            c2 = c1 + tq
            o1 = acc_ref[0:LANES, c1:c1 + tq] / acc_ref[LANES:LANES + 1, c1:c1 + tq]
            o2 = acc_ref[0:LANES, c2:c2 + tq] / acc_ref[LANES:LANES + 1, c2:c2 + tq]
            parts.append(o1 - lam * o2)
        og = jnp.where(low, parts[0], parts[1])
        sq = og * og
        ss = jnp.where(low, jnp.sum(jnp.where(low, sq, 0.0), axis=0, keepdims=True),
                       jnp.sum(jnp.where(low, 0.0, sq), axis=0, keepdims=True))
        o_ref[g * LANES:(g + 1) * LANES, :] = og * lax.rsqrt(ss * (1.0 / (2 * DIFF_DIM)) + EPS) * gcol_ref[...]


def _resident(shape):
    return pl.BlockSpec(shape, lambda *_: (0,) * len(shape), pipeline_mode=pl.Buffered(1))


def _value_tiles_t(v, tk):
    n, c = v.shape
    tiles = v.reshape(n // tk, tk, c).transpose(0, 2, 1)
    extra = jnp.zeros((n // tk, SUBLANES, tk), v.dtype).at[:, 0, :].set(1.0)
    return jnp.concatenate([tiles, extra], axis=1)


def _diff_flash(q_t, k_aug, v_t, lam_row, gcol, tq, tk):
    n = q_t.shape[1]
    assert (tk % tq == 0 or tq % tk == 0) and n % tk == 0 and n % tq == 0 and tq % FLASH_COLS == 0
    width = DIFF_MAPS * tq
    return pl.pallas_call(
        functools.partial(_diff_flash_kernel, tq=tq, tk=tk),
        grid=(n // tq,),
        in_specs=[pl.BlockSpec((BR, tq), lambda qi: (0, qi)),
                  _resident(k_aug.shape),
                  _resident(v_t.shape),
                  _resident((1, tq)),
                  _resident((LANES, tq))],
        out_specs=pl.BlockSpec((BR, tq), lambda qi: (0, qi)),
        out_shape=jax.ShapeDtypeStruct((BR, n), F32),
        scratch_shapes=[pltpu.VMEM((2 * LANES, width), BF16),
                        pltpu.VMEM((1, width), F32),
                        pltpu.VMEM((LANES + SUBLANES, width), F32)],
        compiler_params=_cparams(("arbitrary",)),
        name="diff_flash",
    )(q_t, k_aug, v_t, lam_row, gcol)


def _diff_keys_alibi(k, pos):
    assert k.shape[0] <= POS_SPLIT * 256
    hi, lo = pos // POS_SPLIT, pos % POS_SPLIT
    cols = jnp.stack([hi, hi, hi, lo, lo, lo], axis=1).astype(BF16)
    return jnp.concatenate([k.astype(BF16), _pad_lanes(cols, LANES)], axis=1)


def _page_copies(pt_ref, caches, bufs, sems, layer, b, c, slot, g, pages_per_step):
    page = pt_ref[b, c * pages_per_step + g]
    off = pl.multiple_of(g * PAGE, PAGE)
    copies = []
    for a, (cache, buf) in enumerate(zip(caches, bufs)):
        key_minor = buf.shape[2] == pages_per_step * PAGE
        dst = buf.at[slot, :, pl.ds(off, PAGE)] if key_minor else buf.at[slot, pl.ds(off, PAGE), :]
        copies.append(pltpu.make_async_copy(cache.at[layer, page], dst, sems.at[slot, a]))
    return copies


def _stream_pages(pt_ref, caches, bufs, sems, layer, pages_per_step, n_chunks):
    G = pages_per_step
    b, c = pl.program_id(0), pl.program_id(1)
    step = b * n_chunks + c
    last = step + 1 == pl.num_programs(0) * n_chunks
    slot = step % 2
    wrap = c + 1 == n_chunks
    nb = jnp.where(last, b, jnp.where(wrap, b + 1, b))
    nc = jnp.where(last, c, jnp.where(wrap, 0, c + 1))

    def for_pages(fn):
        def body(g, carry):
            fn(g)
            return carry
        lax.fori_loop(0, G, body, 0)

    def start(bb, cc, sl, g):
        for cp in _page_copies(pt_ref, caches, bufs, sems, layer, bb, cc, sl, g, G):
            cp.start()

    def wait(bb, cc, sl, g):
        for cp in _page_copies(pt_ref, caches, bufs, sems, layer, bb, cc, sl, g, G):
            cp.wait()

    @pl.when(step == 0)
    def _():
        for_pages(lambda g: start(b, c, slot, g))

    for g in range(G):
        wait(b, c, slot, g)

    def start_next():
        for g in range(G):
            start(nb, nc, 1 - slot, g)

    def drain():
        @pl.when(last)
        def _():
            for_pages(lambda g: wait(nb, nc, 1 - slot, g))

    return slot, start_next, drain


def _diff_decode_kernel(pt_ref, q_ref, kn_ref, vn_ref, prm_ref, kc_ref, vc_ref, o_ref,
                        kbuf, vbuf, sems, qs_ref, m_ref, l_ref, acc_ref, p_ref,
                        *, layer, pages_per_step, n_chunks, steps, past_len):
    G = pages_per_step
    R8 = SUBLANES
    c = pl.program_id(1)
    slot, start_next, drain = _stream_pages(pt_ref, (kc_ref, vc_ref), (kbuf, vbuf), sems, layer, G, n_chunks)

    @pl.when(c == 0)
    def _():
        qs_ref[...] = _diff_stack_q(q_ref[0] * DIFF_QSCALE, R8)
        m_ref[...] = jnp.full(m_ref.shape, NEG, F32)
        l_ref[...] = jnp.zeros(l_ref.shape, F32)
        acc_ref[...] = jnp.zeros(acc_ref.shape, F32)

    qs = qs_ref[...]
    width = G * PAGE
    start_next()
    s = _dot(qs, kbuf[slot].astype(BF16))
    ndist = ((c * width + _lane_iota((R8, width))) - (past_len + _row_iota((R8, width)))).astype(F32)
    for blk in range(DIFF_MAPS):
        sb = s[blk * R8:(blk + 1) * R8] + _diff_slope(blk) * ndist
        alpha = _online_update(sb, blk, R8, m_ref, l_ref, p_ref)
        acc_ref[blk * R8:(blk + 1) * R8, :] = alpha * acc_ref[blk * R8:(blk + 1) * R8, :]
    acc_ref[...] = acc_ref[...] + _dot_nt(p_ref[...].astype(BF16), vbuf[slot].astype(BF16))
    drain()

    @pl.when(c == n_chunks - 1)
    def _():
        pad = jnp.zeros((PAGE - R8, LANES), F32)
        kn = jnp.concatenate([kn_ref[0], pad], axis=0).astype(BF16)
        vn = jnp.concatenate([vn_ref[0], pad], axis=0).astype(BF16)
        sn = _dot_nt(qs, kn)
        kt = _lane_iota((R8, PAGE))
        qt = _row_iota((R8, PAGE))
        vis = (kt <= qt) & (kt < steps)
        nd = (kt - qt).astype(F32)
        for blk in range(DIFF_MAPS):
            sb = jnp.where(vis, sn[blk * R8:(blk + 1) * R8] + _diff_slope(blk) * nd, NEG)
            alpha = _online_update(sb, blk, R8, m_ref, l_ref, p_ref)
            acc_ref[blk * R8:(blk + 1) * R8, :] = alpha * acc_ref[blk * R8:(blk + 1) * R8, :]
        acc_ref[...] = acc_ref[...] + _dot(p_ref[:, 0:PAGE].astype(BF16), vn)
        outs = _diff_finish(acc_ref, l_ref, prm_ref, R8)
        o_ref[0, :, 0:LANES] = outs[0]
        o_ref[0, :, LANES:2 * LANES] = outs[1]


def _diff_decode(page_table, q, k_new, v_new, prm, kc, vc, layer, steps, pages_per_step):
    db, n_pages = page_table.shape
    G = pages_per_step
    n_chunks = n_pages // G
    rows = DIFF_MAPS * SUBLANES
    seq3 = lambda b, c, pt: (b, 0, 0)
    grid_spec = pltpu.PrefetchScalarGridSpec(
        num_scalar_prefetch=1,
        grid=(db, n_chunks),
        in_specs=[pl.BlockSpec((1, SUBLANES, BR), seq3),
                  pl.BlockSpec((1, SUBLANES, LANES), seq3),
                  pl.BlockSpec((1, SUBLANES, LANES), seq3),
                  pl.BlockSpec((SUBLANES, LANES), lambda b, c, pt: (0, 0)),
                  pl.BlockSpec(memory_space=pl.ANY),
                  pl.BlockSpec(memory_space=pl.ANY)],
        out_specs=pl.BlockSpec((1, SUBLANES, BR), seq3),
        scratch_shapes=[pltpu.VMEM((2, LANES, G * PAGE), F32),
                        pltpu.VMEM((2, LANES, G * PAGE), F32),
                        pltpu.SemaphoreType.DMA((2, 2)),
                        pltpu.VMEM((rows, LANES), BF16),
                        pltpu.VMEM((rows, LANES), F32),
                        pltpu.VMEM((rows, LANES), F32),
                        pltpu.VMEM((rows, LANES), F32),
                        pltpu.VMEM((rows, G * PAGE), F32)])
    return pl.pallas_call(
        functools.partial(_diff_decode_kernel, layer=layer, pages_per_step=G, n_chunks=n_chunks,
                          steps=steps, past_len=n_pages * PAGE),
        grid_spec=grid_spec,
        out_shape=jax.ShapeDtypeStruct((db, SUBLANES, BR), F32),
        compiler_params=_cparams(("arbitrary", "arbitrary")),
        name="diff_decode",
    )(page_table, q, k_new, v_new, prm, kc, vc)


MLA_QW = 2 * LANES
MLA_SCALE = (MLA_NOPE + MLA_ROPE) ** -0.5 * LOG2E
ROPE_HALF = MLA_ROPE // 2


def _mla_prep_kernel(cq_ref, ckv_ref, kr_ref, cos_ref, sin_ref, qn_ref, kvn_ref, wuq_ref, wcat_ref,
                     qcat_ref, kcat_ref, c_ref, krout_ref, *, q_transposed):
    tm = cq_ref.shape[0]
    lane = _lane_iota((tm, LANES))
    cos = cos_ref[...]
    sin = sin_ref[...]
    q = _dot(_rms(cq_ref[...], qn_ref[...]).astype(BF16), wuq_ref[...])
    n_nope = MLA_HEADS * MLA_NOPE
    rq = q[:, n_nope:n_nope + LANES]
    rq = rq * cos + pltpu.roll(rq, LANES // 2, 1) * jnp.where(lane < LANES // 2, -sin, sin)
    q_in = (jnp.concatenate([q[:, 0:n_nope], rq], axis=1) * MLA_SCALE).astype(BF16)
    wcat_t = wcat_ref[...]
    qcat_ref[...] = _dot_nt(wcat_t, q_in) if q_transposed else _dot_nt(q_in, wcat_t)
    c = _rms(ckv_ref[...], kvn_ref[...])
    c_ref[...] = c
    kr = kr_ref[...]
    swapped = jnp.where(lane < ROPE_HALF, pltpu.roll(kr, LANES - ROPE_HALF, 1), pltpu.roll(kr, ROPE_HALF, 1))
    kro = kr * cos + swapped * jnp.where(lane < ROPE_HALF, -sin, sin)
    kro = jnp.where(lane < MLA_ROPE, kro, 0.0)
    krout_ref[...] = kro[:, 0:MLA_ROPE]
    kcat_ref[...] = jnp.concatenate([c, kro], axis=1)


def _mla_prep(cq, ckv, kr, cos, sin, lp, tm, q_transposed):
    n = cq.shape[0]
    qw = MLA_HEADS * MLA_QW
    row = lambda i: (i, 0)
    fixed = lambda i: (0, 0)
    q_spec = pl.BlockSpec((qw, tm), lambda i: (0, i)) if q_transposed else pl.BlockSpec((tm, qw), row)
    return pl.pallas_call(
        functools.partial(_mla_prep_kernel, q_transposed=q_transposed),
        grid=(n // tm,),
        in_specs=[pl.BlockSpec((tm, MLA_QR), row), pl.BlockSpec((tm, LANES), row), pl.BlockSpec((tm, LANES), row),
                  pl.BlockSpec((tm, LANES), row), pl.BlockSpec((tm, LANES), row),
                  pl.BlockSpec((1, MLA_QR), fixed), pl.BlockSpec((1, LANES), fixed),
                  pl.BlockSpec(lp["mla_wuq"].shape, fixed), pl.BlockSpec(lp["mla_wcat_t"].shape, fixed)],
        out_specs=[q_spec, pl.BlockSpec((tm, MLA_QW), row),
                   pl.BlockSpec((tm, LANES), row), pl.BlockSpec((tm, MLA_ROPE), row)],
        out_shape=[jax.ShapeDtypeStruct((qw, n) if q_transposed else (n, qw), F32),
                   jax.ShapeDtypeStruct((n, MLA_QW), F32),
                   jax.ShapeDtypeStruct((n, LANES), F32), jax.ShapeDtypeStruct((n, MLA_ROPE), F32)],
        compiler_params=_cparams(("arbitrary",)),
        name="mla_prep",
    )(cq, ckv, kr, cos, sin, lp["mla_q_norm"], lp["mla_kv_norm"], lp["mla_wuq"], lp["mla_wcat_t"])


def _mla_stack_q(qcat):
    return jnp.concatenate([qcat[:, h * MLA_QW:(h + 1) * MLA_QW] for h in range(MLA_HEADS)], axis=0).astype(BF16)


def _mla_finish(acc_ref, l_ref, wuv_ref, rows):
    y = jnp.zeros((rows, BR), F32)
    for h in range(MLA_HEADS):
        o_lat = acc_ref[h * rows:(h + 1) * rows, :] / l_ref[h * rows:(h + 1) * rows, :]
        y = y + _dot(o_lat.astype(BF16), wuv_ref[h])
    return y


def _mla_flash_kernel(q_ref, k_ref, ct_ref, wuvt_ref, o_ref, qs_ref, m_ref, acc_ref, *, tq, tk):
    q0 = pl.program_id(0) * tq
    qs_ref[...] = jnp.concatenate([q_ref[h * MLA_QW:(h + 1) * MLA_QW, :] for h in range(MLA_HEADS)],
                                  axis=1).astype(BF16)
    _flash_tiles(k_ref, ct_ref, qs_ref, m_ref, acc_ref, q0, tq, tk)
    y = jnp.zeros((BR, tq), F32)
    for h in range(MLA_HEADS):
        cols = slice(h * tq, (h + 1) * tq)
        o_lat = acc_ref[0:MLA_KVR, cols] / acc_ref[MLA_KVR:MLA_KVR + 1, cols]
        y = y + _dot(wuvt_ref[h], o_lat.astype(BF16))
    o_ref[...] = y


def _mla_flash(qcat_t, kcat, c_t, wuv_t, tq, tk):
    n = qcat_t.shape[1]
    assert (tk % tq == 0 or tq % tk == 0) and n % tk == 0 and n % tq == 0 and tq % FLASH_COLS == 0
    width = MLA_HEADS * tq
    return pl.pallas_call(
        functools.partial(_mla_flash_kernel, tq=tq, tk=tk),
        grid=(n // tq,),
        in_specs=[pl.BlockSpec((MLA_HEADS * MLA_QW, tq), lambda qi: (0, qi)),
                  _resident(kcat.shape),
                  _resident(c_t.shape),
                  _resident(wuv_t.shape)],
        out_specs=pl.BlockSpec((BR, tq), lambda qi: (0, qi)),
        out_shape=jax.ShapeDtypeStruct((BR, n), F32),
        scratch_shapes=[pltpu.VMEM((MLA_QW, width), BF16),
                        pltpu.VMEM((1, width), F32),
                        pltpu.VMEM((MLA_KVR + SUBLANES, width), F32)],
        compiler_params=_cparams(("arbitrary",)),
        name="mla_flash",
    )(qcat_t, kcat, c_t, wuv_t)


def _mla_decode_kernel(pt_ref, q_ref, kn_ref, wuv_ref, cc_ref, rc_ref, o_ref,
                       cbuf, rbuf, sems, qs_ref, m_ref, l_ref, acc_ref, p_ref,
                       *, layer, pages_per_step, n_chunks, steps):
    G = pages_per_step
    R8 = SUBLANES
    c = pl.program_id(1)
    slot, start_next, drain = _stream_pages(pt_ref, (cc_ref, rc_ref), (cbuf, rbuf), sems, layer, G, n_chunks)

    @pl.when(c == 0)
    def _():
        qs_ref[...] = _mla_stack_q(q_ref[0])
        m_ref[...] = jnp.full(m_ref.shape, NEG, F32)
        l_ref[...] = jnp.zeros(l_ref.shape, F32)
        acc_ref[...] = jnp.zeros(acc_ref.shape, F32)

    qs = qs_ref[...]
    q_lat = qs[:, 0:MLA_KVR]
    q_rope = qs[:, MLA_KVR:MLA_KVR + MLA_ROPE]

    start_next()
    cb = cbuf[slot].astype(BF16)
    s = _dot_nt(q_lat, cb) + _dot(q_rope, rbuf[slot].astype(BF16))
    for h in range(MLA_HEADS):
        alpha = _online_update(s[h * R8:(h + 1) * R8], h, R8, m_ref, l_ref, p_ref)
        acc_ref[h * R8:(h + 1) * R8, :] = alpha * acc_ref[h * R8:(h + 1) * R8, :]
    acc_ref[...] = acc_ref[...] + _dot(p_ref[...].astype(BF16), cb)
    drain()

    @pl.when(c == n_chunks - 1)
    def _():
        kn = jnp.concatenate([kn_ref[0], jnp.zeros((PAGE - R8, MLA_QW), F32)], axis=0).astype(BF16)
        sn = _dot_nt(qs, kn)
        kt = _lane_iota((R8, PAGE))
        vis = (kt <= _row_iota((R8, PAGE))) & (kt < steps)
        for h in range(MLA_HEADS):
            alpha = _online_update(jnp.where(vis, sn[h * R8:(h + 1) * R8], NEG), h, R8, m_ref, l_ref, p_ref)
            acc_ref[h * R8:(h + 1) * R8, :] = alpha * acc_ref[h * R8:(h + 1) * R8, :]
        acc_ref[...] = acc_ref[...] + _dot(p_ref[:, 0:PAGE].astype(BF16), kn[:, 0:MLA_KVR])
        o_ref[0] = _mla_finish(acc_ref, l_ref, wuv_ref, R8)


def _mla_decode(page_table, qcat, kcat_new, wuv, cc, rc, layer, steps, pages_per_step):
    db, n_pages = page_table.shape
    G = pages_per_step
    n_chunks = n_pages // G
    rows = MLA_HEADS * SUBLANES
    seq3 = lambda b, c, pt: (b, 0, 0)
    grid_spec = pltpu.PrefetchScalarGridSpec(
        num_scalar_prefetch=1,
        grid=(db, n_chunks),
        in_specs=[pl.BlockSpec((1, SUBLANES, MLA_HEADS * MLA_QW), seq3),
                  pl.BlockSpec((1, SUBLANES, MLA_QW), seq3),
                  pl.BlockSpec(wuv.shape, lambda b, c, pt: (0, 0, 0)),
                  pl.BlockSpec(memory_space=pl.ANY),
                  pl.BlockSpec(memory_space=pl.ANY)],
        out_specs=pl.BlockSpec((1, SUBLANES, BR), seq3),
        scratch_shapes=[pltpu.VMEM((2, G * PAGE, MLA_KVR), F32),
                        pltpu.VMEM((2, MLA_ROPE, G * PAGE), F32),
                        pltpu.SemaphoreType.DMA((2, 2)),
                        pltpu.VMEM((rows, MLA_QW), BF16),
                        pltpu.VMEM((rows, LANES), F32),
                        pltpu.VMEM((rows, LANES), F32),
                        pltpu.VMEM((rows, LANES), F32),
                        pltpu.VMEM((rows, G * PAGE), F32)])
    return pl.pallas_call(
        functools.partial(_mla_decode_kernel, layer=layer, pages_per_step=G, n_chunks=n_chunks, steps=steps),
        grid_spec=grid_spec,
        out_shape=jax.ShapeDtypeStruct((db, SUBLANES, BR), F32),
        compiler_params=_cparams(("arbitrary", "arbitrary")),
        name="mla_decode",
    )(page_table, qcat, kcat_new, wuv, cc, rc)


def _mem_attn_kernel(q_ref, mk_ref, mv_ref, o_ref, *, key_minor):
    rows = q_ref.shape[1]
    lane_head = _lane_iota((rows, BR)) // MEM_DIM
    q = q_ref[0] * (MEM_DIM ** -0.5)
    qs = jnp.concatenate([jnp.where(lane_head == h, q, 0.0) for h in range(MEM_HEADS)], axis=0).astype(BF16)
    mk = mk_ref[0].astype(BF16)
    mv = mv_ref[0].astype(BF16)
    s = _dot(qs, mk) if key_minor else _dot_nt(qs, mk)
    p = jnp.exp(s - jnp.max(s, axis=1, keepdims=True))
    p = (p / jnp.sum(p, axis=1, keepdims=True)).astype(BF16)
    r = _dot_nt(p, mv) if key_minor else _dot(p, mv)
    y = jnp.zeros((rows, BR), F32)
    for h in range(MEM_HEADS):
        y = y + jnp.where(lane_head == h, r[h * rows:(h + 1) * rows], 0.0)
    o_ref[0] = y


def _mem_attn(q3, mk3, mv3, key_minor):
    b, rows, _ = q3.shape
    per_seq = mk3.shape[0] == b and b > 1
    kv_map = (lambda i: (i, 0, 0)) if per_seq else (lambda i: (0, 0, 0))
    return pl.pallas_call(
        functools.partial(_mem_attn_kernel, key_minor=key_minor),
        grid=(b,),
        in_specs=[pl.BlockSpec((1, rows, BR), lambda i: (i, 0, 0)),
                  pl.BlockSpec((1,) + mk3.shape[1:], kv_map),
                  pl.BlockSpec((1,) + mv3.shape[1:], kv_map)],
        out_specs=pl.BlockSpec((1, rows, BR), lambda i: (i, 0, 0)),
        out_shape=jax.ShapeDtypeStruct((b, rows, BR), F32),
        compiler_params=_cparams(("arbitrary",)),
        name="mem_attn",
    )(q3, mk3, mv3)


def _matmul_kernel(x_ref, w_ref, o_ref):
    o_ref[...] = _dot(x_ref[...].astype(BF16), w_ref[...])


def _matmul(x, w):
    m, k = x.shape
    n = w.shape[1]
    return pl.pallas_call(
        _matmul_kernel,
        grid=(1,),
        in_specs=[pl.BlockSpec((m, k), lambda i: (0, 0)), pl.BlockSpec((k, n), lambda i: (0, 0))],
        out_specs=pl.BlockSpec((m, n), lambda i: (0, 0)),
        out_shape=jax.ShapeDtypeStruct((m, n), F32),
        compiler_params=_cparams(("arbitrary",)),
        name="mem_proj",
    )(x, w)


def _merge_kernel(x_ref, y0, y1, y2, y3, y4, gpre_ref, gpost_ref, wg_ref, wb_ref, wo_ref, o_ref, *, transposed):
    x = x_ref[...]
    d = x.shape[1]
    hn = _rms(x, gpre_ref[...]).astype(BF16)
    m = jnp.zeros(x.shape, F32)
    for i, y_ref in enumerate((y0, y1, y2, y3, y4)):
        gate = _sigmoid(_dot_nt(hn, wg_ref[i * d:(i + 1) * d, :]))
        y = y_ref[...].astype(BF16)
        m = m + gate * (_dot_tn(y, wb_ref[i]) if i in transposed else _dot(y, wb_ref[i]))
    o_ref[...] = x + _rms(_dot(m.astype(BF16), wo_ref[...]), gpost_ref[...])


def _merge_out(x, ys, lp, tm, transposed=()):
    n, d = x.shape
    row = lambda i: (i, 0)
    fixed = lambda i: (0, 0)
    y_specs = [pl.BlockSpec((BR, tm), lambda i: (0, i)) if j in transposed else pl.BlockSpec((tm, BR), row)
               for j in range(N_BRANCH)]
    return pl.pallas_call(
        functools.partial(_merge_kernel, transposed=transposed),
        grid=(n // tm,),
        in_specs=[pl.BlockSpec((tm, d), row)] + y_specs + [
            pl.BlockSpec((1, d), fixed), pl.BlockSpec((1, d), fixed),
            _resident(lp["w_gate"].shape), _resident(lp["w_branch"].shape), _resident(lp["w_out"].shape)],
        out_specs=pl.BlockSpec((tm, d), row),
        out_shape=jax.ShapeDtypeStruct((n, d), F32),
        compiler_params=_cparams(("arbitrary",)),
        name="merge_out",
    )(x, *ys, lp["g_pre_mix"], lp["g_post_mix"], lp["w_gate"], lp["w_branch"], lp["w_out"])


FFN_CHUNK = 2 * LANES


def _ffn_kernel(x_ref, st_ref, gpre_ref, gpost_ref, wi_ref, cw_ref, cb_ref, wo_ref, o_ref, ns_ref,
                ext_ref, acc_ref, *, shift, pad, width):
    tm = x_ref.shape[0]

    @pl.when(pl.program_id(0) == 0)
    def _():
        ext_ref[0:pad, :] = st_ref[...]

    x = x_ref[...]
    hn = _rms(x, gpre_ref[...]).astype(BF16)
    for c0 in range(0, width, FFN_CHUNK):
        a = _dot(hn, wi_ref[:, c0:c0 + FFN_CHUNK])
        gate = _dot(hn, wi_ref[:, width + c0:width + c0 + FFN_CHUNK])
        ext_ref[pad:pad + tm, c0:c0 + FFN_CHUNK] = a
        conv = cb_ref[:, c0:c0 + FFN_CHUNK] + cw_ref[FFN_CONV - 1:FFN_CONV, c0:c0 + FFN_CHUNK] * a
        for j in range(FFN_CONV - 1):
            off = pad - (FFN_CONV - 1 - j) * shift
            conv = conv + cw_ref[j:j + 1, c0:c0 + FFN_CHUNK] * ext_ref[off:off + tm, c0:c0 + FFN_CHUNK]
        act = (_gelu_tanh(conv) * gate).astype(BF16)
        contrib = _dot(act, wo_ref[c0:c0 + FFN_CHUNK, :])
        if c0 == 0:
            acc_ref[...] = contrib
        else:
            acc_ref[...] = acc_ref[...] + contrib
    o_ref[...] = x + _rms(acc_ref[...], gpost_ref[...])
    tail = ext_ref[tm:tm + pad, :]
    ns_ref[...] = tail
    ext_ref[0:pad, :] = tail


def _ffn(x, state, lp, shift, tm):
    n, d = x.shape
    pad, width = state.shape
    assert tm >= pad and pad >= (FFN_CONV - 1) * shift and width % FFN_CHUNK == 0
    row = lambda i: (i, 0)
    fixed = lambda i: (0, 0)
    return pl.pallas_call(
        functools.partial(_ffn_kernel, shift=shift, pad=pad, width=width),
        grid=(n // tm,),
        in_specs=[pl.BlockSpec((tm, d), row), pl.BlockSpec((pad, width), fixed),
                  pl.BlockSpec((1, d), fixed), pl.BlockSpec((1, d), fixed),
                  _resident(lp["w_ffn_in"].shape),
                  pl.BlockSpec((FFN_CONV, width), fixed), pl.BlockSpec((1, width), fixed),
                  _resident(lp["w_ffn_out"].shape)],
        out_specs=[pl.BlockSpec((tm, d), row), pl.BlockSpec((pad, width), fixed)],
        out_shape=[jax.ShapeDtypeStruct((n, d), F32), jax.ShapeDtypeStruct((pad, width), F32)],
        scratch_shapes=[pltpu.VMEM((pad + tm, width), F32), pltpu.VMEM((tm, d), F32)],
        compiler_params=_cparams(("arbitrary",)),
        name="conv_ffn",
    )(x, state, lp["g_pre_ffn"], lp["g_post_ffn"], lp["w_ffn_in"], lp["ffn_conv_w"], lp["ffn_conv_b"],
      lp["w_ffn_out"])


def _rope_tables(pos):
    freqs = jnp.power(ROPE_THETA, -jnp.arange(ROPE_HALF, dtype=F32) / ROPE_HALF)
    ang = pos.astype(F32)[:, None] * freqs
    reps = LANES // ROPE_HALF
    return jnp.tile(jnp.cos(ang), (1, reps)), jnp.tile(jnp.sin(ang), (1, reps))


def _prep_layer(li, P):
    d_model = P["w_in"].shape[1]
    w_in_t = jnp.transpose(P["w_in"], (2, 0, 1))[:, li, :]
    bounds = np.cumsum((0,) + IN_WIDTHS)
    seg = [w_in_t[bounds[i]:bounds[i + 1]] for i in range(len(IN_WIDTHS))]
    grp = DIFF_HEADS // DIFF_KV
    seg[3] = seg[3].reshape(DIFF_KV, grp, 2 * DIFF_DIM, d_model).swapaxes(0, 1).reshape(-1, d_model)
    seg = [jnp.pad(s, ((0, wd - s.shape[0]), (0, 0))) for s, wd in zip(seg, PROJ_WIDTHS)]
    lp = {"w_cat": jnp.concatenate(seg, axis=0).astype(BF16),
          "w_gate": w_in_t[bounds[-1]:].astype(BF16)}
    for name in ("g_pre_mix", "g_post_mix", "g_pre_ffn", "g_post_ffn", "ssd_conv_b", "ssd_norm", "gla_bg",
                 "mla_q_norm", "mla_kv_norm", "ffn_conv_b"):
        lp[name] = P[name][li][None]
    lp["ssd_conv_w"] = P["ssd_conv_w"][li]
    lp["ffn_conv_w"] = P["ffn_conv_w"][li]
    prm = jnp.zeros((SUBLANES, LANES), F32)
    lp["ssd_prm"] = prm.at[0, :SSD_HEADS].set(P["ssd_dt_bias"][li]).at[1, :SSD_HEADS].set(-jnp.exp(P["ssd_a_log"][li]))
    lp["ssd_dskip"] = jnp.repeat(P["ssd_d"][li], SSD_HEAD_DIM)[None]
    lam_init = 0.8 - 0.6 * math.exp(-0.3 * li)
    lam = (jnp.exp(jnp.sum(P["diff_lq1"][li] * P["diff_lk1"][li]))
           - jnp.exp(jnp.sum(P["diff_lq2"][li] * P["diff_lk2"][li])) + lam_init)
    lp["diff_prm"] = prm.at[0].set(jnp.tile(P["diff_norm"][li], DIFF_KV) * (1.0 - lam_init)).at[1].set(lam)
    lp["gla_wg2"] = jnp.zeros((LANES, LANES), F32).at[:GLA_RANK].set(P["gla_wg2"][li]).astype(BF16)
    lp["gla_norm"] = jnp.tile(P["gla_norm"][li], GLA_HEADS)[None]
    wuq = P["mla_wuq"][li].reshape(MLA_QR, MLA_HEADS, MLA_NOPE + MLA_ROPE)
    lp["mla_wuq"] = jnp.concatenate(
        [wuq[:, :, :MLA_NOPE].reshape(MLA_QR, -1),
         wuq[:, :, MLA_NOPE:MLA_NOPE + ROPE_HALF].reshape(MLA_QR, -1),
         wuq[:, :, MLA_NOPE + ROPE_HALF:].reshape(MLA_QR, -1)], axis=1).astype(BF16)
    n_nope = MLA_HEADS * MLA_NOPE
    wcat = jnp.zeros((n_nope + LANES, MLA_HEADS * MLA_QW), F32)
    wuv = jnp.zeros((MLA_HEADS, MLA_KVR, BR), F32)
    eye = jnp.eye(ROPE_HALF, dtype=F32)
    for h in range(MLA_HEADS):
        wcat = wcat.at[h * MLA_NOPE:(h + 1) * MLA_NOPE, h * MLA_QW:h * MLA_QW + MLA_KVR].set(P["mla_wuk"][li][:, h, :].T)
        for half in range(2):
            r0 = n_nope + half * (LANES // 2) + h * ROPE_HALF
            c0 = h * MLA_QW + MLA_KVR + half * ROPE_HALF
            wcat = wcat.at[r0:r0 + ROPE_HALF, c0:c0 + ROPE_HALF].set(eye)
        wuv = wuv.at[h, :, h * MLA_V:(h + 1) * MLA_V].set(P["mla_wuv"][li][:, h, :])
    lp["mla_wcat_t"] = wcat.T.astype(BF16)
    lp["mla_wuv"] = wuv.astype(BF16)
    lp["mla_wuv_t"] = wuv.transpose(0, 2, 1).astype(BF16)
    lp["w_mem_kv"] = jnp.concatenate([P["w_mem_k"][li], P["w_mem_v"][li]], axis=1).astype(BF16)
    wb = P["w_branch"][li]
    wb_diff = wb[1].reshape(DIFF_KV, grp, 2 * DIFF_DIM, d_model).swapaxes(0, 1).reshape(BR, d_model)
    lp["w_branch"] = wb.at[1].set(wb_diff).astype(BF16)
    lp["w_out"] = P["w_out"][li].astype(BF16)
    lp["w_ffn_in"] = P["w_ffn_in"][li].astype(BF16)
    lp["w_ffn_out"] = P["w_ffn_out"][li].astype(BF16)
    return lp


def _row_tile(n, want):
    return want if n % want == 0 else n


def _layer(x, lp, li, db, steps, st, past):
    n = x.shape[0]
    tm = _row_tile(n, 256)
    prompt = past is None
    (z, xbc, dt, dq, dk, dv, gq, gk, gv, gg, gr, cq, ckv, kr, mq) = _proj_in(
        x, lp["g_pre_mix"], lp["w_cat"], _row_tile(n, 512), transposed=(3,) if prompt else ())
    y_ssd, ssd_conv, ssd_h = _ssd_branch(z, xbc, dt, st["ssd_conv"], st["ssd_h"], lp, db, steps)
    y_gla, gla_s = _gla_branch(gq, gk, gv, gg, gr, st["gla_s"], lp, db, steps)
    past_len = 0 if prompt else past["page_table"].shape[1] * PAGE
    pos = jnp.repeat(past_len + jnp.arange(steps, dtype=jnp.int32), db)
    cos, sin = _rope_tables(pos)
    qcat, kcat, c_rows, kr_rows = _mla_prep(cq, ckv, kr, cos, sin, lp, tm, q_transposed=prompt)
    if prompt:
        tk = _row_tile(n, 512)
        tq_diff, tq_mla = _row_tile(n, 512), _row_tile(n, 1024)
        gcol = jnp.broadcast_to(lp["diff_prm"][0][:, None], (LANES, tq_diff))
        lam_row = jnp.broadcast_to(lp["diff_prm"][1, 0], (1, tq_diff))
        y_diff = _diff_flash(dq, _diff_keys_alibi(dk, pos), _value_tiles_t(dv.astype(BF16), tk),
                             lam_row, gcol, tq_diff, tk)
        kb = kcat.astype(BF16)
        y_mla = _mla_flash(qcat, kb, _value_tiles_t(kb[:, :MLA_KVR], tk), lp["mla_wuv_t"], tq_mla, tk)
        y_mem = _mem_attn(mq.reshape(n // tm, tm, BR), st["mem_k"], st["mem_v"], key_minor=False).reshape(n, BR)
    else:
        seq = lambda a: _to_seq_major(a, db, steps, SUBLANES)
        pt = past["page_table"]
        g_pages = math.gcd(pt.shape[1], 64)
        y_diff = _to_time_major(_diff_decode(pt, seq(dq), seq(dk), seq(dv), lp["diff_prm"], past["diff_k"],
                                             past["diff_v"], li, steps, g_pages), steps)
        y_mla = _to_time_major(_mla_decode(pt, seq(qcat), seq(kcat), lp["mla_wuv"], past["mla_ckv"],
                                           past["mla_kr"], li, steps, g_pages), steps)
        y_mem = _to_time_major(_mem_attn(seq(mq), st["mem_k"], st["mem_v"], key_minor=True), steps)
    x = _merge_out(x, (y_ssd, y_diff, y_gla, y_mla, y_mem), lp, tm, transposed=(1, 3) if prompt else ())
    x, ffn_conv = _ffn(x, st["ffn_conv"], lp, shift=db, tm=_row_tile(n, 512) if db == 1 else n)
    return x, (dk, dv, c_rows, kr_rows, ssd_h, ssd_conv, gla_s, ffn_conv)


def kernel(x_prompt, x_sample, cache_diff_k, cache_diff_v, cache_mla_ckv, cache_mla_krope, cache_mem_k, cache_mem_v, state_ssd, state_ssd_conv, state_gla, state_ffn_conv, page_table, mem_prompt, w_in, ssd_conv_w, ssd_conv_b, ssd_dt_bias, ssd_a_log, ssd_d, ssd_norm, diff_lq1, diff_lk1, diff_lq2, diff_lk2, diff_norm, gla_wg2, gla_bg, gla_norm, mla_q_norm, mla_kv_norm, mla_wuq, mla_wuk, mla_wuv, w_mem_k, w_mem_v, w_branch, w_out, g_pre_mix, g_post_mix, g_pre_ffn, g_post_ffn, w_ffn_in, ffn_conv_w, ffn_conv_b, w_ffn_out):
    P = dict(w_in=w_in, ssd_conv_w=ssd_conv_w, ssd_conv_b=ssd_conv_b, ssd_dt_bias=ssd_dt_bias,
             ssd_a_log=ssd_a_log, ssd_d=ssd_d, ssd_norm=ssd_norm, diff_lq1=diff_lq1, diff_lk1=diff_lk1,
             diff_lq2=diff_lq2, diff_lk2=diff_lk2, diff_norm=diff_norm, gla_wg2=gla_wg2, gla_bg=gla_bg,
             gla_norm=gla_norm, mla_q_norm=mla_q_norm, mla_kv_norm=mla_kv_norm, mla_wuq=mla_wuq,
             mla_wuk=mla_wuk, mla_wuv=mla_wuv, w_mem_k=w_mem_k, w_mem_v=w_mem_v, w_branch=w_branch,
             w_out=w_out, g_pre_mix=g_pre_mix, g_post_mix=g_post_mix, g_pre_ffn=g_pre_ffn,
             g_post_ffn=g_post_ffn, w_ffn_in=w_ffn_in, ffn_conv_w=ffn_conv_w, ffn_conv_b=ffn_conv_b,
             w_ffn_out=w_ffn_out)
    depth = w_in.shape[0]
    pb, seq_len, d_model = x_prompt.shape
    assert pb == 1
    db, steps, _ = x_sample.shape
    mem_len = mem_prompt.shape[1]
    ffn_dim = ffn_conv_w.shape[2]
    past = {"page_table": page_table,
            "diff_k": jnp.transpose(cache_diff_k, (0, 1, 3, 4, 2)).reshape(depth, -1, LANES, PAGE),
            "diff_v": jnp.transpose(cache_diff_v, (0, 1, 3, 4, 2)).reshape(depth, -1, LANES, PAGE),
            "mla_ckv": cache_mla_ckv,
            "mla_kr": jnp.transpose(cache_mla_krope, (0, 1, 3, 2))}
    mem_kt = jnp.transpose(cache_mem_k, (0, 1, 3, 4, 2)).reshape(depth, db, BR, mem_len)
    mem_vt = jnp.transpose(cache_mem_v, (0, 1, 3, 4, 2)).reshape(depth, db, BR, mem_len)
    xp = x_prompt[0]
    xs = x_sample.transpose(1, 0, 2).reshape(steps * db, d_model)
    new_p, new_s, mem_kp, mem_vp = [], [], [], []
    for li in range(depth):
        lp = _prep_layer(li, P)
        mkv = _matmul(mem_prompt[0], lp["w_mem_kv"])
        mk, mv = mkv[:, :BR], mkv[:, BR:]
        mem_kp.append(mk.reshape(1, mem_len, MEM_HEADS, MEM_DIM))
        mem_vp.append(mv.reshape(1, mem_len, MEM_HEADS, MEM_DIM))
        st_p = {"ssd_conv": jnp.zeros((SUBLANES, SSD_CONV_DIM), F32),
                "ssd_h": jnp.zeros((1, SSD_HEADS, SSD_HEAD_DIM, SSD_STATE), F32),
                "gla_s": jnp.zeros((1, GLA_HEADS, GLA_DK, GLA_DV), F32),
                "ffn_conv": jnp.zeros((SUBLANES, ffn_dim), F32),
                "mem_k": mk[None], "mem_v": mv[None]}
        xp, o = _layer(xp, lp, li, 1, seq_len, st_p, None)
        dk, dv, c_rows, kr_rows, ssd_h, ssd_conv, gla_s, ffn_conv = o
        new_p.append((dk.reshape(1, seq_len, DIFF_KV, 2 * DIFF_DIM), dv.reshape(1, seq_len, DIFF_KV, 2 * DIFF_DIM),
                      c_rows[None], kr_rows[None], ssd_h, ssd_conv[None, SUBLANES - (SSD_CONV - 1):],
                      gla_s, ffn_conv[None, SUBLANES - (FFN_CONV - 1):]))
        st_s = {"ssd_conv": state_ssd_conv[li].transpose(1, 0, 2).reshape((SSD_CONV - 1) * db, SSD_CONV_DIM),
                "ssd_h": state_ssd[li],
                "gla_s": state_gla[li],
                "ffn_conv": state_ffn_conv[li].transpose(1, 0, 2).reshape((FFN_CONV - 1) * db, ffn_dim),
                "mem_k": mem_kt[li], "mem_v": mem_vt[li]}
        xs, o = _layer(xs, lp, li, db, steps, st_s, past)
        dk, dv, c_rows, kr_rows, ssd_h, ssd_conv, gla_s, ffn_conv = o
        bm = lambda a: a.reshape(-1, db, a.shape[-1]).transpose(1, 0, 2)
        new_s.append((bm(dk).reshape(db, steps, DIFF_KV, 2 * DIFF_DIM), bm(dv).reshape(db, steps, DIFF_KV, 2 * DIFF_DIM),
                      bm(c_rows), bm(kr_rows), ssd_h, bm(ssd_conv), gla_s, bm(ffn_conv)))
    stk = lambda outs, j: jnp.stack([o[j] for o in outs])
    yp = xp[None]
    ys = xs.reshape(steps, db, d_model).transpose(1, 0, 2)
    return (yp, ys,
            stk(new_p, 0), stk(new_p, 1), stk(new_p, 2), stk(new_p, 3),
            jnp.stack(mem_kp), jnp.stack(mem_vp),
            stk(new_p, 4), stk(new_p, 5), stk(new_p, 6), stk(new_p, 7),
            stk(new_s, 0), stk(new_s, 1), stk(new_s, 2), stk(new_s, 3),
            stk(new_s, 4), stk(new_s, 5), stk(new_s, 6), stk(new_s, 7))
```

```python
import functools
import math

import numpy as np
import jax
import jax.numpy as jnp
from jax import lax
from jax.experimental import pallas as pl
from jax.experimental.pallas import tpu as pltpu

F32 = jnp.float32
BF16 = jnp.bfloat16

BR = 256
N_BRANCH = 5
SSD_HEADS, SSD_HEAD_DIM, SSD_GROUPS, SSD_STATE, SSD_CONV = 4, 64, 2, 128, 4
SSD_CONV_DIM = BR + 2 * SSD_GROUPS * SSD_STATE
DIFF_HEADS, DIFF_KV, DIFF_DIM = 4, 2, 32
GLA_HEADS, GLA_DK, GLA_DV, GLA_RANK, GLA_TAU = 4, 32, 64, 16, 16.0
MLA_HEADS, MLA_QR, MLA_KVR, MLA_NOPE, MLA_ROPE, MLA_V = 4, 256, 128, 64, 32, 64
ROPE_THETA = 10000.0
MEM_HEADS, MEM_DIM = 4, 64
FFN_CONV = 3
PAGE = 128
EPS = 1e-6
NEG = -1e30

IN_WIDTHS = (BR, SSD_CONV_DIM, SSD_HEADS, DIFF_HEADS * 2 * DIFF_DIM, DIFF_KV * 2 * DIFF_DIM,
             DIFF_KV * 2 * DIFF_DIM, GLA_HEADS * GLA_DK, GLA_HEADS * GLA_DK, GLA_HEADS * GLA_DV,
             GLA_RANK, GLA_HEADS * GLA_DV, MLA_QR, MLA_KVR, MLA_ROPE, MEM_HEADS * MEM_DIM)
PROJ_WIDTHS = (256, 768, 128, 256, 128, 128, 128, 128, 256, 128, 256, 256, 128, 128, 256)

LANES = 128
SUBLANES = 8
VMEM_LIMIT = 56 * 1024 * 1024


def _cparams(sem):
    return pltpu.CompilerParams(dimension_semantics=sem, vmem_limit_bytes=VMEM_LIMIT)


def _dot(a, b):
    return jnp.dot(a, b, preferred_element_type=F32)


def _dot_nt(a, b):
    return lax.dot_general(a, b, (((1,), (1,)), ((), ())), preferred_element_type=F32)


def _dot_tn(a, b):
    return lax.dot_general(a, b, (((0,), (0,)), ((), ())), preferred_element_type=F32)


def _split3(x):
    hi = x.astype(BF16)
    r = x - hi.astype(F32)
    mid = r.astype(BF16)
    lo = (r - mid.astype(F32)).astype(BF16)
    return hi, mid, lo


def _xdot_l(m01, x):
    hi, mid, lo = _split3(x)
    return _dot(m01, hi) + _dot(m01, mid) + _dot(m01, lo)


def _xdot_r(x, m01):
    hi, mid, lo = _split3(x)
    return _dot(hi, m01) + _dot(mid, m01) + _dot(lo, m01)


def _xdot_nt(m01, x):
    hi, mid, lo = _split3(x)
    return _dot_nt(m01, hi) + _dot_nt(m01, mid) + _dot_nt(m01, lo)


def _rms(x, g):
    return x * lax.rsqrt(jnp.mean(x * x, axis=-1, keepdims=True) + EPS) * g


def _silu(x):
    return x * (1.0 / (1.0 + jnp.exp(-x)))


def _sigmoid(x):
    return 1.0 / (1.0 + jnp.exp(-x))


def _softplus(x):
    return jnp.maximum(x, 0.0) + jnp.log(1.0 + jnp.exp(-jnp.abs(x)))


def _log_sigmoid(x):
    return -_softplus(-x)


def _gelu_tanh(x):
    c = math.sqrt(2.0 / math.pi)
    return 0.5 * x * (1.0 + jnp.tanh(c * (x + 0.044715 * (x * x * x))))


def _lane_iota(shape):
    return lax.broadcasted_iota(jnp.int32, shape, len(shape) - 1)


def _row_iota(shape):
    return lax.broadcasted_iota(jnp.int32, shape, len(shape) - 2)


def _proj_in_kernel(x_ref, g_ref, w_ref, *o_refs, transposed):
    xn = _rms(x_ref[...], g_ref[...]).astype(BF16)
    off = 0
    for i, (o_ref, wd) in enumerate(zip(o_refs, PROJ_WIDTHS)):
        w = w_ref[off:off + wd, :]
        o_ref[...] = _dot_nt(w, xn) if i in transposed else _dot_nt(xn, w)
        off += wd


def _proj_in(x, g, w, tm, transposed=()):
    n, d = x.shape
    spec = lambda i, wd: (pl.BlockSpec((wd, tm), lambda r: (0, r)) if i in transposed
                          else pl.BlockSpec((tm, wd), lambda r: (r, 0)))
    shape = lambda i, wd: jax.ShapeDtypeStruct((wd, n) if i in transposed else (n, wd), F32)
    return pl.pallas_call(
        functools.partial(_proj_in_kernel, transposed=transposed),
        grid=(n // tm,),
        in_specs=[pl.BlockSpec((tm, d), lambda i: (i, 0)),
                  pl.BlockSpec((1, d), lambda i: (0, 0)),
                  _resident(w.shape)],
        out_specs=[spec(i, wd) for i, wd in enumerate(PROJ_WIDTHS)],
        out_shape=[shape(i, wd) for i, wd in enumerate(PROJ_WIDTHS)],
        compiler_params=_cparams(("arbitrary",)),
        name="proj_in",
    )(x, g, w)


def _conv_taps(ext_ref, cur, w_ref, b_ref, taps, shift, pad, tm):
    acc = b_ref[...] + w_ref[taps - 1:taps, :] * cur
    for j in range(taps - 1):
        off = pad - (taps - 1 - j) * shift
        acc = acc + w_ref[j:j + 1, :] * ext_ref[off:off + tm, :]
    return acc


def _conv_silu_kernel(x_ref, st_ref, w_ref, b_ref, y_ref, ns_ref, ext_ref, *, taps, shift, pad):
    tm = x_ref.shape[0]

    @pl.when(pl.program_id(0) == 0)
    def _():
        ext_ref[0:pad, :] = st_ref[...]

    cur = x_ref[...]
    ext_ref[pad:pad + tm, :] = cur
    y_ref[...] = _silu(_conv_taps(ext_ref, cur, w_ref, b_ref, taps, shift, pad, tm))
    tail = ext_ref[tm:tm + pad, :]
    ns_ref[...] = tail
    ext_ref[0:pad, :] = tail


def _conv_silu(x, state, w, b, shift, tm):
    n, c = x.shape
    taps = w.shape[0]
    pad = state.shape[0]
    assert tm >= pad and pad >= (taps - 1) * shift
    return pl.pallas_call(
        functools.partial(_conv_silu_kernel, taps=taps, shift=shift, pad=pad),
        grid=(n // tm,),
        in_specs=[pl.BlockSpec((tm, c), lambda i: (i, 0)),
                  pl.BlockSpec((pad, c), lambda i: (0, 0)),
                  pl.BlockSpec((taps, c), lambda i: (0, 0)),
                  pl.BlockSpec((1, c), lambda i: (0, 0))],
        out_specs=[pl.BlockSpec((tm, c), lambda i: (i, 0)),
                   pl.BlockSpec((pad, c), lambda i: (0, 0))],
        out_shape=[jax.ShapeDtypeStruct((n, c), F32), jax.ShapeDtypeStruct((pad, c), F32)],
        scratch_shapes=[pltpu.VMEM((pad + tm, c), F32)],
        compiler_params=_cparams(("arbitrary",)),
        name="conv_silu",
    )(x, state, w, b)


def _ssd_kernel(z_ref, xbc_ref, dt_ref, h0_ref, prm_ref, dskip_ref, ng_ref, tri_ref, exp_ref, eye_ref,
                y_ref, hout_ref, st_ref, *, chunk, t_valid, n_chunks):
    for j in range(z_ref.shape[0]):
        _ssd_chunk(j, z_ref, xbc_ref, dt_ref, h0_ref, prm_ref, dskip_ref, ng_ref, tri_ref, exp_ref, eye_ref,
                   y_ref, hout_ref, st_ref, chunk, t_valid, n_chunks)


def _when(pred, always):
    return (lambda fn: fn()) if always else pl.when(pred)


def _ssd_chunk(j, z_ref, xbc_ref, dt_ref, h0_ref, prm_ref, dskip_ref, ng_ref, tri_ref, exp_ref, eye_ref,
               y_ref, hout_ref, st_ref, chunk, t_valid, n_chunks):
    L = chunk
    c = pl.program_id(1)

    @_when(c == 0, n_chunks == 1)
    def _():
        st_ref[j] = h0_ref[j].reshape(SSD_HEADS * SSD_HEAD_DIM, SSD_STATE).T

    xbc = xbc_ref[j]
    xs = xbc[:, 0:BR]
    bm = xbc[:, BR:2 * BR].astype(BF16)
    cm = xbc[:, 2 * BR:3 * BR].astype(BF16)
    dt = _softplus(dt_ref[j] + prm_ref[0:1, :])
    if t_valid < L:
        dt = jnp.where(_row_iota((L, LANES)) < t_valid, dt, 0.0)
    d_a = dt * prm_ref[1:2, :]
    tri = tri_ref[...]
    expand = exp_ref[...]
    cs = _xdot_l(tri, d_a)
    cs_exp = _xdot_r(cs, expand)
    dt_exp = _xdot_r(dt, expand)
    cs_t = _xdot_nt(eye_ref[...], cs)
    xdt = xs * dt_exp
    xdt_b = xdt.astype(BF16)
    causal = _row_iota((L, L)) >= _lane_iota((L, L))
    lane_head = _lane_iota((L, BR)) // SSD_HEAD_DIM
    st = st_ref[j]
    st_b = st.astype(BF16)
    y = xs * dskip_ref[...]
    for g in range(SSD_GROUPS):
        cg = cm[:, g * SSD_STATE:(g + 1) * SSD_STATE]
        bg = bm[:, g * SSD_STATE:(g + 1) * SSD_STATE]
        cb = _dot_nt(cg, bg)
        for hh in range(SSD_HEADS // SSD_GROUPS):
            h = g * (SSD_HEADS // SSD_GROUPS) + hh
            dec = jnp.exp(jnp.where(causal, cs[:, h:h + 1] - cs_t[h:h + 1, :], NEG))
            yd = _dot((cb * dec).astype(BF16), xdt_b)
            y = y + jnp.where(lane_head == h, yd, 0.0)
    half = _lane_iota((L, BR)) < (BR // 2)
    y_off = jnp.where(half, _dot(cm[:, 0:SSD_STATE], st_b), _dot(cm[:, SSD_STATE:2 * SSD_STATE], st_b))
    y = y + y_off * jnp.exp(cs_exp)
    cs_last = cs_exp[L - 1:L, :]
    xw = (xdt * jnp.exp(cs_last - cs_exp)).astype(BF16)
    half_s = _lane_iota((SSD_STATE, BR)) < (BR // 2)
    upd = jnp.where(half_s, _dot_tn(bm[:, 0:SSD_STATE], xw), _dot_tn(bm[:, SSD_STATE:2 * SSD_STATE], xw))
    st_new = st * jnp.exp(cs_last) + upd
    st_ref[j] = st_new
    y_ref[j] = _rms(y * _silu(z_ref[j]), ng_ref[...])

    @_when(c == n_chunks - 1, n_chunks == 1)
    def _():
        hout_ref[j] = st_new.T.reshape(SSD_HEADS, SSD_HEAD_DIM, SSD_STATE)


SCAN_SEQS = 8


def _ssd_scan(z, xbc, dt_raw, h0, prm, dskip, ng, chunk, t_valid):
    b, t, _ = z.shape
    n_chunks = t // chunk
    sb = math.gcd(b, SCAN_SEQS)
    tri = jnp.asarray(np.tril(np.ones((chunk, chunk), np.float32)), BF16)
    expand = np.zeros((LANES, BR), np.float32)
    for h in range(SSD_HEADS):
        expand[h, h * SSD_HEAD_DIM:(h + 1) * SSD_HEAD_DIM] = 1.0
    eye = np.eye(SUBLANES, LANES, dtype=np.float32)
    row = lambda bi, ci: (bi, ci, 0)
    fixed2 = lambda bi, ci: (0, 0)
    return pl.pallas_call(
        functools.partial(_ssd_kernel, chunk=chunk, t_valid=t_valid, n_chunks=n_chunks),
        grid=(b // sb, n_chunks),
        in_specs=[pl.BlockSpec((sb, chunk, BR), row),
                  pl.BlockSpec((sb, chunk, SSD_CONV_DIM), row),
                  pl.BlockSpec((sb, chunk, LANES), row),
                  pl.BlockSpec((sb, SSD_HEADS, SSD_HEAD_DIM, SSD_STATE), lambda bi, ci: (bi, 0, 0, 0)),
                  pl.BlockSpec((SUBLANES, LANES), fixed2),
                  pl.BlockSpec((1, BR), fixed2),
                  pl.BlockSpec((1, BR), fixed2),
                  pl.BlockSpec((chunk, chunk), fixed2),
                  pl.BlockSpec((LANES, BR), fixed2),
                  pl.BlockSpec((SUBLANES, LANES), fixed2)],
        out_specs=[pl.BlockSpec((sb, chunk, BR), row),
                   pl.BlockSpec((sb, SSD_HEADS, SSD_HEAD_DIM, SSD_STATE), lambda bi, ci: (bi, 0, 0, 0))],
        out_shape=[jax.ShapeDtypeStruct((b, t, BR), F32),
                   jax.ShapeDtypeStruct((b, SSD_HEADS, SSD_HEAD_DIM, SSD_STATE), F32)],
        scratch_shapes=[pltpu.VMEM((sb, SSD_STATE, BR), F32)],
        compiler_params=_cparams(("arbitrary", "arbitrary")),
        name="ssd_scan",
    )(z, xbc, dt_raw, h0, prm, dskip, ng, tri, jnp.asarray(expand, BF16), jnp.asarray(eye, BF16))


def _pad_lanes(v, width):
    return jnp.pad(v, [(0, 0)] * (v.ndim - 1) + [(0, width - v.shape[-1])])


def _to_seq_major(a, db, steps, pad_to):
    c = a.shape[-1]
    a = a.reshape(steps, db, c).transpose(1, 0, 2)
    return jnp.pad(a, ((0, 0), (0, pad_to - steps), (0, 0)))


def _to_time_major(a, steps):
    db, _, c = a.shape
    return a[:, :steps].transpose(1, 0, 2).reshape(steps * db, c)


def _ssd_branch(z, xbc_raw, dt_raw, conv_state, h0, lp, db, steps):
    n = z.shape[0]
    tm = min(n, 512)
    xbc, conv_new = _conv_silu(xbc_raw, conv_state, lp["ssd_conv_w"], lp["ssd_conv_b"], shift=db, tm=tm)
    if db == 1:
        chunk = min(steps, 128)
        y, h_new = _ssd_scan(z[None], xbc[None], dt_raw[None], h0, lp["ssd_prm"], lp["ssd_dskip"],
                             lp["ssd_norm"], chunk, chunk)
        return y[0], conv_new, h_new
    zs, xs, ds = (_to_seq_major(a, db, steps, SUBLANES) for a in (z, xbc, dt_raw))
    y, h_new = _ssd_scan(zs, xs, ds, h0, lp["ssd_prm"], lp["ssd_dskip"], lp["ssd_norm"], SUBLANES, steps)
    return _to_time_major(y, steps), conv_new, h_new


def _gla_consts(chunk):
    L = chunk
    nl = int(math.log2(L))
    assert 2 ** nl == L
    t = np.arange(L)
    mats, masks = [], []
    for lvl in range(nl):
        b = L >> (lvl + 1)
        same = (t[:, None] // b) == (t[None, :] // b)
        mats.append((same & (t[None, :] <= t[:, None])).astype(np.float32))
    for lvl in range(nl):
        b = L >> (lvl + 1)
        same = (t[:, None] // b) == (t[None, :] // b)
        mats.append((same & (t[None, :] > t[:, None])).astype(np.float32))
    mats.append((t[None, :] <= t[:, None]).astype(np.float32))
    mats.append((t[None, :] > t[:, None]).astype(np.float32))
    for lvl in range(nl):
        b = L >> (lvl + 1)
        tb, sb = t[:, None] // b, t[None, :] // b
        masks.append(((tb % 2 == 1) & (sb == tb - 1)).astype(np.float32))
    masks.append((t[:, None] == t[None, :]).astype(np.float32))
    masks = np.stack([np.tile(m, (GLA_HEADS, 1)) for m in masks])
    ones = np.zeros((BR, BR), np.float32)
    for h in range(GLA_HEADS):
        ones[h * GLA_DV:(h + 1) * GLA_DV, h * GLA_DV:(h + 1) * GLA_DV] = 1.0
    return (jnp.asarray(np.concatenate(mats, 0), BF16), jnp.asarray(masks, F32), jnp.asarray(ones, BF16), nl)


def _gla_kernel(q_ref, k_ref, v_ref, gg_ref, r_ref, s0_ref, wg2_ref, bg_ref, ng_ref, mall_ref, masks_ref,
                ones_ref, y_ref, sout_ref, st_ref, *, chunk, t_valid, n_chunks, nl):
    for j in range(q_ref.shape[0]):
        _gla_chunk(j, q_ref, k_ref, v_ref, gg_ref, r_ref, s0_ref, wg2_ref, bg_ref, ng_ref, mall_ref, masks_ref,
                   ones_ref, y_ref, sout_ref, st_ref, chunk, t_valid, n_chunks, nl)


def _gla_chunk(j, q_ref, k_ref, v_ref, gg_ref, r_ref, s0_ref, wg2_ref, bg_ref, ng_ref, mall_ref, masks_ref,
               ones_ref, y_ref, sout_ref, st_ref, chunk, t_valid, n_chunks, nl):
    L = chunk
    c = pl.program_id(1)

    @_when(c == 0, n_chunks == 1)
    def _():
        st_ref[j] = s0_ref[j]

    q = q_ref[j] * (GLA_DK ** -0.5)
    k = k_ref[j]
    g = _log_sigmoid(_dot(gg_ref[j].astype(BF16), wg2_ref[...]) + bg_ref[...]) * (1.0 / GLA_TAU)
    if t_valid < L:
        live = _row_iota((L, LANES)) < t_valid
        g = jnp.where(live, g, 0.0)
        k = jnp.where(live, k, 0.0)
    v_b = v_ref[j].astype(BF16)
    gsum = _xdot_l(mall_ref[...], g)
    lane_head = _lane_iota((L, LANES)) // GLA_DK

    def heads_on_rows(x):
        return jnp.concatenate([jnp.where(lane_head == h, x, 0.0) for h in range(GLA_HEADS)], axis=0).astype(BF16)

    att = masks_ref[nl] * _dot_nt(heads_on_rows(q), k.astype(BF16))
    for lvl in range(nl):
        qt = q * jnp.exp(gsum[lvl * L:(lvl + 1) * L])
        kt = k * jnp.exp(gsum[(nl + lvl) * L:(nl + lvl + 1) * L])
        att = att + masks_ref[lvl] * _dot_nt(heads_on_rows(qt), kt.astype(BF16))
    bc = gsum[2 * nl * L:(2 * nl + 1) * L]
    tail = gsum[(2 * nl + 1) * L:(2 * nl + 2) * L]
    st = st_ref[j]
    res = _dot(att.astype(BF16), v_b) + _dot_nt(heads_on_rows(q * jnp.exp(bc)), st.astype(BF16))
    out_head = _lane_iota((L, BR)) // GLA_DV
    o = jnp.zeros((L, BR), F32)
    for h in range(GLA_HEADS):
        o = o + jnp.where(out_head == h, res[h * L:(h + 1) * L], 0.0)
    st_new = st * jnp.exp(bc[L - 1:L, :]) + _dot_tn(v_b, (k * jnp.exp(tail)).astype(BF16))
    st_ref[j] = st_new
    msq = _xdot_r(o * o, ones_ref[...]) * (1.0 / GLA_DV)
    y_ref[j] = o * lax.rsqrt(msq + EPS) * ng_ref[...] * _silu(r_ref[j])

    @_when(c == n_chunks - 1, n_chunks == 1)
    def _():
        sout_ref[j] = st_new


def _gla_scan(q, k, v, gg, r, s0, wg2, bg, ng, chunk, t_valid):
    b, t, _ = q.shape
    n_chunks = t // chunk
    mall, masks, ones, nl = _gla_consts(chunk)
    sb = math.gcd(b, SCAN_SEQS)
    row = lambda bi, ci: (bi, ci, 0)
    fixed2 = lambda bi, ci: (0, 0)
    return pl.pallas_call(
        functools.partial(_gla_kernel, chunk=chunk, t_valid=t_valid, n_chunks=n_chunks, nl=nl),
        grid=(b // sb, n_chunks),
        in_specs=[pl.BlockSpec((sb, chunk, LANES), row),
                  pl.BlockSpec((sb, chunk, LANES), row),
                  pl.BlockSpec((sb, chunk, BR), row),
                  pl.BlockSpec((sb, chunk, LANES), row),
                  pl.BlockSpec((sb, chunk, BR), row),
                  pl.BlockSpec((sb, BR, LANES), lambda bi, ci: (bi, 0, 0)),
                  pl.BlockSpec((LANES, LANES), fixed2),
                  pl.BlockSpec((1, LANES), fixed2),
                  pl.BlockSpec((1, BR), fixed2),
                  pl.BlockSpec(mall.shape, fixed2),
                  pl.BlockSpec(masks.shape, lambda bi, ci: (0, 0, 0)),
                  pl.BlockSpec((BR, BR), fixed2)],
        out_specs=[pl.BlockSpec((sb, chunk, BR), row),
                   pl.BlockSpec((sb, BR, LANES), lambda bi, ci: (bi, 0, 0))],
        out_shape=[jax.ShapeDtypeStruct((b, t, BR), F32), jax.ShapeDtypeStruct((b, BR, LANES), F32)],
        scratch_shapes=[pltpu.VMEM((sb, BR, LANES), F32)],
        compiler_params=_cparams(("arbitrary", "arbitrary")),
        name="gla_scan",
    )(q, k, v, gg, r, s0, wg2, bg, ng, mall, masks, ones)


def _gla_state_in(s):
    blocks = [jnp.pad(s[:, h].transpose(0, 2, 1), ((0, 0), (0, 0), (h * GLA_DK, LANES - (h + 1) * GLA_DK)))
              for h in range(GLA_HEADS)]
    return jnp.concatenate(blocks, axis=1)


def _gla_state_out(st):
    return jnp.stack([st[:, h * GLA_DV:(h + 1) * GLA_DV, h * GLA_DK:(h + 1) * GLA_DK].transpose(0, 2, 1)
                      for h in range(GLA_HEADS)], axis=1)


def _gla_branch(q, k, v, gg, r, s0, lp, db, steps):
    st0 = _gla_state_in(s0)
    args = (lp["gla_wg2"], lp["gla_bg"], lp["gla_norm"])
    if db == 1:
        chunk = min(steps, 128)
        y, st = _gla_scan(q[None], k[None], v[None], gg[None], r[None], st0, *args, chunk, chunk)
        return y[0], _gla_state_out(st)
    qs, ks, vs, gs, rs = (_to_seq_major(a, db, steps, SUBLANES) for a in (q, k, v, gg, r))
    y, st = _gla_scan(qs, ks, vs, gs, rs, st0, *args, SUBLANES, steps)
    return _to_time_major(y, steps), _gla_state_out(st)


DIFF_MAPS = DIFF_HEADS * 2
POS_SPLIT = 128


def _diff_stack_q(q, rows):
    lane = _lane_iota((rows, LANES))
    blocks = []
    for kv in range(DIFF_KV):
        for g in range(DIFF_HEADS // DIFF_KV):
            qg = q[:, g * LANES:(g + 1) * LANES]
            for i in range(2):
                lo = kv * 2 * DIFF_DIM + i * DIFF_DIM
                blocks.append(jnp.where((lane >= lo) & (lane < lo + DIFF_DIM), qg, 0.0))
    return jnp.concatenate(blocks, axis=0).astype(BF16)


LOG2E = math.log2(math.e)
DIFF_QSCALE = DIFF_DIM ** -0.5 * LOG2E


def _diff_slope(blk):
    head = blk // 2
    return 2.0 ** (-8.0 * (head + 1) / DIFF_HEADS) * LOG2E


def _bf16_parts(c):
    parts, r = [], np.float32(c)
    for _ in range(3):
        p = np.float32(np.asarray(r).astype(BF16))
        parts.append(float(p))
        r = np.float32(r - p)
    return parts


def _online_update(s, blk, rows, m_ref, l_ref, p_ref):
    r0 = blk * rows
    m_prev = m_ref[r0:r0 + rows, :]
    m_new = jnp.maximum(m_prev, jnp.max(s, axis=1, keepdims=True))
    alpha = jnp.exp2(m_prev - m_new)
    p = jnp.exp2(s - m_new[:, 0:1])
    l_ref[r0:r0 + rows, :] = alpha * l_ref[r0:r0 + rows, :] + jnp.sum(p, axis=1, keepdims=True)
    m_ref[r0:r0 + rows, :] = m_new
    p_ref[r0:r0 + rows, 0:s.shape[1]] = p.astype(p_ref.dtype)
    return alpha


def _diff_finish(acc_ref, l_ref, prm_ref, rows):
    lane = _lane_iota((rows, LANES))
    low = lane < 2 * DIFF_DIM
    lam = prm_ref[1:2, :]
    outs = []
    for g in range(DIFF_HEADS // DIFF_KV):
        parts = []
        for kv in range(DIFF_KV):
            b1 = ((kv * 2 + g) * 2) * rows
            b2 = b1 + rows
            o1 = acc_ref[b1:b1 + rows, :] / l_ref[b1:b1 + rows, :]
            o2 = acc_ref[b2:b2 + rows, :] / l_ref[b2:b2 + rows, :]
            parts.append(o1 - lam * o2)
        og = jnp.where(low, parts[0], parts[1])
        sq = og * og
        ss = jnp.where(low, jnp.sum(jnp.where(low, sq, 0.0), axis=1, keepdims=True),
                       jnp.sum(jnp.where(low, 0.0, sq), axis=1, keepdims=True))
        outs.append(og * lax.rsqrt(ss * (1.0 / (2 * DIFF_DIM)) + EPS) * prm_ref[0:1, :])
    return outs


def _diff_stack_q_alibi(q, rows):
    row = _row_iota((LANES, rows))
    blocks = []
    for kv in range(DIFF_KV):
        for g in range(DIFF_HEADS // DIFF_KV):
            qg = q[g * LANES:(g + 1) * LANES, :]
            for i in range(2):
                lo = kv * 2 * DIFF_DIM + i * DIFF_DIM
                parts = _bf16_parts(_diff_slope((kv * 2 + g) * 2 + i))
                qm = jnp.where((row >= lo) & (row < lo + DIFF_DIM), qg, 0.0)
                aug = jnp.zeros((LANES, rows), F32)
                for j, part in enumerate(parts):
                    aug = jnp.where(row == j, part * POS_SPLIT, jnp.where(row == len(parts) + j, part, aug))
                blocks.append(jnp.concatenate([qm, aug], axis=0))
    return jnp.concatenate(blocks, axis=1).astype(BF16)


FLASH_COLS = 2 * LANES
FLASH_AHEAD = 4


def _flash_tiles(k_ref, vt_ref, qs_ref, m_ref, acc_ref, q0, tq, tk):
    width = qs_ref.shape[1]

    def kv_tile(ki, masked):
        k = k_ref[pl.ds(pl.multiple_of(ki * tk, tk), tk), :]
        vt = vt_ref[ki]
        m_all = m_ref[...]
        ms, accs = [], []
        starts = list(range(0, width, FLASH_COLS))
        pending = [_dot(k, qs_ref[:, c:c + FLASH_COLS]) for c in starts[:FLASH_AHEAD]]
        for idx, c0 in enumerate(starts):
            cols = slice(c0, c0 + FLASH_COLS)
            s = pending.pop(0)
            if idx + FLASH_AHEAD < len(starts):
                c_next = starts[idx + FLASH_AHEAD]
                pending.append(_dot(k, qs_ref[:, c_next:c_next + FLASH_COLS]))
            if masked:
                kpos = ki * tk + _row_iota((tk, FLASH_COLS))
                qpos = q0 + (c0 + _lane_iota((tk, FLASH_COLS))) % tq
                s = jnp.where(kpos <= qpos, s, NEG)
            m_prev = m_all[:, cols]
            m_new = jnp.maximum(m_prev, jnp.max(s, axis=0, keepdims=True))
            p = jnp.exp2(s - m_new).astype(BF16)
            ms.append(m_new)
            accs.append(jnp.exp2(m_prev - m_new) * acc_ref[:, cols] + _dot(vt, p))
        m_ref[...] = jnp.concatenate(ms, axis=1)
        acc_ref[...] = jnp.concatenate(accs, axis=1)

    m_ref[...] = jnp.full(m_ref.shape, NEG, F32)
    acc_ref[...] = jnp.zeros(acc_ref.shape, F32)
    n_full = q0 // tk

    def body(ki, carry):
        kv_tile(ki, False)
        return carry

    lax.fori_loop(0, n_full, body, 0)
    for j in range(max(tq // tk, 1)):
        kv_tile(n_full + j, True)


def _diff_flash_kernel(q_ref, k_ref, vt_ref, lam_ref, gcol_ref, o_ref, qs_ref, m_ref, acc_ref, *, tq, tk):
    q0 = pl.program_id(0) * tq
    qs_ref[...] = _diff_stack_q_alibi(q_ref[...] * DIFF_QSCALE, tq)
    _flash_tiles(k_ref, vt_ref, qs_ref, m_ref, acc_ref, q0, tq, tk)
    row = _row_iota((LANES, tq))
    low = row < 2 * DIFF_DIM
    lam = lam_ref[...]
    for g in range(DIFF_HEADS // DIFF_KV):
        parts = []
        for kv in range(DIFF_KV):
            c1 = ((kv * 2 + g) * 2) * tq
            c2 = c1 + tq
            o1 = acc_ref[0:LANES, c1:c1 + tq] / acc_ref[LANES:LANES + 1, c1:c1 + tq]
            o2 = acc_ref[0:LANES, c2:c2 + tq] / acc_ref[LANES:LANES + 1, c2:c2 + tq]
            parts.append(o1 - lam * o2)
        og = jnp.where(low, parts[0], parts[1])
        sq = og * og
        ss = jnp.where(low, jnp.sum(jnp.where(low, sq, 0.0), axis=0, keepdims=True),
                       jnp.sum(jnp.where(low, 0.0, sq), axis=0, keepdims=True))
        o_ref[g * LANES:(g + 1) * LANES, :] = og * lax.rsqrt(ss * (1.0 / (2 * DIFF_DIM)) + EPS) * gcol_ref[...]


def _resident(shape):
    return pl.BlockSpec(shape, lambda *_: (0,) * len(shape), pipeline_mode=pl.Buffered(1))


def _value_tiles_t(v, tk):
    n, c = v.shape
    tiles = v.reshape(n // tk, tk, c).transpose(0, 2, 1)
    extra = jnp.zeros((n // tk, SUBLANES, tk), v.dtype).at[:, 0, :].set(1.0)
    return jnp.concatenate([tiles, extra], axis=1)


def _diff_flash(q_t, k_aug, v_t, lam_row, gcol, tq, tk):
    n = q_t.shape[1]
    assert (tk % tq == 0 or tq % tk == 0) and n % tk == 0 and n % tq == 0 and tq % FLASH_COLS == 0
    width = DIFF_MAPS * tq
    return pl.pallas_call(
        functools.partial(_diff_flash_kernel, tq=tq, tk=tk),
        grid=(n // tq,),
        in_specs=[pl.BlockSpec((BR, tq), lambda qi: (0, qi)),
                  _resident(k_aug.shape),
                  _resident(v_t.shape),
                  _resident((1, tq)),
                  _resident((LANES, tq))],
        out_specs=pl.BlockSpec((BR, tq), lambda qi: (0, qi)),
        out_shape=jax.ShapeDtypeStruct((BR, n), F32),
        scratch_shapes=[pltpu.VMEM((2 * LANES, width), BF16),
                        pltpu.VMEM((1, width), F32),
                        pltpu.VMEM((LANES + SUBLANES, width), F32)],
        compiler_params=_cparams(("arbitrary",)),
        name="diff_flash",
    )(q_t, k_aug, v_t, lam_row, gcol)


def _diff_keys_alibi(k, pos):
    assert k.shape[0] <= POS_SPLIT * 256
    hi, lo = pos // POS_SPLIT, pos % POS_SPLIT
    cols = jnp.stack([hi, hi, hi, lo, lo, lo], axis=1).astype(BF16)
    return jnp.concatenate([k.astype(BF16), _pad_lanes(cols, LANES)], axis=1)


def _page_copies(pt_ref, caches, bufs, sems, layer, b, c, slot, g, pages_per_step):
    page = pt_ref[b, c * pages_per_step + g]
    off = pl.multiple_of(g * PAGE, PAGE)
    copies = []
    for a, (cache, buf) in enumerate(zip(caches, bufs)):
        key_minor = buf.shape[2] == pages_per_step * PAGE
        dst = buf.at[slot, :, pl.ds(off, PAGE)] if key_minor else buf.at[slot, pl.ds(off, PAGE), :]
        copies.append(pltpu.make_async_copy(cache.at[layer, page], dst, sems.at[slot, a]))
    return copies


def _stream_pages(pt_ref, caches, bufs, sems, layer, pages_per_step, n_chunks):
    G = pages_per_step
    b, c = pl.program_id(0), pl.program_id(1)
    step = b * n_chunks + c
    last = step + 1 == pl.num_programs(0) * n_chunks
    slot = step % 2
    wrap = c + 1 == n_chunks
    nb = jnp.where(last, b, jnp.where(wrap, b + 1, b))
    nc = jnp.where(last, c, jnp.where(wrap, 0, c + 1))

    def for_pages(fn):
        def body(g, carry):
            fn(g)
            return carry
        lax.fori_loop(0, G, body, 0)

    def start(bb, cc, sl, g):
        for cp in _page_copies(pt_ref, caches, bufs, sems, layer, bb, cc, sl, g, G):
            cp.start()

    def wait(bb, cc, sl, g):
        for cp in _page_copies(pt_ref, caches, bufs, sems, layer, bb, cc, sl, g, G):
            cp.wait()

    @pl.when(step == 0)
    def _():
        for_pages(lambda g: start(b, c, slot, g))

    for g in range(G):
        wait(b, c, slot, g)

    def start_next():
        for g in range(G):
            start(nb, nc, 1 - slot, g)

    def drain():
        @pl.when(last)
        def _():
            for_pages(lambda g: wait(nb, nc, 1 - slot, g))

    return slot, start_next, drain


def _diff_decode_kernel(pt_ref, q_ref, kn_ref, vn_ref, prm_ref, kc_ref, vc_ref, o_ref,
                        kbuf, vbuf, sems, qs_ref, m_ref, l_ref, acc_ref, p_ref,
                        *, layer, pages_per_step, n_chunks, steps, past_len):
    G = pages_per_step
    R8 = SUBLANES
    c = pl.program_id(1)
    slot, start_next, drain = _stream_pages(pt_ref, (kc_ref, vc_ref), (kbuf, vbuf), sems, layer, G, n_chunks)

    @pl.when(c == 0)
    def _():
        qs_ref[...] = _diff_stack_q(q_ref[0] * DIFF_QSCALE, R8)
        m_ref[...] = jnp.full(m_ref.shape, NEG, F32)
        l_ref[...] = jnp.zeros(l_ref.shape, F32)
        acc_ref[...] = jnp.zeros(acc_ref.shape, F32)

    qs = qs_ref[...]
    width = G * PAGE
    start_next()
    s = _dot(qs, kbuf[slot].astype(BF16))
    ndist = ((c * width + _lane_iota((R8, width))) - (past_len + _row_iota((R8, width)))).astype(F32)
    for blk in range(DIFF_MAPS):
        sb = s[blk * R8:(blk + 1) * R8] + _diff_slope(blk) * ndist
        alpha = _online_update(sb, blk, R8, m_ref, l_ref, p_ref)
        acc_ref[blk * R8:(blk + 1) * R8, :] = alpha * acc_ref[blk * R8:(blk + 1) * R8, :]
    acc_ref[...] = acc_ref[...] + _dot_nt(p_ref[...].astype(BF16), vbuf[slot].astype(BF16))
    drain()

    @pl.when(c == n_chunks - 1)
    def _():
        pad = jnp.zeros((PAGE - R8, LANES), F32)
        kn = jnp.concatenate([kn_ref[0], pad], axis=0).astype(BF16)
        vn = jnp.concatenate([vn_ref[0], pad], axis=0).astype(BF16)
        sn = _dot_nt(qs, kn)
        kt = _lane_iota((R8, PAGE))
        qt = _row_iota((R8, PAGE))
        vis = (kt <= qt) & (kt < steps)
        nd = (kt - qt).astype(F32)
        for blk in range(DIFF_MAPS):
            sb = jnp.where(vis, sn[blk * R8:(blk + 1) * R8] + _diff_slope(blk) * nd, NEG)
            alpha = _online_update(sb, blk, R8, m_ref, l_ref, p_ref)
            acc_ref[blk * R8:(blk + 1) * R8, :] = alpha * acc_ref[blk * R8:(blk + 1) * R8, :]
        acc_ref[...] = acc_ref[...] + _dot(p_ref[:, 0:PAGE].astype(BF16), vn)
        outs = _diff_finish(acc_ref, l_ref, prm_ref, R8)
        o_ref[0, :, 0:LANES] = outs[0]
        o_ref[0, :, LANES:2 * LANES] = outs[1]


def _diff_decode(page_table, q, k_new, v_new, prm, kc, vc, layer, steps, pages_per_step):
    db, n_pages = page_table.shape
    G = pages_per_step
    n_chunks = n_pages // G
    rows = DIFF_MAPS * SUBLANES
    seq3 = lambda b, c, pt: (b, 0, 0)
    grid_spec = pltpu.PrefetchScalarGridSpec(
        num_scalar_prefetch=1,
        grid=(db, n_chunks),
        in_specs=[pl.BlockSpec((1, SUBLANES, BR), seq3),
                  pl.BlockSpec((1, SUBLANES, LANES), seq3),
                  pl.BlockSpec((1, SUBLANES, LANES), seq3),
                  pl.BlockSpec((SUBLANES, LANES), lambda b, c, pt: (0, 0)),
                  pl.BlockSpec(memory_space=pl.ANY),
                  pl.BlockSpec(memory_space=pl.ANY)],
        out_specs=pl.BlockSpec((1, SUBLANES, BR), seq3),
        scratch_shapes=[pltpu.VMEM((2, LANES, G * PAGE), F32),
                        pltpu.VMEM((2, LANES, G * PAGE), F32),
                        pltpu.SemaphoreType.DMA((2, 2)),
                        pltpu.VMEM((rows, LANES), BF16),
                        pltpu.VMEM((rows, LANES), F32),
                        pltpu.VMEM((rows, LANES), F32),
                        pltpu.VMEM((rows, LANES), F32),
                        pltpu.VMEM((rows, G * PAGE), F32)])
    return pl.pallas_call(
        functools.partial(_diff_decode_kernel, layer=layer, pages_per_step=G, n_chunks=n_chunks,
                          steps=steps, past_len=n_pages * PAGE),
        grid_spec=grid_spec,
        out_shape=jax.ShapeDtypeStruct((db, SUBLANES, BR), F32),
        compiler_params=_cparams(("arbitrary", "arbitrary")),
        name="diff_decode",
    )(page_table, q, k_new, v_new, prm, kc, vc)


MLA_QW = 2 * LANES
MLA_SCALE = (MLA_NOPE + MLA_ROPE) ** -0.5 * LOG2E
ROPE_HALF = MLA_ROPE // 2


def _mla_prep_kernel(cq_ref, ckv_ref, kr_ref, cos_ref, sin_ref, qn_ref, kvn_ref, wuq_ref, wcat_ref,
                     qcat_ref, kcat_ref, c_ref, krout_ref, *, q_transposed):
    tm = cq_ref.shape[0]
    lane = _lane_iota((tm, LANES))
    cos = cos_ref[...]
    sin = sin_ref[...]
    q = _dot(_rms(cq_ref[...], qn_ref[...]).astype(BF16), wuq_ref[...])
    n_nope = MLA_HEADS * MLA_NOPE
    rq = q[:, n_nope:n_nope + LANES]
    rq = rq * cos + pltpu.roll(rq, LANES // 2, 1) * jnp.where(lane < LANES // 2, -sin, sin)
    q_in = (jnp.concatenate([q[:, 0:n_nope], rq], axis=1) * MLA_SCALE).astype(BF16)
    wcat_t = wcat_ref[...]
    qcat_ref[...] = _dot_nt(wcat_t, q_in) if q_transposed else _dot_nt(q_in, wcat_t)
    c = _rms(ckv_ref[...], kvn_ref[...])
    c_ref[...] = c
    kr = kr_ref[...]
    swapped = jnp.where(lane < ROPE_HALF, pltpu.roll(kr, LANES - ROPE_HALF, 1), pltpu.roll(kr, ROPE_HALF, 1))
    kro = kr * cos + swapped * jnp.where(lane < ROPE_HALF, -sin, sin)
    kro = jnp.where(lane < MLA_ROPE, kro, 0.0)
    krout_ref[...] = kro[:, 0:MLA_ROPE]
    kcat_ref[...] = jnp.concatenate([c, kro], axis=1)


def _mla_prep(cq, ckv, kr, cos, sin, lp, tm, q_transposed):
    n = cq.shape[0]
    qw = MLA_HEADS * MLA_QW
    row = lambda i: (i, 0)
    fixed = lambda i: (0, 0)
    q_spec = pl.BlockSpec((qw, tm), lambda i: (0, i)) if q_transposed else pl.BlockSpec((tm, qw), row)
    return pl.pallas_call(
        functools.partial(_mla_prep_kernel, q_transposed=q_transposed),
        grid=(n // tm,),
        in_specs=[pl.BlockSpec((tm, MLA_QR), row), pl.BlockSpec((tm, LANES), row), pl.BlockSpec((tm, LANES), row),
                  pl.BlockSpec((tm, LANES), row), pl.BlockSpec((tm, LANES), row),
                  pl.BlockSpec((1, MLA_QR), fixed), pl.BlockSpec((1, LANES), fixed),
                  pl.BlockSpec(lp["mla_wuq"].shape, fixed), pl.BlockSpec(lp["mla_wcat_t"].shape, fixed)],
        out_specs=[q_spec, pl.BlockSpec((tm, MLA_QW), row),
                   pl.BlockSpec((tm, LANES), row), pl.BlockSpec((tm, MLA_ROPE), row)],
        out_shape=[jax.ShapeDtypeStruct((qw, n) if q_transposed else (n, qw), F32),
                   jax.ShapeDtypeStruct((n, MLA_QW), F32),
                   jax.ShapeDtypeStruct((n, LANES), F32), jax.ShapeDtypeStruct((n, MLA_ROPE), F32)],
        compiler_params=_cparams(("arbitrary",)),
        name="mla_prep",
    )(cq, ckv, kr, cos, sin, lp["mla_q_norm"], lp["mla_kv_norm"], lp["mla_wuq"], lp["mla_wcat_t"])


def _mla_stack_q(qcat):
    return jnp.concatenate([qcat[:, h * MLA_QW:(h + 1) * MLA_QW] for h in range(MLA_HEADS)], axis=0).astype(BF16)


def _mla_finish(acc_ref, l_ref, wuv_ref, rows):
    y = jnp.zeros((rows, BR), F32)
    for h in range(MLA_HEADS):
        o_lat = acc_ref[h * rows:(h + 1) * rows, :] / l_ref[h * rows:(h + 1) * rows, :]
        y = y + _dot(o_lat.astype(BF16), wuv_ref[h])
    return y


def _mla_flash_kernel(q_ref, k_ref, ct_ref, wuvt_ref, o_ref, qs_ref, m_ref, acc_ref, *, tq, tk):
    q0 = pl.program_id(0) * tq
    qs_ref[...] = jnp.concatenate([q_ref[h * MLA_QW:(h + 1) * MLA_QW, :] for h in range(MLA_HEADS)],
                                  axis=1).astype(BF16)
    _flash_tiles(k_ref, ct_ref, qs_ref, m_ref, acc_ref, q0, tq, tk)
    y = jnp.zeros((BR, tq), F32)
    for h in range(MLA_HEADS):
        cols = slice(h * tq, (h + 1) * tq)
        o_lat = acc_ref[0:MLA_KVR, cols] / acc_ref[MLA_KVR:MLA_KVR + 1, cols]
        y = y + _dot(wuvt_ref[h], o_lat.astype(BF16))
    o_ref[...] = y


def _mla_flash(qcat_t, kcat, c_t, wuv_t, tq, tk):
    n = qcat_t.shape[1]
    assert (tk % tq == 0 or tq % tk == 0) and n % tk == 0 and n % tq == 0 and tq % FLASH_COLS == 0
    width = MLA_HEADS * tq
    return pl.pallas_call(
        functools.partial(_mla_flash_kernel, tq=tq, tk=tk),
        grid=(n // tq,),
        in_specs=[pl.BlockSpec((MLA_HEADS * MLA_QW, tq), lambda qi: (0, qi)),
                  _resident(kcat.shape),
                  _resident(c_t.shape),
                  _resident(wuv_t.shape)],
        out_specs=pl.BlockSpec((BR, tq), lambda qi: (0, qi)),
        out_shape=jax.ShapeDtypeStruct((BR, n), F32),
        scratch_shapes=[pltpu.VMEM((MLA_QW, width), BF16),
                        pltpu.VMEM((1, width), F32),
                        pltpu.VMEM((MLA_KVR + SUBLANES, width), F32)],
        compiler_params=_cparams(("arbitrary",)),
        name="mla_flash",
    )(qcat_t, kcat, c_t, wuv_t)


def _mla_decode_kernel(pt_ref, q_ref, kn_ref, wuv_ref, cc_ref, rc_ref, o_ref,
                       cbuf, rbuf, sems, qs_ref, m_ref, l_ref, acc_ref, p_ref,
                       *, layer, pages_per_step, n_chunks, steps):
    G = pages_per_step
    R8 = SUBLANES
    c = pl.program_id(1)
    slot, start_next, drain = _stream_pages(pt_ref, (cc_ref, rc_ref), (cbuf, rbuf), sems, layer, G, n_chunks)

    @pl.when(c == 0)
    def _():
        qs_ref[...] = _mla_stack_q(q_ref[0])
        m_ref[...] = jnp.full(m_ref.shape, NEG, F32)
        l_ref[...] = jnp.zeros(l_ref.shape, F32)
        acc_ref[...] = jnp.zeros(acc_ref.shape, F32)

    qs = qs_ref[...]
    q_lat = qs[:, 0:MLA_KVR]
    q_rope = qs[:, MLA_KVR:MLA_KVR + MLA_ROPE]

    start_next()
    cb = cbuf[slot].astype(BF16)
    s = _dot_nt(q_lat, cb) + _dot(q_rope, rbuf[slot].astype(BF16))
    for h in range(MLA_HEADS):
        alpha = _online_update(s[h * R8:(h + 1) * R8], h, R8, m_ref, l_ref, p_ref)
        acc_ref[h * R8:(h + 1) * R8, :] = alpha * acc_ref[h * R8:(h + 1) * R8, :]
    acc_ref[...] = acc_ref[...] + _dot(p_ref[...].astype(BF16), cb)
    drain()

    @pl.when(c == n_chunks - 1)
    def _():
        kn = jnp.concatenate([kn_ref[0], jnp.zeros((PAGE - R8, MLA_QW), F32)], axis=0).astype(BF16)
        sn = _dot_nt(qs, kn)
        kt = _lane_iota((R8, PAGE))
        vis = (kt <= _row_iota((R8, PAGE))) & (kt < steps)
        for h in range(MLA_HEADS):
            alpha = _online_update(jnp.where(vis, sn[h * R8:(h + 1) * R8], NEG), h, R8, m_ref, l_ref, p_ref)
            acc_ref[h * R8:(h + 1) * R8, :] = alpha * acc_ref[h * R8:(h + 1) * R8, :]
        acc_ref[...] = acc_ref[...] + _dot(p_ref[:, 0:PAGE].astype(BF16), kn[:, 0:MLA_KVR])
        o_ref[0] = _mla_finish(acc_ref, l_ref, wuv_ref, R8)


def _mla_decode(page_table, qcat, kcat_new, wuv, cc, rc, layer, steps, pages_per_step):
    db, n_pages = page_table.shape
    G = pages_per_step
    n_chunks = n_pages // G
    rows = MLA_HEADS * SUBLANES
    seq3 = lambda b, c, pt: (b, 0, 0)
    grid_spec = pltpu.PrefetchScalarGridSpec(
        num_scalar_prefetch=1,
        grid=(db, n_chunks),
        in_specs=[pl.BlockSpec((1, SUBLANES, MLA_HEADS * MLA_QW), seq3),
                  pl.BlockSpec((1, SUBLANES, MLA_QW), seq3),
                  pl.BlockSpec(wuv.shape, lambda b, c, pt: (0, 0, 0)),
                  pl.BlockSpec(memory_space=pl.ANY),
                  pl.BlockSpec(memory_space=pl.ANY)],
        out_specs=pl.BlockSpec((1, SUBLANES, BR), seq3),
        scratch_shapes=[pltpu.VMEM((2, G * PAGE, MLA_KVR), F32),
                        pltpu.VMEM((2, MLA_ROPE, G * PAGE), F32),
                        pltpu.SemaphoreType.DMA((2, 2)),
                        pltpu.VMEM((rows, MLA_QW), BF16),
                        pltpu.VMEM((rows, LANES), F32),
                        pltpu.VMEM((rows, LANES), F32),
                        pltpu.VMEM((rows, LANES), F32),
                        pltpu.VMEM((rows, G * PAGE), F32)])
    return pl.pallas_call(
        functools.partial(_mla_decode_kernel, layer=layer, pages_per_step=G, n_chunks=n_chunks, steps=steps),
        grid_spec=grid_spec,
        out_shape=jax.ShapeDtypeStruct((db, SUBLANES, BR), F32),
        compiler_params=_cparams(("arbitrary", "arbitrary")),
        name="mla_decode",
    )(page_table, qcat, kcat_new, wuv, cc, rc)


def _mem_attn_kernel(q_ref, mk_ref, mv_ref, o_ref, *, key_minor):
    rows = q_ref.shape[1]
    lane_head = _lane_iota((rows, BR)) // MEM_DIM
    for j in range(q_ref.shape[0]):
        jk = j if mk_ref.shape[0] > 1 else 0
        q = q_ref[j] * (MEM_DIM ** -0.5)
        qs = jnp.concatenate([jnp.where(lane_head == h, q, 0.0) for h in range(MEM_HEADS)], axis=0).astype(BF16)
        mk = mk_ref[jk].astype(BF16)
        mv = mv_ref[jk].astype(BF16)
        s = _dot(qs, mk) if key_minor else _dot_nt(qs, mk)
        p = jnp.exp(s - jnp.max(s, axis=1, keepdims=True))
        p = (p / jnp.sum(p, axis=1, keepdims=True)).astype(BF16)
        r = _dot_nt(p, mv) if key_minor else _dot(p, mv)
        y = jnp.zeros((rows, BR), F32)
        for h in range(MEM_HEADS):
            y = y + jnp.where(lane_head == h, r[h * rows:(h + 1) * rows], 0.0)
        o_ref[j] = y


def _mem_attn(q3, mk3, mv3, key_minor):
    b, rows, _ = q3.shape
    per_seq = mk3.shape[0] == b and b > 1
    sb = math.gcd(b, SCAN_SEQS) if per_seq else 1
    kv_map = (lambda i: (i, 0, 0)) if per_seq else (lambda i: (0, 0, 0))
    return pl.pallas_call(
        functools.partial(_mem_attn_kernel, key_minor=key_minor),
        grid=(b // sb,),
        in_specs=[pl.BlockSpec((sb, rows, BR), lambda i: (i, 0, 0)),
                  pl.BlockSpec((sb,) + mk3.shape[1:], kv_map),
                  pl.BlockSpec((sb,) + mv3.shape[1:], kv_map)],
        out_specs=pl.BlockSpec((sb, rows, BR), lambda i: (i, 0, 0)),
        out_shape=jax.ShapeDtypeStruct((b, rows, BR), F32),
        compiler_params=_cparams(("arbitrary",)),
        name="mem_attn",
    )(q3, mk3, mv3)


def _matmul_kernel(x_ref, w_ref, o_ref):
    o_ref[...] = _dot(x_ref[...].astype(BF16), w_ref[...])


def _matmul(x, w):
    m, k = x.shape
    n = w.shape[1]
    return pl.pallas_call(
        _matmul_kernel,
        grid=(1,),
        in_specs=[pl.BlockSpec((m, k), lambda i: (0, 0)), pl.BlockSpec((k, n), lambda i: (0, 0))],
        out_specs=pl.BlockSpec((m, n), lambda i: (0, 0)),
        out_shape=jax.ShapeDtypeStruct((m, n), F32),
        compiler_params=_cparams(("arbitrary",)),
        name="mem_proj",
    )(x, w)


def _merge_kernel(x_ref, y0, y1, y2, y3, y4, gpre_ref, gpost_ref, wg_ref, wb_ref, wo_ref, o_ref, *, transposed):
    x = x_ref[...]
    d = x.shape[1]
    hn = _rms(x, gpre_ref[...]).astype(BF16)
    m = jnp.zeros(x.shape, F32)
    for i, y_ref in enumerate((y0, y1, y2, y3, y4)):
        gate = _sigmoid(_dot_nt(hn, wg_ref[i * d:(i + 1) * d, :]))
        y = y_ref[...].astype(BF16)
        m = m + gate * (_dot_tn(y, wb_ref[i]) if i in transposed else _dot(y, wb_ref[i]))
    o_ref[...] = x + _rms(_dot(m.astype(BF16), wo_ref[...]), gpost_ref[...])


def _merge_out(x, ys, lp, tm, transposed=()):
    n, d = x.shape
    row = lambda i: (i, 0)
    fixed = lambda i: (0, 0)
    y_specs = [pl.BlockSpec((BR, tm), lambda i: (0, i)) if j in transposed else pl.BlockSpec((tm, BR), row)
               for j in range(N_BRANCH)]
    return pl.pallas_call(
        functools.partial(_merge_kernel, transposed=transposed),
        grid=(n // tm,),
        in_specs=[pl.BlockSpec((tm, d), row)] + y_specs + [
            pl.BlockSpec((1, d), fixed), pl.BlockSpec((1, d), fixed),
            _resident(lp["w_gate"].shape), _resident(lp["w_branch"].shape), _resident(lp["w_out"].shape)],
        out_specs=pl.BlockSpec((tm, d), row),
        out_shape=jax.ShapeDtypeStruct((n, d), F32),
        compiler_params=_cparams(("arbitrary",)),
        name="merge_out",
    )(x, *ys, lp["g_pre_mix"], lp["g_post_mix"], lp["w_gate"], lp["w_branch"], lp["w_out"])


FFN_CHUNK = 2 * LANES


def _ffn_kernel(x_ref, st_ref, gpre_ref, gpost_ref, wi_ref, cw_ref, cb_ref, wo_ref, o_ref, ns_ref,
                ext_ref, acc_ref, *, shift, pad, width):
    tm = x_ref.shape[0]

    @pl.when(pl.program_id(0) == 0)
    def _():
        ext_ref[0:pad, :] = st_ref[...]

    x = x_ref[...]
    hn = _rms(x, gpre_ref[...]).astype(BF16)
    for c0 in range(0, width, FFN_CHUNK):
        a = _dot(hn, wi_ref[:, c0:c0 + FFN_CHUNK])
        gate = _dot(hn, wi_ref[:, width + c0:width + c0 + FFN_CHUNK])
        ext_ref[pad:pad + tm, c0:c0 + FFN_CHUNK] = a
        conv = cb_ref[:, c0:c0 + FFN_CHUNK] + cw_ref[FFN_CONV - 1:FFN_CONV, c0:c0 + FFN_CHUNK] * a
        for j in range(FFN_CONV - 1):
            off = pad - (FFN_CONV - 1 - j) * shift
            conv = conv + cw_ref[j:j + 1, c0:c0 + FFN_CHUNK] * ext_ref[off:off + tm, c0:c0 + FFN_CHUNK]
        act = (_gelu_tanh(conv) * gate).astype(BF16)
        contrib = _dot(act, wo_ref[c0:c0 + FFN_CHUNK, :])
        if c0 == 0:
            acc_ref[...] = contrib
        else:
            acc_ref[...] = acc_ref[...] + contrib
    o_ref[...] = x + _rms(acc_ref[...], gpost_ref[...])
    tail = ext_ref[tm:tm + pad, :]
    ns_ref[...] = tail
    ext_ref[0:pad, :] = tail


def _ffn(x, state, lp, shift, tm):
    n, d = x.shape
    pad, width = state.shape
    assert tm >= pad and pad >= (FFN_CONV - 1) * shift and width % FFN_CHUNK == 0
    row = lambda i: (i, 0)
    fixed = lambda i: (0, 0)
    return pl.pallas_call(
        functools.partial(_ffn_kernel, shift=shift, pad=pad, width=width),
        grid=(n // tm,),
        in_specs=[pl.BlockSpec((tm, d), row), pl.BlockSpec((pad, width), fixed),
                  pl.BlockSpec((1, d), fixed), pl.BlockSpec((1, d), fixed),
                  _resident(lp["w_ffn_in"].shape),
                  pl.BlockSpec((FFN_CONV, width), fixed), pl.BlockSpec((1, width), fixed),
                  _resident(lp["w_ffn_out"].shape)],
        out_specs=[pl.BlockSpec((tm, d), row), pl.BlockSpec((pad, width), fixed)],
        out_shape=[jax.ShapeDtypeStruct((n, d), F32), jax.ShapeDtypeStruct((pad, width), F32)],
        scratch_shapes=[pltpu.VMEM((pad + tm, width), F32), pltpu.VMEM((tm, d), F32)],
        compiler_params=_cparams(("arbitrary",)),
        name="conv_ffn",
    )(x, state, lp["g_pre_ffn"], lp["g_post_ffn"], lp["w_ffn_in"], lp["ffn_conv_w"], lp["ffn_conv_b"],
      lp["w_ffn_out"])


def _rope_tables(pos):
    freqs = jnp.power(ROPE_THETA, -jnp.arange(ROPE_HALF, dtype=F32) / ROPE_HALF)
    ang = pos.astype(F32)[:, None] * freqs
    reps = LANES // ROPE_HALF
    return jnp.tile(jnp.cos(ang), (1, reps)), jnp.tile(jnp.sin(ang), (1, reps))


def _prep_layer(li, P):
    d_model = P["w_in"].shape[1]
    w_in_t = jnp.transpose(P["w_in"], (2, 0, 1))[:, li, :]
    bounds = np.cumsum((0,) + IN_WIDTHS)
    seg = [w_in_t[bounds[i]:bounds[i + 1]] for i in range(len(IN_WIDTHS))]
    grp = DIFF_HEADS // DIFF_KV
    seg[3] = seg[3].reshape(DIFF_KV, grp, 2 * DIFF_DIM, d_model).swapaxes(0, 1).reshape(-1, d_model)
    seg = [jnp.pad(s, ((0, wd - s.shape[0]), (0, 0))) for s, wd in zip(seg, PROJ_WIDTHS)]
    lp = {"w_cat": jnp.concatenate(seg, axis=0).astype(BF16),
          "w_gate": w_in_t[bounds[-1]:].astype(BF16)}
    for name in ("g_pre_mix", "g_post_mix", "g_pre_ffn", "g_post_ffn", "ssd_conv_b", "ssd_norm", "gla_bg",
                 "mla_q_norm", "mla_kv_norm", "ffn_conv_b"):
        lp[name] = P[name][li][None]
    lp["ssd_conv_w"] = P["ssd_conv_w"][li]
    lp["ffn_conv_w"] = P["ffn_conv_w"][li]
    prm = jnp.zeros((SUBLANES, LANES), F32)
    lp["ssd_prm"] = prm.at[0, :SSD_HEADS].set(P["ssd_dt_bias"][li]).at[1, :SSD_HEADS].set(-jnp.exp(P["ssd_a_log"][li]))
    lp["ssd_dskip"] = jnp.repeat(P["ssd_d"][li], SSD_HEAD_DIM)[None]
    lam_init = 0.8 - 0.6 * math.exp(-0.3 * li)
    lam = (jnp.exp(jnp.sum(P["diff_lq1"][li] * P["diff_lk1"][li]))
           - jnp.exp(jnp.sum(P["diff_lq2"][li] * P["diff_lk2"][li])) + lam_init)
    lp["diff_prm"] = prm.at[0].set(jnp.tile(P["diff_norm"][li], DIFF_KV) * (1.0 - lam_init)).at[1].set(lam)
    lp["gla_wg2"] = jnp.zeros((LANES, LANES), F32).at[:GLA_RANK].set(P["gla_wg2"][li]).astype(BF16)
    lp["gla_norm"] = jnp.tile(P["gla_norm"][li], GLA_HEADS)[None]
    wuq = P["mla_wuq"][li].reshape(MLA_QR, MLA_HEADS, MLA_NOPE + MLA_ROPE)
    lp["mla_wuq"] = jnp.concatenate(
        [wuq[:, :, :MLA_NOPE].reshape(MLA_QR, -1),
         wuq[:, :, MLA_NOPE:MLA_NOPE + ROPE_HALF].reshape(MLA_QR, -1),
         wuq[:, :, MLA_NOPE + ROPE_HALF:].reshape(MLA_QR, -1)], axis=1).astype(BF16)
    n_nope = MLA_HEADS * MLA_NOPE
    wcat = jnp.zeros((n_nope + LANES, MLA_HEADS * MLA_QW), F32)
    wuv = jnp.zeros((MLA_HEADS, MLA_KVR, BR), F32)
    eye = jnp.eye(ROPE_HALF, dtype=F32)
    for h in range(MLA_HEADS):
        wcat = wcat.at[h * MLA_NOPE:(h + 1) * MLA_NOPE, h * MLA_QW:h * MLA_QW + MLA_KVR].set(P["mla_wuk"][li][:, h, :].T)
        for half in range(2):
            r0 = n_nope + half * (LANES // 2) + h * ROPE_HALF
            c0 = h * MLA_QW + MLA_KVR + half * ROPE_HALF
            wcat = wcat.at[r0:r0 + ROPE_HALF, c0:c0 + ROPE_HALF].set(eye)
        wuv = wuv.at[h, :, h * MLA_V:(h + 1) * MLA_V].set(P["mla_wuv"][li][:, h, :])
    lp["mla_wcat_t"] = wcat.T.astype(BF16)
    lp["mla_wuv"] = wuv.astype(BF16)
    lp["mla_wuv_t"] = wuv.transpose(0, 2, 1).astype(BF16)
    lp["w_mem_kv"] = jnp.concatenate([P["w_mem_k"][li], P["w_mem_v"][li]], axis=1).astype(BF16)
    wb = P["w_branch"][li]
    wb_diff = wb[1].reshape(DIFF_KV, grp, 2 * DIFF_DIM, d_model).swapaxes(0, 1).reshape(BR, d_model)
    lp["w_branch"] = wb.at[1].set(wb_diff).astype(BF16)
    lp["w_out"] = P["w_out"][li].astype(BF16)
    lp["w_ffn_in"] = P["w_ffn_in"][li].astype(BF16)
    lp["w_ffn_out"] = P["w_ffn_out"][li].astype(BF16)
    return lp


def _row_tile(n, want):
    return want if n % want == 0 else n


def _layer(x, lp, li, db, steps, st, past):
    n = x.shape[0]
    tm = _row_tile(n, 256)
    prompt = past is None
    (z, xbc, dt, dq, dk, dv, gq, gk, gv, gg, gr, cq, ckv, kr, mq) = _proj_in(
        x, lp["g_pre_mix"], lp["w_cat"], _row_tile(n, 512), transposed=(3,) if prompt else ())
    y_ssd, ssd_conv, ssd_h = _ssd_branch(z, xbc, dt, st["ssd_conv"], st["ssd_h"], lp, db, steps)
    y_gla, gla_s = _gla_branch(gq, gk, gv, gg, gr, st["gla_s"], lp, db, steps)
    past_len = 0 if prompt else past["page_table"].shape[1] * PAGE
    pos = jnp.repeat(past_len + jnp.arange(steps, dtype=jnp.int32), db)
    cos, sin = _rope_tables(pos)
    qcat, kcat, c_rows, kr_rows = _mla_prep(cq, ckv, kr, cos, sin, lp, tm, q_transposed=prompt)
    if prompt:
        tk = _row_tile(n, 512)
        tq_diff, tq_mla = _row_tile(n, 512), _row_tile(n, 1024)
        gcol = jnp.broadcast_to(lp["diff_prm"][0][:, None], (LANES, tq_diff))
        lam_row = jnp.broadcast_to(lp["diff_prm"][1, 0], (1, tq_diff))
        y_diff = _diff_flash(dq, _diff_keys_alibi(dk, pos), _value_tiles_t(dv.astype(BF16), tk),
                             lam_row, gcol, tq_diff, tk)
        kb = kcat.astype(BF16)
        y_mla = _mla_flash(qcat, kb, _value_tiles_t(kb[:, :MLA_KVR], tk), lp["mla_wuv_t"], tq_mla, tk)
        y_mem = _mem_attn(mq.reshape(n // tm, tm, BR), st["mem_k"], st["mem_v"], key_minor=False).reshape(n, BR)
    else:
        seq = lambda a: _to_seq_major(a, db, steps, SUBLANES)
        pt = past["page_table"]
        g_pages = math.gcd(pt.shape[1], 64)
        y_diff = _to_time_major(_diff_decode(pt, seq(dq), seq(dk), seq(dv), lp["diff_prm"], past["diff_k"],
                                             past["diff_v"], li, steps, g_pages), steps)
        y_mla = _to_time_major(_mla_decode(pt, seq(qcat), seq(kcat), lp["mla_wuv"], past["mla_ckv"],
                                           past["mla_kr"], li, steps, g_pages), steps)
        y_mem = _to_time_major(_mem_attn(seq(mq), st["mem_k"], st["mem_v"], key_minor=True), steps)
    x = _merge_out(x, (y_ssd, y_diff, y_gla, y_mla, y_mem), lp, tm, transposed=(1, 3) if prompt else ())
    x, ffn_conv = _ffn(x, st["ffn_conv"], lp, shift=db, tm=_row_tile(n, 512) if db == 1 else n)
    return x, (dk, dv, c_rows, kr_rows, ssd_h, ssd_conv, gla_s, ffn_conv)


def kernel(x_prompt, x_sample, cache_diff_k, cache_diff_v, cache_mla_ckv, cache_mla_krope, cache_mem_k, cache_mem_v, state_ssd, state_ssd_conv, state_gla, state_ffn_conv, page_table, mem_prompt, w_in, ssd_conv_w, ssd_conv_b, ssd_dt_bias, ssd_a_log, ssd_d, ssd_norm, diff_lq1, diff_lk1, diff_lq2, diff_lk2, diff_norm, gla_wg2, gla_bg, gla_norm, mla_q_norm, mla_kv_norm, mla_wuq, mla_wuk, mla_wuv, w_mem_k, w_mem_v, w_branch, w_out, g_pre_mix, g_post_mix, g_pre_ffn, g_post_ffn, w_ffn_in, ffn_conv_w, ffn_conv_b, w_ffn_out):
    P = dict(w_in=w_in, ssd_conv_w=ssd_conv_w, ssd_conv_b=ssd_conv_b, ssd_dt_bias=ssd_dt_bias,
             ssd_a_log=ssd_a_log, ssd_d=ssd_d, ssd_norm=ssd_norm, diff_lq1=diff_lq1, diff_lk1=diff_lk1,
             diff_lq2=diff_lq2, diff_lk2=diff_lk2, diff_norm=diff_norm, gla_wg2=gla_wg2, gla_bg=gla_bg,
             gla_norm=gla_norm, mla_q_norm=mla_q_norm, mla_kv_norm=mla_kv_norm, mla_wuq=mla_wuq,
             mla_wuk=mla_wuk, mla_wuv=mla_wuv, w_mem_k=w_mem_k, w_mem_v=w_mem_v, w_branch=w_branch,
             w_out=w_out, g_pre_mix=g_pre_mix, g_post_mix=g_post_mix, g_pre_ffn=g_pre_ffn,
             g_post_ffn=g_post_ffn, w_ffn_in=w_ffn_in, ffn_conv_w=ffn_conv_w, ffn_conv_b=ffn_conv_b,
             w_ffn_out=w_ffn_out)
    depth = w_in.shape[0]
    pb, seq_len, d_model = x_prompt.shape
    assert pb == 1
    db, steps, _ = x_sample.shape
    mem_len = mem_prompt.shape[1]
    ffn_dim = ffn_conv_w.shape[2]
    past = {"page_table": page_table,
            "diff_k": jnp.transpose(cache_diff_k, (0, 1, 3, 4, 2)).reshape(depth, -1, LANES, PAGE),
            "diff_v": jnp.transpose(cache_diff_v, (0, 1, 3, 4, 2)).reshape(depth, -1, LANES, PAGE),
            "mla_ckv": cache_mla_ckv,
            "mla_kr": jnp.transpose(cache_mla_krope, (0, 1, 3, 2))}
    mem_kt = jnp.transpose(cache_mem_k, (0, 1, 3, 4, 2)).reshape(depth, db, BR, mem_len)
    mem_vt = jnp.transpose(cache_mem_v, (0, 1, 3, 4, 2)).reshape(depth, db, BR, mem_len)
    xp = x_prompt[0]
    xs = x_sample.transpose(1, 0, 2).reshape(steps * db, d_model)
    new_p, new_s, mem_kp, mem_vp = [], [], [], []
    for li in range(depth):
        lp = _prep_layer(li, P)
        mkv = _matmul(mem_prompt[0], lp["w_mem_kv"])
        mk, mv = mkv[:, :BR], mkv[:, BR:]
        mem_kp.append(mk.reshape(1, mem_len, MEM_HEADS, MEM_DIM))
        mem_vp.append(mv.reshape(1, mem_len, MEM_HEADS, MEM_DIM))
        st_p = {"ssd_conv": jnp.zeros((SUBLANES, SSD_CONV_DIM), F32),
                "ssd_h": jnp.zeros((1, SSD_HEADS, SSD_HEAD_DIM, SSD_STATE), F32),
                "gla_s": jnp.zeros((1, GLA_HEADS, GLA_DK, GLA_DV), F32),
                "ffn_conv": jnp.zeros((SUBLANES, ffn_dim), F32),
                "mem_k": mk[None], "mem_v": mv[None]}
        xp, o = _layer(xp, lp, li, 1, seq_len, st_p, None)
        dk, dv, c_rows, kr_rows, ssd_h, ssd_conv, gla_s, ffn_conv = o
        new_p.append((dk.reshape(1, seq_len, DIFF_KV, 2 * DIFF_DIM), dv.reshape(1, seq_len, DIFF_KV, 2 * DIFF_DIM),
                      c_rows[None], kr_rows[None], ssd_h, ssd_conv[None, SUBLANES - (SSD_CONV - 1):],
                      gla_s, ffn_conv[None, SUBLANES - (FFN_CONV - 1):]))
        st_s = {"ssd_conv": state_ssd_conv[li].transpose(1, 0, 2).reshape((SSD_CONV - 1) * db, SSD_CONV_DIM),
                "ssd_h": state_ssd[li],
                "gla_s": state_gla[li],
                "ffn_conv": state_ffn_conv[li].transpose(1, 0, 2).reshape((FFN_CONV - 1) * db, ffn_dim),
                "mem_k": mem_kt[li], "mem_v": mem_vt[li]}
        xs, o = _layer(xs, lp, li, db, steps, st_s, past)
        dk, dv, c_rows, kr_rows, ssd_h, ssd_conv, gla_s, ffn_conv = o
        bm = lambda a: a.reshape(-1, db, a.shape[-1]).transpose(1, 0, 2)
        new_s.append((bm(dk).reshape(db, steps, DIFF_KV, 2 * DIFF_DIM), bm(dv).reshape(db, steps, DIFF_KV, 2 * DIFF_DIM),
                      bm(c_rows), bm(kr_rows), ssd_h, bm(ssd_conv), gla_s, bm(ffn_conv)))
    stk = lambda outs, j: jnp.stack([o[j] for o in outs])
    yp = xp[None]
    ys = xs.reshape(steps, db, d_model).transpose(1, 0, 2)
    return (yp, ys,
            stk(new_p, 0), stk(new_p, 1), stk(new_p, 2), stk(new_p, 3),
            jnp.stack(mem_kp), jnp.stack(mem_vp),
            stk(new_p, 4), stk(new_p, 5), stk(new_p, 6), stk(new_p, 7),
            stk(new_s, 0), stk(new_s, 1), stk(new_s, 2), stk(new_s, 3),
            stk(new_s, 4), stk(new_s, 5), stk(new_s, 6), stk(new_s, 7))
```

```python
import functools
import math

import numpy as np
import jax
import jax.numpy as jnp
from jax import lax
from jax.experimental import pallas as pl
from jax.experimental.pallas import tpu as pltpu

F32 = jnp.float32
BF16 = jnp.bfloat16

BR = 256
N_BRANCH = 5
SSD_HEADS, SSD_HEAD_DIM, SSD_GROUPS, SSD_STATE, SSD_CONV = 4, 64, 2, 128, 4
SSD_CONV_DIM = BR + 2 * SSD_GROUPS * SSD_STATE
DIFF_HEADS, DIFF_KV, DIFF_DIM = 4, 2, 32
GLA_HEADS, GLA_DK, GLA_DV, GLA_RANK, GLA_TAU = 4, 32, 64, 16, 16.0
MLA_HEADS, MLA_QR, MLA_KVR, MLA_NOPE, MLA_ROPE, MLA_V = 4, 256, 128, 64, 32, 64
ROPE_THETA = 10000.0
MEM_HEADS, MEM_DIM = 4, 64
FFN_CONV = 3
PAGE = 128
EPS = 1e-6
NEG = -1e30

IN_WIDTHS = (BR, SSD_CONV_DIM, SSD_HEADS, DIFF_HEADS * 2 * DIFF_DIM, DIFF_KV * 2 * DIFF_DIM,
             DIFF_KV * 2 * DIFF_DIM, GLA_HEADS * GLA_DK, GLA_HEADS * GLA_DK, GLA_HEADS * GLA_DV,
             GLA_RANK, GLA_HEADS * GLA_DV, MLA_QR, MLA_KVR, MLA_ROPE, MEM_HEADS * MEM_DIM)
PROJ_WIDTHS = (256, 768, 128, 256, 128, 128, 128, 128, 256, 128, 256, 256, 128, 128, 256)

LANES = 128
SUBLANES = 8
VMEM_LIMIT = 56 * 1024 * 1024


def _cparams(sem):
    return pltpu.CompilerParams(dimension_semantics=sem, vmem_limit_bytes=VMEM_LIMIT)


def _dot(a, b):
    return jnp.dot(a, b, preferred_element_type=F32)


def _dot_nt(a, b):
    return lax.dot_general(a, b, (((1,), (1,)), ((), ())), preferred_element_type=F32)


def _dot_tn(a, b):
    return lax.dot_general(a, b, (((0,), (0,)), ((), ())), preferred_element_type=F32)


def _split3(x):
    hi = x.astype(BF16)
    r = x - hi.astype(F32)
    mid = r.astype(BF16)
    lo = (r - mid.astype(F32)).astype(BF16)
    return hi, mid, lo


def _xdot_l(m01, x):
    hi, mid, lo = _split3(x)
    return _dot(m01, hi) + _dot(m01, mid) + _dot(m01, lo)


def _xdot_r(x, m01):
    hi, mid, lo = _split3(x)
    return _dot(hi, m01) + _dot(mid, m01) + _dot(lo, m01)


def _xdot_nt(m01, x):
    hi, mid, lo = _split3(x)
    return _dot_nt(m01, hi) + _dot_nt(m01, mid) + _dot_nt(m01, lo)


def _rms(x, g):
    return x * lax.rsqrt(jnp.mean(x * x, axis=-1, keepdims=True) + EPS) * g


def _silu(x):
    return x * (1.0 / (1.0 + jnp.exp(-x)))


def _sigmoid(x):
    return 1.0 / (1.0 + jnp.exp(-x))


def _softplus(x):
    return jnp.maximum(x, 0.0) + jnp.log(1.0 + jnp.exp(-jnp.abs(x)))


def _log_sigmoid(x):
    return -_softplus(-x)


def _gelu_tanh(x):
    c = math.sqrt(2.0 / math.pi)
    return 0.5 * x * (1.0 + jnp.tanh(c * (x + 0.044715 * (x * x * x))))


def _lane_iota(shape):
    return lax.broadcasted_iota(jnp.int32, shape, len(shape) - 1)


def _row_iota(shape):
    return lax.broadcasted_iota(jnp.int32, shape, len(shape) - 2)


DIFF_K_SEG, DIFF_V_SEG = 4, 5
VALUE_EXTRA_ROWS = 16


def _value_tile(vt):
    extra = jnp.where(_row_iota((VALUE_EXTRA_ROWS, vt.shape[1])) == 0, 1.0, 0.0)
    return jnp.concatenate([vt, extra], axis=0).astype(BF16)


def _proj_in_kernel(x_ref, g_ref, w_ref, *o_refs, transposed, attn_feed):
    tm = x_ref.shape[0]
    xn = _rms(x_ref[...], g_ref[...]).astype(BF16)
    off = 0
    for i, (o_ref, wd) in enumerate(zip(o_refs, PROJ_WIDTHS)):
        w = w_ref[off:off + wd, :]
        res = _dot_nt(w, xn) if i in transposed else _dot_nt(xn, w)
        o_ref[...] = res
        if attn_feed and i == DIFF_K_SEG:
            pos = pl.program_id(0) * tm + _row_iota((tm, LANES))
            lane = _lane_iota((tm, LANES))
            cols = jnp.where(lane < 3, pos // POS_SPLIT, jnp.where(lane < 6, pos % POS_SPLIT, 0)).astype(F32)
            o_refs[-2][...] = jnp.concatenate([res, cols], axis=1).astype(BF16)
        if attn_feed and i == DIFF_V_SEG:
            o_refs[-1][0] = _value_tile(_dot_nt(w, xn))
        off += wd


def _proj_in(x, g, w, tm, transposed=(), attn_feed=False):
    n, d = x.shape
    spec = lambda i, wd: (pl.BlockSpec((wd, tm), lambda r: (0, r)) if i in transposed
                          else pl.BlockSpec((tm, wd), lambda r: (r, 0)))
    shape = lambda i, wd: jax.ShapeDtypeStruct((wd, n) if i in transposed else (n, wd), F32)
    out_specs = [spec(i, wd) for i, wd in enumerate(PROJ_WIDTHS)]
    out_shape = [shape(i, wd) for i, wd in enumerate(PROJ_WIDTHS)]
    if attn_feed:
        assert n <= POS_SPLIT * 256
        rows = LANES + VALUE_EXTRA_ROWS
        out_specs += [pl.BlockSpec((tm, 2 * LANES), lambda r: (r, 0)), pl.BlockSpec((1, rows, tm), lambda r: (r, 0, 0))]
        out_shape += [jax.ShapeDtypeStruct((n, 2 * LANES), BF16), jax.ShapeDtypeStruct((n // tm, rows, tm), BF16)]
    return pl.pallas_call(
        functools.partial(_proj_in_kernel, transposed=transposed, attn_feed=attn_feed),
        grid=(n // tm,),
        in_specs=[pl.BlockSpec((tm, d), lambda i: (i, 0)),
                  pl.BlockSpec((1, d), lambda i: (0, 0)),
                  _resident(w.shape)],
        out_specs=out_specs,
        out_shape=out_shape,
        compiler_params=_cparams(("arbitrary",)),
        name="proj_in",
    )(x, g, w)


def _conv_taps(ext_ref, cur, w_ref, b_ref, taps, shift, pad, tm):
    acc = b_ref[...] + w_ref[taps - 1:taps, :] * cur
    for j in range(taps - 1):
        off = pad - (taps - 1 - j) * shift
        acc = acc + w_ref[j:j + 1, :] * ext_ref[off:off + tm, :]
    return acc


def _conv_silu_kernel(x_ref, st_ref, w_ref, b_ref, y_ref, ns_ref, ext_ref, *, taps, shift, pad):
    tm = x_ref.shape[0]

    @pl.when(pl.program_id(0) == 0)
    def _():
        ext_ref[0:pad, :] = st_ref[...]

    cur = x_ref[...]
    ext_ref[pad:pad + tm, :] = cur
    y_ref[...] = _silu(_conv_taps(ext_ref, cur, w_ref, b_ref, taps, shift, pad, tm))
    tail = ext_ref[tm:tm + pad, :]
    ns_ref[...] = tail
    ext_ref[0:pad, :] = tail


def _conv_silu(x, state, w, b, shift, tm):
    n, c = x.shape
    taps = w.shape[0]
    pad = state.shape[0]
    assert tm >= pad and pad >= (taps - 1) * shift
    return pl.pallas_call(
        functools.partial(_conv_silu_kernel, taps=taps, shift=shift, pad=pad),
        grid=(n // tm,),
        in_specs=[pl.BlockSpec((tm, c), lambda i: (i, 0)),
                  pl.BlockSpec((pad, c), lambda i: (0, 0)),
                  pl.BlockSpec((taps, c), lambda i: (0, 0)),
                  pl.BlockSpec((1, c), lambda i: (0, 0))],
        out_specs=[pl.BlockSpec((tm, c), lambda i: (i, 0)),
                   pl.BlockSpec((pad, c), lambda i: (0, 0))],
        out_shape=[jax.ShapeDtypeStruct((n, c), F32), jax.ShapeDtypeStruct((pad, c), F32)],
        scratch_shapes=[pltpu.VMEM((pad + tm, c), F32)],
        compiler_params=_cparams(("arbitrary",)),
        name="conv_silu",
    )(x, state, w, b)


def _ssd_kernel(z_ref, xbc_ref, dt_ref, h0_ref, prm_ref, dskip_ref, ng_ref, tri_ref, exp_ref, eye_ref,
                y_ref, hout_ref, st_ref, *, chunk, t_valid, n_chunks):
    for j in range(z_ref.shape[0]):
        _ssd_chunk(j, z_ref, xbc_ref, dt_ref, h0_ref, prm_ref, dskip_ref, ng_ref, tri_ref, exp_ref, eye_ref,
                   y_ref, hout_ref, st_ref, chunk, t_valid, n_chunks)


def _when(pred, always):
    return (lambda fn: fn()) if always else pl.when(pred)


def _ssd_chunk(j, z_ref, xbc_ref, dt_ref, h0_ref, prm_ref, dskip_ref, ng_ref, tri_ref, exp_ref, eye_ref,
               y_ref, hout_ref, st_ref, chunk, t_valid, n_chunks):
    L = chunk
    c = pl.program_id(1)

    @_when(c == 0, n_chunks == 1)
    def _():
        st_ref[j] = h0_ref[j].reshape(SSD_HEADS * SSD_HEAD_DIM, SSD_STATE).T

    xbc = xbc_ref[j]
    xs = xbc[:, 0:BR]
    bm = xbc[:, BR:2 * BR].astype(BF16)
    cm = xbc[:, 2 * BR:3 * BR].astype(BF16)
    dt = _softplus(dt_ref[j] + prm_ref[0:1, :])
    if t_valid < L:
        dt = jnp.where(_row_iota((L, LANES)) < t_valid, dt, 0.0)
    d_a = dt * prm_ref[1:2, :]
    tri = tri_ref[...]
    expand = exp_ref[...]
    cs = _xdot_l(tri, d_a)
    cs_exp = _xdot_r(cs, expand)
    dt_exp = _xdot_r(dt, expand)
    cs_t = _xdot_nt(eye_ref[...], cs)
    xdt = xs * dt_exp
    xdt_b = xdt.astype(BF16)
    causal = _row_iota((L, L)) >= _lane_iota((L, L))
    lane_head = _lane_iota((L, BR)) // SSD_HEAD_DIM
    st = st_ref[j]
    st_b = st.astype(BF16)
    y = xs * dskip_ref[...]
    for g in range(SSD_GROUPS):
        cg = cm[:, g * SSD_STATE:(g + 1) * SSD_STATE]
        bg = bm[:, g * SSD_STATE:(g + 1) * SSD_STATE]
        cb = _dot_nt(cg, bg)
        for hh in range(SSD_HEADS // SSD_GROUPS):
            h = g * (SSD_HEADS // SSD_GROUPS) + hh
            dec = jnp.exp(jnp.where(causal, cs[:, h:h + 1] - cs_t[h:h + 1, :], NEG))
            yd = _dot((cb * dec).astype(BF16), xdt_b)
            y = y + jnp.where(lane_head == h, yd, 0.0)
    half = _lane_iota((L, BR)) < (BR // 2)
    y_off = jnp.where(half, _dot(cm[:, 0:SSD_STATE], st_b), _dot(cm[:, SSD_STATE:2 * SSD_STATE], st_b))
    y = y + y_off * jnp.exp(cs_exp)
    cs_last = cs_exp[L - 1:L, :]
    xw = (xdt * jnp.exp(cs_last - cs_exp)).astype(BF16)
    half_s = _lane_iota((SSD_STATE, BR)) < (BR // 2)
    upd = jnp.where(half_s, _dot_tn(bm[:, 0:SSD_STATE], xw), _dot_tn(bm[:, SSD_STATE:2 * SSD_STATE], xw))
    st_new = st * jnp.exp(cs_last) + upd
    st_ref[j] = st_new
    y_ref[j] = _rms(y * _silu(z_ref[j]), ng_ref[...])

    @_when(c == n_chunks - 1, n_chunks == 1)
    def _():
        hout_ref[j] = st_new.T.reshape(SSD_HEADS, SSD_HEAD_DIM, SSD_STATE)


SCAN_SEQS = 8


def _ssd_scan(z, xbc, dt_raw, h0, prm, dskip, ng, chunk, t_valid):
    b, t, _ = z.shape
    n_chunks = t // chunk
    sb = math.gcd(b, SCAN_SEQS)
    tri = jnp.asarray(np.tril(np.ones((chunk, chunk), np.float32)), BF16)
    expand = np.zeros((LANES, BR), np.float32)
    for h in range(SSD_HEADS):
        expand[h, h * SSD_HEAD_DIM:(h + 1) * SSD_HEAD_DIM] = 1.0
    eye = np.eye(SUBLANES, LANES, dtype=np.float32)
    row = lambda bi, ci: (bi, ci, 0)
    fixed2 = lambda bi, ci: (0, 0)
    return pl.pallas_call(
        functools.partial(_ssd_kernel, chunk=chunk, t_valid=t_valid, n_chunks=n_chunks),
        grid=(b // sb, n_chunks),
        in_specs=[pl.BlockSpec((sb, chunk, BR), row),
                  pl.BlockSpec((sb, chunk, SSD_CONV_DIM), row),
                  pl.BlockSpec((sb, chunk, LANES), row),
                  pl.BlockSpec((sb, SSD_HEADS, SSD_HEAD_DIM, SSD_STATE), lambda bi, ci: (bi, 0, 0, 0)),
                  pl.BlockSpec((SUBLANES, LANES), fixed2),
                  pl.BlockSpec((1, BR), fixed2),
                  pl.BlockSpec((1, BR), fixed2),
                  pl.BlockSpec((chunk, chunk), fixed2),
                  pl.BlockSpec((LANES, BR), fixed2),
                  pl.BlockSpec((SUBLANES, LANES), fixed2)],
        out_specs=[pl.BlockSpec((sb, chunk, BR), row),
                   pl.BlockSpec((sb, SSD_HEADS, SSD_HEAD_DIM, SSD_STATE), lambda bi, ci: (bi, 0, 0, 0))],
        out_shape=[jax.ShapeDtypeStruct((b, t, BR), F32),
                   jax.ShapeDtypeStruct((b, SSD_HEADS, SSD_HEAD_DIM, SSD_STATE), F32)],
        scratch_shapes=[pltpu.VMEM((sb, SSD_STATE, BR), F32)],
        compiler_params=_cparams(("arbitrary", "arbitrary")),
        name="ssd_scan",
    )(z, xbc, dt_raw, h0, prm, dskip, ng, tri, jnp.asarray(expand, BF16), jnp.asarray(eye, BF16))


def _pad_lanes(v, width):
    return jnp.pad(v, [(0, 0)] * (v.ndim - 1) + [(0, width - v.shape[-1])])


def _to_seq_major(a, db, steps, pad_to):
    c = a.shape[-1]
    a = a.reshape(steps, db, c).transpose(1, 0, 2)
    return jnp.pad(a, ((0, 0), (0, pad_to - steps), (0, 0)))


def _to_time_major(a, steps):
    db, _, c = a.shape
    return a[:, :steps].transpose(1, 0, 2).reshape(steps * db, c)


def _ssd_branch(z, xbc_raw, dt_raw, conv_state, h0, lp, db, steps):
    n = z.shape[0]
    tm = min(n, 512)
    xbc, conv_new = _conv_silu(xbc_raw, conv_state, lp["ssd_conv_w"], lp["ssd_conv_b"], shift=db, tm=tm)
    if db == 1:
        chunk = min(steps, 128)
        y, h_new = _ssd_scan(z[None], xbc[None], dt_raw[None], h0, lp["ssd_prm"], lp["ssd_dskip"],
                             lp["ssd_norm"], chunk, chunk)
        return y[0], conv_new, h_new
    zs, xs, ds = (_to_seq_major(a, db, steps, SUBLANES) for a in (z, xbc, dt_raw))
    y, h_new = _ssd_scan(zs, xs, ds, h0, lp["ssd_prm"], lp["ssd_dskip"], lp["ssd_norm"], SUBLANES, steps)
    return _to_time_major(y, steps), conv_new, h_new


def _gla_consts(chunk):
    L = chunk
    nl = int(math.log2(L))
    assert 2 ** nl == L
    t = np.arange(L)
    mats, masks = [], []
    for lvl in range(nl):
        b = L >> (lvl + 1)
        same = (t[:, None] // b) == (t[None, :] // b)
        mats.append((same & (t[None, :] <= t[:, None])).astype(np.float32))
    for lvl in range(nl):
        b = L >> (lvl + 1)
        same = (t[:, None] // b) == (t[None, :] // b)
        mats.append((same & (t[None, :] > t[:, None])).astype(np.float32))
    mats.append((t[None, :] <= t[:, None]).astype(np.float32))
    mats.append((t[None, :] > t[:, None]).astype(np.float32))
    for lvl in range(nl):
        b = L >> (lvl + 1)
        tb, sb = t[:, None] // b, t[None, :] // b
        masks.append(((tb % 2 == 1) & (sb == tb - 1)).astype(np.float32))
    masks.append((t[:, None] == t[None, :]).astype(np.float32))
    masks = np.stack([np.tile(m, (GLA_HEADS, 1)) for m in masks])
    ones = np.zeros((BR, BR), np.float32)
    for h in range(GLA_HEADS):
        ones[h * GLA_DV:(h + 1) * GLA_DV, h * GLA_DV:(h + 1) * GLA_DV] = 1.0
    return (jnp.asarray(np.concatenate(mats, 0), BF16), jnp.asarray(masks, F32), jnp.asarray(ones, BF16), nl)


def _gla_kernel(q_ref, k_ref, v_ref, gg_ref, r_ref, s0_ref, wg2_ref, bg_ref, ng_ref, mall_ref, masks_ref,
                ones_ref, y_ref, sout_ref, st_ref, *, chunk, t_valid, n_chunks, nl):
    for j in range(q_ref.shape[0]):
        _gla_chunk(j, q_ref, k_ref, v_ref, gg_ref, r_ref, s0_ref, wg2_ref, bg_ref, ng_ref, mall_ref, masks_ref,
                   ones_ref, y_ref, sout_ref, st_ref, chunk, t_valid, n_chunks, nl)


def _gla_chunk(j, q_ref, k_ref, v_ref, gg_ref, r_ref, s0_ref, wg2_ref, bg_ref, ng_ref, mall_ref, masks_ref,
               ones_ref, y_ref, sout_ref, st_ref, chunk, t_valid, n_chunks, nl):
    L = chunk
    c = pl.program_id(1)

    @_when(c == 0, n_chunks == 1)
    def _():
        st_ref[j] = s0_ref[j]

    q = q_ref[j] * (GLA_DK ** -0.5)
    k = k_ref[j]
    g = _log_sigmoid(_dot(gg_ref[j].astype(BF16), wg2_ref[...]) + bg_ref[...]) * (1.0 / GLA_TAU)
    if t_valid < L:
        live = _row_iota((L, LANES)) < t_valid
        g = jnp.where(live, g, 0.0)
        k = jnp.where(live, k, 0.0)
    v_b = v_ref[j].astype(BF16)
    gsum = _xdot_l(mall_ref[...], g)
    lane_head = _lane_iota((L, LANES)) // GLA_DK

    def heads_on_rows(x):
        return jnp.concatenate([jnp.where(lane_head == h, x, 0.0) for h in range(GLA_HEADS)], axis=0).astype(BF16)

    att = masks_ref[nl] * _dot_nt(heads_on_rows(q), k.astype(BF16))
    for lvl in range(nl):
        qt = q * jnp.exp(gsum[lvl * L:(lvl + 1) * L])
        kt = k * jnp.exp(gsum[(nl + lvl) * L:(nl + lvl + 1) * L])
        att = att + masks_ref[lvl] * _dot_nt(heads_on_rows(qt), kt.astype(BF16))
    bc = gsum[2 * nl * L:(2 * nl + 1) * L]
    tail = gsum[(2 * nl + 1) * L:(2 * nl + 2) * L]
    st = st_ref[j]
    res = _dot(att.astype(BF16), v_b) + _dot_nt(heads_on_rows(q * jnp.exp(bc)), st.astype(BF16))
    out_head = _lane_iota((L, BR)) // GLA_DV
    o = jnp.zeros((L, BR), F32)
    for h in range(GLA_HEADS):
        o = o + jnp.where(out_head == h, res[h * L:(h + 1) * L], 0.0)
    st_new = st * jnp.exp(bc[L - 1:L, :]) + _dot_tn(v_b, (k * jnp.exp(tail)).astype(BF16))
    st_ref[j] = st_new
    msq = _xdot_r(o * o, ones_ref[...]) * (1.0 / GLA_DV)
    y_ref[j] = o * lax.rsqrt(msq + EPS) * ng_ref[...] * _silu(r_ref[j])

    @_when(c == n_chunks - 1, n_chunks == 1)
    def _():
        sout_ref[j] = st_new


def _gla_scan(q, k, v, gg, r, s0, wg2, bg, ng, chunk, t_valid):
    b, t, _ = q.shape
    n_chunks = t // chunk
    mall, masks, ones, nl = _gla_consts(chunk)
    sb = math.gcd(b, SCAN_SEQS)
    row = lambda bi, ci: (bi, ci, 0)
    fixed2 = lambda bi, ci: (0, 0)
    return pl.pallas_call(
        functools.partial(_gla_kernel, chunk=chunk, t_valid=t_valid, n_chunks=n_chunks, nl=nl),
        grid=(b // sb, n_chunks),
        in_specs=[pl.BlockSpec((sb, chunk, LANES), row),
                  pl.BlockSpec((sb, chunk, LANES), row),
                  pl.BlockSpec((sb, chunk, BR), row),
                  pl.BlockSpec((sb, chunk, LANES), row),
                  pl.BlockSpec((sb, chunk, BR), row),
                  pl.BlockSpec((sb, BR, LANES), lambda bi, ci: (bi, 0, 0)),
                  pl.BlockSpec((LANES, LANES), fixed2),
                  pl.BlockSpec((1, LANES), fixed2),
                  pl.BlockSpec((1, BR), fixed2),
                  pl.BlockSpec(mall.shape, fixed2),
                  pl.BlockSpec(masks.shape, lambda bi, ci: (0, 0, 0)),
                  pl.BlockSpec((BR, BR), fixed2)],
        out_specs=[pl.BlockSpec((sb, chunk, BR), row),
                   pl.BlockSpec((sb, BR, LANES), lambda bi, ci: (bi, 0, 0))],
        out_shape=[jax.ShapeDtypeStruct((b, t, BR), F32), jax.ShapeDtypeStruct((b, BR, LANES), F32)],
        scratch_shapes=[pltpu.VMEM((sb, BR, LANES), F32)],
        compiler_params=_cparams(("arbitrary", "arbitrary")),
        name="gla_scan",
    )(q, k, v, gg, r, s0, wg2, bg, ng, mall, masks, ones)


def _gla_state_in(s):
    blocks = [jnp.pad(s[:, h].transpose(0, 2, 1), ((0, 0), (0, 0), (h * GLA_DK, LANES - (h + 1) * GLA_DK)))
              for h in range(GLA_HEADS)]
    return jnp.concatenate(blocks, axis=1)


def _gla_state_out(st):
    return jnp.stack([st[:, h * GLA_DV:(h + 1) * GLA_DV, h * GLA_DK:(h + 1) * GLA_DK].transpose(0, 2, 1)
                      for h in range(GLA_HEADS)], axis=1)


def _gla_branch(q, k, v, gg, r, s0, lp, db, steps):
    st0 = _gla_state_in(s0)
    args = (lp["gla_wg2"], lp["gla_bg"], lp["gla_norm"])
    if db == 1:
        chunk = min(steps, 128)
        y, st = _gla_scan(q[None], k[None], v[None], gg[None], r[None], st0, *args, chunk, chunk)
        return y[0], _gla_state_out(st)
    qs, ks, vs, gs, rs = (_to_seq_major(a, db, steps, SUBLANES) for a in (q, k, v, gg, r))
    y, st = _gla_scan(qs, ks, vs, gs, rs, st0, *args, SUBLANES, steps)
    return _to_time_major(y, steps), _gla_state_out(st)


DIFF_MAPS = DIFF_HEADS * 2
POS_SPLIT = 128


def _diff_stack_q(q, rows):
    lane = _lane_iota((rows, LANES))
    blocks = []
    for kv in range(DIFF_KV):
        for g in range(DIFF_HEADS // DIFF_KV):
            qg = q[:, g * LANES:(g + 1) * LANES]
            for i in range(2):
                lo = kv * 2 * DIFF_DIM + i * DIFF_DIM
                blocks.append(jnp.where((lane >= lo) & (lane < lo + DIFF_DIM), qg, 0.0))
    return jnp.concatenate(blocks, axis=0).astype(BF16)


LOG2E = math.log2(math.e)
DIFF_QSCALE = DIFF_DIM ** -0.5 * LOG2E


def _diff_slope(blk):
    head = blk // 2
    return 2.0 ** (-8.0 * (head + 1) / DIFF_HEADS) * LOG2E


def _bf16_parts(c):
    parts, r = [], np.float32(c)
    for _ in range(3):
        p = np.float32(np.asarray(r).astype(BF16))
        parts.append(float(p))
        r = np.float32(r - p)
    return parts


def _online_update(s, blk, rows, m_ref, l_ref, p_ref):
    r0 = blk * rows
    m_prev = m_ref[r0:r0 + rows, :]
    m_new = jnp.maximum(m_prev, jnp.max(s, axis=1, keepdims=True))
    alpha = jnp.exp2(m_prev - m_new)
    p = jnp.exp2(s - m_new[:, 0:1])
    l_ref[r0:r0 + rows, :] = alpha * l_ref[r0:r0 + rows, :] + jnp.sum(p, axis=1, keepdims=True)
    m_ref[r0:r0 + rows, :] = m_new
    p_ref[r0:r0 + rows, 0:s.shape[1]] = p.astype(p_ref.dtype)
    return alpha


def _diff_finish(acc_ref, l_ref, prm_ref, rows):
    lane = _lane_iota((rows, LANES))
    low = lane < 2 * DIFF_DIM
    lam = prm_ref[1:2, :]
    outs = []
    for g in range(DIFF_HEADS // DIFF_KV):
        parts = []
        for kv in range(DIFF_KV):
            b1 = ((kv * 2 + g) * 2) * rows
            b2 = b1 + rows
            o1 = acc_ref[b1:b1 + rows, :] / l_ref[b1:b1 + rows, :]
            o2 = acc_ref[b2:b2 + rows, :] / l_ref[b2:b2 + rows, :]
            parts.append(o1 - lam * o2)
        og = jnp.where(low, parts[0], parts[1])
        sq = og * og
        ss = jnp.where(low, jnp.sum(jnp.where(low, sq, 0.0), axis=1, keepdims=True),
                       jnp.sum(jnp.where(low, 0.0, sq), axis=1, keepdims=True))
        outs.append(og * lax.rsqrt(ss * (1.0 / (2 * DIFF_DIM)) + EPS) * prm_ref[0:1, :])
    return outs


def _diff_stack_q_alibi(q, rows):
    row = _row_iota((LANES, rows))
    blocks = []
    for kv in range(DIFF_KV):
        for g in range(DIFF_HEADS // DIFF_KV):
            qg = q[g * LANES:(g + 1) * LANES, :]
            for i in range(2):
                lo = kv * 2 * DIFF_DIM + i * DIFF_DIM
                parts = _bf16_parts(_diff_slope((kv * 2 + g) * 2 + i))
                qm = jnp.where((row >= lo) & (row < lo + DIFF_DIM), qg, 0.0)
                aug = jnp.zeros((LANES, rows), F32)
                for j, part in enumerate(parts):
                    aug = jnp.where(row == j, part * POS_SPLIT, jnp.where(row == len(parts) + j, part, aug))
                blocks.append(jnp.concatenate([qm, aug], axis=0))
    return jnp.concatenate(blocks, axis=1).astype(BF16)


FLASH_COLS = 2 * LANES
FLASH_AHEAD = 3


def _flash_tiles(k_ref, vt_ref, qs_ref, m_ref, acc_ref, q0, tq, tk):
    width = qs_ref.shape[1]

    def kv_tile(ki, masked):
        k = k_ref[pl.ds(pl.multiple_of(ki * tk, tk), tk), :]
        vt = vt_ref[ki]
        m_all = m_ref[...]
        ms, accs = [], []
        starts = list(range(0, width, FLASH_COLS))
        pending = [_dot(k, qs_ref[:, c:c + FLASH_COLS]) for c in starts[:FLASH_AHEAD]]
        for idx, c0 in enumerate(starts):
            cols = slice(c0, c0 + FLASH_COLS)
            s = pending.pop(0)
            if idx + FLASH_AHEAD < len(starts):
                c_next = starts[idx + FLASH_AHEAD]
                pending.append(_dot(k, qs_ref[:, c_next:c_next + FLASH_COLS]))
            if masked:
                kpos = ki * tk + _row_iota((tk, FLASH_COLS))
                qpos = q0 + (c0 + _lane_iota((tk, FLASH_COLS))) % tq
                s = jnp.where(kpos <= qpos, s, NEG)
            m_prev = m_all[:, cols]
            m_new = jnp.maximum(m_prev, jnp.max(s, axis=0, keepdims=True))
            p = jnp.exp2(s - m_new).astype(BF16)
            ms.append(m_new)
            accs.append(jnp.exp2(m_prev - m_new) * acc_ref[:, cols] + _dot(vt, p))
        m_ref[...] = jnp.concatenate(ms, axis=1)
        acc_ref[...] = jnp.concatenate(accs, axis=1)

    m_ref[...] = jnp.full(m_ref.shape, NEG, F32)
    acc_ref[...] = jnp.zeros(acc_ref.shape, F32)
    n_full = q0 // tk

    def body(ki, carry):
        kv_tile(ki, False)
        return carry

    lax.fori_loop(0, n_full, body, 0)
    for j in range(max(tq // tk, 1)):
        kv_tile(n_full + j, True)


def _diff_flash_kernel(q_ref, k_ref, vt_ref, lam_ref, gcol_ref, o_ref, qs_ref, m_ref, acc_ref, *, tq, tk):
    q0 = pl.program_id(0) * tq
    qs_ref[...] = _diff_stack_q_alibi(q_ref[...] * DIFF_QSCALE, tq)
    _flash_tiles(k_ref, vt_ref, qs_ref, m_ref, acc_ref, q0, tq, tk)
    row = _row_iota((LANES, tq))
    low = row < 2 * DIFF_DIM
    lam = lam_ref[...]
    for g in range(DIFF_HEADS // DIFF_KV):
        parts = []
        for kv in range(DIFF_KV):
            c1 = ((kv * 2 + g) * 2) * tq
            c2 = c1 + tq
            o1 = acc_ref[0:LANES, c1:c1 + tq] / acc_ref[LANES:LANES + 1, c1:c1 + tq]
            o2 = acc_ref[0:LANES, c2:c2 + tq] / acc_ref[LANES:LANES + 1, c2:c2 + tq]
            parts.append(o1 - lam * o2)
        og = jnp.where(low, parts[0], parts[1])
        sq = og * og
        ss = jnp.where(low, jnp.sum(jnp.where(low, sq, 0.0), axis=0, keepdims=True),
                       jnp.sum(jnp.where(low, 0.0, sq), axis=0, keepdims=True))
        o_ref[g * LANES:(g + 1) * LANES, :] = og * lax.rsqrt(ss * (1.0 / (2 * DIFF_DIM)) + EPS) * gcol_ref[...]


def _resident(shape):
    return pl.BlockSpec(shape, lambda *_: (0,) * len(shape), pipeline_mode=pl.Buffered(1))


def _diff_flash(q_t, k_aug, v_t, lam_row, gcol, tq, tk):
    n = q_t.shape[1]
    assert (tk % tq == 0 or tq % tk == 0) and n % tk == 0 and n % tq == 0 and tq % FLASH_COLS == 0
    width = DIFF_MAPS * tq
    return pl.pallas_call(
        functools.partial(_diff_flash_kernel, tq=tq, tk=tk),
        grid=(n // tq,),
        in_specs=[pl.BlockSpec((BR, tq), lambda qi: (0, qi)),
                  _resident(k_aug.shape),
                  _resident(v_t.shape),
                  _resident((1, tq)),
                  _resident((LANES, tq))],
        out_specs=pl.BlockSpec((BR, tq), lambda qi: (0, qi)),
        out_shape=jax.ShapeDtypeStruct((BR, n), F32),
        scratch_shapes=[pltpu.VMEM((2 * LANES, width), BF16),
                        pltpu.VMEM((1, width), F32),
                        pltpu.VMEM((LANES + VALUE_EXTRA_ROWS, width), F32)],
        compiler_params=_cparams(("arbitrary",)),
        name="diff_flash",
    )(q_t, k_aug, v_t, lam_row, gcol)


def _page_copies(pt_ref, caches, bufs, sems, layer, b, c, slot, g, pages_per_step):
    page = pt_ref[b, c * pages_per_step + g]
    off = pl.multiple_of(g * PAGE, PAGE)
    copies = []
    for a, (cache, buf) in enumerate(zip(caches, bufs)):
        key_minor = buf.shape[2] == pages_per_step * PAGE
        dst = buf.at[slot, :, pl.ds(off, PAGE)] if key_minor else buf.at[slot, pl.ds(off, PAGE), :]
        copies.append(pltpu.make_async_copy(cache.at[layer, page], dst, sems.at[slot, a]))
    return copies


def _stream_pages(pt_ref, caches, bufs, sems, layer, pages_per_step, n_chunks):
    G = pages_per_step
    b, c = pl.program_id(0), pl.program_id(1)
    step = b * n_chunks + c
    last = step + 1 == pl.num_programs(0) * n_chunks
    slot = step % 2
    wrap = c + 1 == n_chunks
    nb = jnp.where(last, b, jnp.where(wrap, b + 1, b))
    nc = jnp.where(last, c, jnp.where(wrap, 0, c + 1))

    def for_pages(fn):
        def body(g, carry):
            fn(g)
            return carry
        lax.fori_loop(0, G, body, 0)

    def start(bb, cc, sl, g):
        for cp in _page_copies(pt_ref, caches, bufs, sems, layer, bb, cc, sl, g, G):
            cp.start()

    def wait(bb, cc, sl, g):
        for cp in _page_copies(pt_ref, caches, bufs, sems, layer, bb, cc, sl, g, G):
            cp.wait()

    @pl.when(step == 0)
    def _():
        for_pages(lambda g: start(b, c, slot, g))

    for g in range(G):
        wait(b, c, slot, g)

    def start_next():
        for g in range(G):
            start(nb, nc, 1 - slot, g)

    def drain():
        @pl.when(last)
        def _():
            for_pages(lambda g: wait(nb, nc, 1 - slot, g))

    return slot, start_next, drain


def _diff_decode_kernel(pt_ref, q_ref, kn_ref, vn_ref, prm_ref, kc_ref, vc_ref, o_ref,
                        kbuf, vbuf, sems, qs_ref, m_ref, l_ref, acc_ref, p_ref,
                        *, layer, pages_per_step, n_chunks, steps, past_len):
    G = pages_per_step
    R8 = SUBLANES
    c = pl.program_id(1)
    slot, start_next, drain = _stream_pages(pt_ref, (kc_ref, vc_ref), (kbuf, vbuf), sems, layer, G, n_chunks)

    @pl.when(c == 0)
    def _():
        qs_ref[...] = _diff_stack_q(q_ref[0] * DIFF_QSCALE, R8)
        m_ref[...] = jnp.full(m_ref.shape, NEG, F32)
        l_ref[...] = jnp.zeros(l_ref.shape, F32)
        acc_ref[...] = jnp.zeros(acc_ref.shape, F32)

    qs = qs_ref[...]
    width = G * PAGE
    start_next()
    s = _dot(qs, kbuf[slot].astype(BF16))
    ndist = ((c * width + _lane_iota((R8, width))) - (past_len + _row_iota((R8, width)))).astype(F32)
    for blk in range(DIFF_MAPS):
        sb = s[blk * R8:(blk + 1) * R8] + _diff_slope(blk) * ndist
        alpha = _online_update(sb, blk, R8, m_ref, l_ref, p_ref)
        acc_ref[blk * R8:(blk + 1) * R8, :] = alpha * acc_ref[blk * R8:(blk + 1) * R8, :]
    acc_ref[...] = acc_ref[...] + _dot_nt(p_ref[...].astype(BF16), vbuf[slot].astype(BF16))
    drain()

    @pl.when(c == n_chunks - 1)
    def _():
        pad = jnp.zeros((PAGE - R8, LANES), F32)
        kn = jnp.concatenate([kn_ref[0], pad], axis=0).astype(BF16)
        vn = jnp.concatenate([vn_ref[0], pad], axis=0).astype(BF16)
        sn = _dot_nt(qs, kn)
        kt = _lane_iota((R8, PAGE))
        qt = _row_iota((R8, PAGE))
        vis = (kt <= qt) & (kt < steps)
        nd = (kt - qt).astype(F32)
        for blk in range(DIFF_MAPS):
            sb = jnp.where(vis, sn[blk * R8:(blk + 1) * R8] + _diff_slope(blk) * nd, NEG)
            alpha = _online_update(sb, blk, R8, m_ref, l_ref, p_ref)
            acc_ref[blk * R8:(blk + 1) * R8, :] = alpha * acc_ref[blk * R8:(blk + 1) * R8, :]
        acc_ref[...] = acc_ref[...] + _dot(p_ref[:, 0:PAGE].astype(BF16), vn)
        outs = _diff_finish(acc_ref, l_ref, prm_ref, R8)
        o_ref[0, :, 0:LANES] = outs[0]
        o_ref[0, :, LANES:2 * LANES] = outs[1]


def _diff_decode(page_table, q, k_new, v_new, prm, kc, vc, layer, steps, pages_per_step):
    db, n_pages = page_table.shape
    G = pages_per_step
    n_chunks = n_pages // G
    rows = DIFF_MAPS * SUBLANES
    seq3 = lambda b, c, pt: (b, 0, 0)
    grid_spec = pltpu.PrefetchScalarGridSpec(
        num_scalar_prefetch=1,
        grid=(db, n_chunks),
        in_specs=[pl.BlockSpec((1, SUBLANES, BR), seq3),
                  pl.BlockSpec((1, SUBLANES, LANES), seq3),
                  pl.BlockSpec((1, SUBLANES, LANES), seq3),
                  pl.BlockSpec((SUBLANES, LANES), lambda b, c, pt: (0, 0)),
                  pl.BlockSpec(memory_space=pl.ANY),
                  pl.BlockSpec(memory_space=pl.ANY)],
        out_specs=pl.BlockSpec((1, SUBLANES, BR), seq3),
        scratch_shapes=[pltpu.VMEM((2, LANES, G * PAGE), F32),
                        pltpu.VMEM((2, LANES, G * PAGE), F32),
                        pltpu.SemaphoreType.DMA((2, 2)),
                        pltpu.VMEM((rows, LANES), BF16),
                        pltpu.VMEM((rows, LANES), F32),
                        pltpu.VMEM((rows, LANES), F32),
                        pltpu.VMEM((rows, LANES), F32),
                        pltpu.VMEM((rows, G * PAGE), F32)])
    return pl.pallas_call(
        functools.partial(_diff_decode_kernel, layer=layer, pages_per_step=G, n_chunks=n_chunks,
                          steps=steps, past_len=n_pages * PAGE),
        grid_spec=grid_spec,
        out_shape=jax.ShapeDtypeStruct((db, SUBLANES, BR), F32),
        compiler_params=_cparams(("arbitrary", "arbitrary")),
        name="diff_decode",
    )(page_table, q, k_new, v_new, prm, kc, vc)


MLA_QW = 2 * LANES
MLA_SCALE = (MLA_NOPE + MLA_ROPE) ** -0.5 * LOG2E
ROPE_HALF = MLA_ROPE // 2


def _mla_prep_kernel(cq_ref, ckv_ref, kr_ref, cos_ref, sin_ref, qn_ref, kvn_ref, wuq_ref, wcat_ref,
                     qcat_ref, kcat_ref, c_ref, krout_ref, *maybe_ct_ref, q_transposed):
    tm = cq_ref.shape[0]
    lane = _lane_iota((tm, LANES))
    cos = cos_ref[...]
    sin = sin_ref[...]
    q = _dot(_rms(cq_ref[...], qn_ref[...]).astype(BF16), wuq_ref[...])
    n_nope = MLA_HEADS * MLA_NOPE
    rq = q[:, n_nope:n_nope + LANES]
    rq = rq * cos + pltpu.roll(rq, LANES // 2, 1) * jnp.where(lane < LANES // 2, -sin, sin)
    q_in = (jnp.concatenate([q[:, 0:n_nope], rq], axis=1) * MLA_SCALE).astype(BF16)
    wcat_t = wcat_ref[...]
    qcat_ref[...] = _dot_nt(wcat_t, q_in) if q_transposed else _dot_nt(q_in, wcat_t)
    c = _rms(ckv_ref[...], kvn_ref[...])
    c_ref[...] = c
    kr = kr_ref[...]
    swapped = jnp.where(lane < ROPE_HALF, pltpu.roll(kr, LANES - ROPE_HALF, 1), pltpu.roll(kr, ROPE_HALF, 1))
    kro = kr * cos + swapped * jnp.where(lane < ROPE_HALF, -sin, sin)
    kro = jnp.where(lane < MLA_ROPE, kro, 0.0)
    krout_ref[...] = kro[:, 0:MLA_ROPE]
    kcat_ref[...] = jnp.concatenate([c, kro], axis=1).astype(kcat_ref.dtype)
    for ct_ref in maybe_ct_ref:
        ct_ref[0] = _value_tile(c.T)


def _mla_prep(cq, ckv, kr, cos, sin, lp, tm, q_transposed):
    n = cq.shape[0]
    qw = MLA_HEADS * MLA_QW
    row = lambda i: (i, 0)
    fixed = lambda i: (0, 0)
    q_spec = pl.BlockSpec((qw, tm), lambda i: (0, i)) if q_transposed else pl.BlockSpec((tm, qw), row)
    rows = MLA_KVR + VALUE_EXTRA_ROWS
    ct_spec = [pl.BlockSpec((1, rows, tm), lambda i: (i, 0, 0))] if q_transposed else []
    ct_shape = [jax.ShapeDtypeStruct((n // tm, rows, tm), BF16)] if q_transposed else []
    return pl.pallas_call(
        functools.partial(_mla_prep_kernel, q_transposed=q_transposed),
        grid=(n // tm,),
        in_specs=[pl.BlockSpec((tm, MLA_QR), row), pl.BlockSpec((tm, LANES), row), pl.BlockSpec((tm, LANES), row),
                  pl.BlockSpec((tm, LANES), row), pl.BlockSpec((tm, LANES), row),
                  pl.BlockSpec((1, MLA_QR), fixed), pl.BlockSpec((1, LANES), fixed),
                  pl.BlockSpec(lp["mla_wuq"].shape, fixed), pl.BlockSpec(lp["mla_wcat_t"].shape, fixed)],
        out_specs=[q_spec, pl.BlockSpec((tm, MLA_QW), row),
                   pl.BlockSpec((tm, LANES), row), pl.BlockSpec((tm, MLA_ROPE), row)] + ct_spec,
        out_shape=[jax.ShapeDtypeStruct((qw, n) if q_transposed else (n, qw), F32),
                   jax.ShapeDtypeStruct((n, MLA_QW), BF16 if q_transposed else F32),
                   jax.ShapeDtypeStruct((n, LANES), F32), jax.ShapeDtypeStruct((n, MLA_ROPE), F32)] + ct_shape,
        compiler_params=_cparams(("arbitrary",)),
        name="mla_prep",
    )(cq, ckv, kr, cos, sin, lp["mla_q_norm"], lp["mla_kv_norm"], lp["mla_wuq"], lp["mla_wcat_t"])


def _mla_stack_q(qcat):
    return jnp.concatenate([qcat[:, h * MLA_QW:(h + 1) * MLA_QW] for h in range(MLA_HEADS)], axis=0).astype(BF16)


def _mla_finish(acc_ref, l_ref, wuv_ref, rows):
    y = jnp.zeros((rows, BR), F32)
    for h in range(MLA_HEADS):
        o_lat = acc_ref[h * rows:(h + 1) * rows, :] / l_ref[h * rows:(h + 1) * rows, :]
        y = y + _dot(o_lat.astype(BF16), wuv_ref[h])
    return y


def _mla_flash_kernel(q_ref, k_ref, ct_ref, wuvt_ref, o_ref, qs_ref, m_ref, acc_ref, *, tq, tk):
    q0 = pl.program_id(0) * tq
    qs_ref[...] = jnp.concatenate([q_ref[h * MLA_QW:(h + 1) * MLA_QW, :] for h in range(MLA_HEADS)],
                                  axis=1).astype(BF16)
    _flash_tiles(k_ref, ct_ref, qs_ref, m_ref, acc_ref, q0, tq, tk)
    y = jnp.zeros((BR, tq), F32)
    for h in range(MLA_HEADS):
        cols = slice(h * tq, (h + 1) * tq)
        o_lat = acc_ref[0:MLA_KVR, cols] / acc_ref[MLA_KVR:MLA_KVR + 1, cols]
        y = y + _dot(wuvt_ref[h], o_lat.astype(BF16))
    o_ref[...] = y


def _mla_flash(qcat_t, kcat, c_t, wuv_t, tq, tk):
    n = qcat_t.shape[1]
    assert (tk % tq == 0 or tq % tk == 0) and n % tk == 0 and n % tq == 0 and tq % FLASH_COLS == 0
    width = MLA_HEADS * tq
    return pl.pallas_call(
        functools.partial(_mla_flash_kernel, tq=tq, tk=tk),
        grid=(n // tq,),
        in_specs=[pl.BlockSpec((MLA_HEADS * MLA_QW, tq), lambda qi: (0, qi)),
                  _resident(kcat.shape),
                  _resident(c_t.shape),
                  _resident(wuv_t.shape)],
        out_specs=pl.BlockSpec((BR, tq), lambda qi: (0, qi)),
        out_shape=jax.ShapeDtypeStruct((BR, n), F32),
        scratch_shapes=[pltpu.VMEM((MLA_QW, width), BF16),
                        pltpu.VMEM((1, width), F32),
                        pltpu.VMEM((MLA_KVR + VALUE_EXTRA_ROWS, width), F32)],
        compiler_params=_cparams(("arbitrary",)),
        name="mla_flash",
    )(qcat_t, kcat, c_t, wuv_t)


def _mla_decode_kernel(pt_ref, q_ref, kn_ref, wuv_ref, cc_ref, rc_ref, o_ref,
                       cbuf, rbuf, sems, qs_ref, m_ref, l_ref, acc_ref, p_ref,
                       *, layer, pages_per_step, n_chunks, steps):
    G = pages_per_step
    R8 = SUBLANES
    c = pl.program_id(1)
    slot, start_next, drain = _stream_pages(pt_ref, (cc_ref, rc_ref), (cbuf, rbuf), sems, layer, G, n_chunks)

    @pl.when(c == 0)
    def _():
        qs_ref[...] = _mla_stack_q(q_ref[0])
        m_ref[...] = jnp.full(m_ref.shape, NEG, F32)
        l_ref[...] = jnp.zeros(l_ref.shape, F32)
        acc_ref[...] = jnp.zeros(acc_ref.shape, F32)

    qs = qs_ref[...]
    q_lat = qs[:, 0:MLA_KVR]
    q_rope = qs[:, MLA_KVR:MLA_KVR + MLA_ROPE]

    start_next()
    cb = cbuf[slot].astype(BF16)
    s = _dot_nt(q_lat, cb) + _dot(q_rope, rbuf[slot].astype(BF16))
    for h in range(MLA_HEADS):
        alpha = _online_update(s[h * R8:(h + 1) * R8], h, R8, m_ref, l_ref, p_ref)
        acc_ref[h * R8:(h + 1) * R8, :] = alpha * acc_ref[h * R8:(h + 1) * R8, :]
    acc_ref[...] = acc_ref[...] + _dot(p_ref[...].astype(BF16), cb)
    drain()

    @pl.when(c == n_chunks - 1)
    def _():
        kn = jnp.concatenate([kn_ref[0], jnp.zeros((PAGE - R8, MLA_QW), F32)], axis=0).astype(BF16)
        sn = _dot_nt(qs, kn)
        kt = _lane_iota((R8, PAGE))
        vis = (kt <= _row_iota((R8, PAGE))) & (kt < steps)
        for h in range(MLA_HEADS):
            alpha = _online_update(jnp.where(vis, sn[h * R8:(h + 1) * R8], NEG), h, R8, m_ref, l_ref, p_ref)
            acc_ref[h * R8:(h + 1) * R8, :] = alpha * acc_ref[h * R8:(h + 1) * R8, :]
        acc_ref[...] = acc_ref[...] + _dot(p_ref[:, 0:PAGE].astype(BF16), kn[:, 0:MLA_KVR])
        o_ref[0] = _mla_finish(acc_ref, l_ref, wuv_ref, R8)


def _mla_decode(page_table, qcat, kcat_new, wuv, cc, rc, layer, steps, pages_per_step):
    db, n_pages = page_table.shape
    G = pages_per_step
    n_chunks = n_pages // G
    rows = MLA_HEADS * SUBLANES
    seq3 = lambda b, c, pt: (b, 0, 0)
    grid_spec = pltpu.PrefetchScalarGridSpec(
        num_scalar_prefetch=1,
        grid=(db, n_chunks),
        in_specs=[pl.BlockSpec((1, SUBLANES, MLA_HEADS * MLA_QW), seq3),
                  pl.BlockSpec((1, SUBLANES, MLA_QW), seq3),
                  pl.BlockSpec(wuv.shape, lambda b, c, pt: (0, 0, 0)),
                  pl.BlockSpec(memory_space=pl.ANY),
                  pl.BlockSpec(memory_space=pl.ANY)],
        out_specs=pl.BlockSpec((1, SUBLANES, BR), seq3),
        scratch_shapes=[pltpu.VMEM((2, G * PAGE, MLA_KVR), F32),
                        pltpu.VMEM((2, MLA_ROPE, G * PAGE), F32),
                        pltpu.SemaphoreType.DMA((2, 2)),
                        pltpu.VMEM((rows, MLA_QW), BF16),
                        pltpu.VMEM((rows, LANES), F32),
                        pltpu.VMEM((rows, LANES), F32),
                        pltpu.VMEM((rows, LANES), F32),
                        pltpu.VMEM((rows, G * PAGE), F32)])
    return pl.pallas_call(
        functools.partial(_mla_decode_kernel, layer=layer, pages_per_step=G, n_chunks=n_chunks, steps=steps),
        grid_spec=grid_spec,
        out_shape=jax.ShapeDtypeStruct((db, SUBLANES, BR), F32),
        compiler_params=_cparams(("arbitrary", "arbitrary")),
        name="mla_decode",
    )(page_table, qcat, kcat_new, wuv, cc, rc)


def _mem_attn_kernel(q_ref, mk_ref, mv_ref, o_ref, *, key_minor):
    rows = q_ref.shape[1]
    lane_head = _lane_iota((rows, BR)) // MEM_DIM
    for j in range(q_ref.shape[0]):
        jk = j if mk_ref.shape[0] > 1 else 0
        q = q_ref[j] * (MEM_DIM ** -0.5)
        qs = jnp.concatenate([jnp.where(lane_head == h, q, 0.0) for h in range(MEM_HEADS)], axis=0).astype(BF16)
        mk = mk_ref[jk].astype(BF16)
        mv = mv_ref[jk].astype(BF16)
        s = _dot(qs, mk) if key_minor else _dot_nt(qs, mk)
        p = jnp.exp(s - jnp.max(s, axis=1, keepdims=True))
        p = (p / jnp.sum(p, axis=1, keepdims=True)).astype(BF16)
        r = _dot_nt(p, mv) if key_minor else _dot(p, mv)
        y = jnp.zeros((rows, BR), F32)
        for h in range(MEM_HEADS):
            y = y + jnp.where(lane_head == h, r[h * rows:(h + 1) * rows], 0.0)
        o_ref[j] = y


def _mem_attn(q3, mk3, mv3, key_minor):
    b, rows, _ = q3.shape
    per_seq = mk3.shape[0] == b and b > 1
    sb = math.gcd(b, SCAN_SEQS) if per_seq else 1
    kv_map = (lambda i: (i, 0, 0)) if per_seq else (lambda i: (0, 0, 0))
    return pl.pallas_call(
        functools.partial(_mem_attn_kernel, key_minor=key_minor),
        grid=(b // sb,),
        in_specs=[pl.BlockSpec((sb, rows, BR), lambda i: (i, 0, 0)),
                  pl.BlockSpec((sb,) + mk3.shape[1:], kv_map),
                  pl.BlockSpec((sb,) + mv3.shape[1:], kv_map)],
        out_specs=pl.BlockSpec((sb, rows, BR), lambda i: (i, 0, 0)),
        out_shape=jax.ShapeDtypeStruct((b, rows, BR), F32),
        compiler_params=_cparams(("arbitrary",)),
        name="mem_attn",
    )(q3, mk3, mv3)


def _matmul_kernel(x_ref, w_ref, o_ref):
    o_ref[...] = _dot(x_ref[...].astype(BF16), w_ref[...])


def _matmul(x, w):
    m, k = x.shape
    n = w.shape[1]
    return pl.pallas_call(
        _matmul_kernel,
        grid=(1,),
        in_specs=[pl.BlockSpec((m, k), lambda i: (0, 0)), pl.BlockSpec((k, n), lambda i: (0, 0))],
        out_specs=pl.BlockSpec((m, n), lambda i: (0, 0)),
        out_shape=jax.ShapeDtypeStruct((m, n), F32),
        compiler_params=_cparams(("arbitrary",)),
        name="mem_proj",
    )(x, w)


def _merge_kernel(x_ref, y0, y1, y2, y3, y4, gpre_ref, gpost_ref, wg_ref, wb_ref, wo_ref, o_ref, *, transposed):
    x = x_ref[...]
    d = x.shape[1]
    hn = _rms(x, gpre_ref[...]).astype(BF16)
    m = jnp.zeros(x.shape, F32)
    for i, y_ref in enumerate((y0, y1, y2, y3, y4)):
        gate = _sigmoid(_dot_nt(hn, wg_ref[i * d:(i + 1) * d, :]))
        y = y_ref[...].astype(BF16)
        m = m + gate * (_dot_tn(y, wb_ref[i]) if i in transposed else _dot(y, wb_ref[i]))
    o_ref[...] = x + _rms(_dot(m.astype(BF16), wo_ref[...]), gpost_ref[...])


def _merge_out(x, ys, lp, tm, transposed=()):
    n, d = x.shape
    row = lambda i: (i, 0)
    fixed = lambda i: (0, 0)
    y_specs = [pl.BlockSpec((BR, tm), lambda i: (0, i)) if j in transposed else pl.BlockSpec((tm, BR), row)
               for j in range(N_BRANCH)]
    return pl.pallas_call(
        functools.partial(_merge_kernel, transposed=transposed),
        grid=(n // tm,),
        in_specs=[pl.BlockSpec((tm, d), row)] + y_specs + [
            pl.BlockSpec((1, d), fixed), pl.BlockSpec((1, d), fixed),
            _resident(lp["w_gate"].shape), _resident(lp["w_branch"].shape), _resident(lp["w_out"].shape)],
        out_specs=pl.BlockSpec((tm, d), row),
        out_shape=jax.ShapeDtypeStruct((n, d), F32),
        compiler_params=_cparams(("arbitrary",)),
        name="merge_out",
    )(x, *ys, lp["g_pre_mix"], lp["g_post_mix"], lp["w_gate"], lp["w_branch"], lp["w_out"])


FFN_CHUNK = 2 * LANES


def _ffn_kernel(x_ref, st_ref, gpre_ref, gpost_ref, wi_ref, cw_ref, cb_ref, wo_ref, o_ref, ns_ref,
                ext_ref, acc_ref, *, shift, pad, width):
    tm = x_ref.shape[0]

    @pl.when(pl.program_id(0) == 0)
    def _():
        ext_ref[0:pad, :] = st_ref[...]

    x = x_ref[...]
    hn = _rms(x, gpre_ref[...]).astype(BF16)
    for c0 in range(0, width, FFN_CHUNK):
        a = _dot(hn, wi_ref[:, c0:c0 + FFN_CHUNK])
        gate = _dot(hn, wi_ref[:, width + c0:width + c0 + FFN_CHUNK])
        ext_ref[pad:pad + tm, c0:c0 + FFN_CHUNK] = a
        conv = cb_ref[:, c0:c0 + FFN_CHUNK] + cw_ref[FFN_CONV - 1:FFN_CONV, c0:c0 + FFN_CHUNK] * a
        for j in range(FFN_CONV - 1):
            off = pad - (FFN_CONV - 1 - j) * shift
            conv = conv + cw_ref[j:j + 1, c0:c0 + FFN_CHUNK] * ext_ref[off:off + tm, c0:c0 + FFN_CHUNK]
        act = (_gelu_tanh(conv) * gate).astype(BF16)
        contrib = _dot(act, wo_ref[c0:c0 + FFN_CHUNK, :])
        if c0 == 0:
            acc_ref[...] = contrib
        else:
            acc_ref[...] = acc_ref[...] + contrib
    o_ref[...] = x + _rms(acc_ref[...], gpost_ref[...])
    tail = ext_ref[tm:tm + pad, :]
    ns_ref[...] = tail
    ext_ref[0:pad, :] = tail


def _ffn(x, state, lp, shift, tm):
    n, d = x.shape
    pad, width = state.shape
    assert tm >= pad and pad >= (FFN_CONV - 1) * shift and width % FFN_CHUNK == 0
    row = lambda i: (i, 0)
    fixed = lambda i: (0, 0)
    return pl.pallas_call(
        functools.partial(_ffn_kernel, shift=shift, pad=pad, width=width),
        grid=(n // tm,),
        in_specs=[pl.BlockSpec((tm, d), row), pl.BlockSpec((pad, width), fixed),
                  pl.BlockSpec((1, d), fixed), pl.BlockSpec((1, d), fixed),
                  _resident(lp["w_ffn_in"].shape),
                  pl.BlockSpec((FFN_CONV, width), fixed), pl.BlockSpec((1, width), fixed),
                  _resident(lp["w_ffn_out"].shape)],
        out_specs=[pl.BlockSpec((tm, d), row), pl.BlockSpec((pad, width), fixed)],
        out_shape=[jax.ShapeDtypeStruct((n, d), F32), jax.ShapeDtypeStruct((pad, width), F32)],
        scratch_shapes=[pltpu.VMEM((pad + tm, width), F32), pltpu.VMEM((tm, d), F32)],
        compiler_params=_cparams(("arbitrary",)),
        name="conv_ffn",
    )(x, state, lp["g_pre_ffn"], lp["g_post_ffn"], lp["w_ffn_in"], lp["ffn_conv_w"], lp["ffn_conv_b"],
      lp["w_ffn_out"])


def _rope_tables(pos):
    freqs = jnp.power(ROPE_THETA, -jnp.arange(ROPE_HALF, dtype=F32) / ROPE_HALF)
    ang = pos.astype(F32)[:, None] * freqs
    reps = LANES // ROPE_HALF
    return jnp.tile(jnp.cos(ang), (1, reps)), jnp.tile(jnp.sin(ang), (1, reps))


def _prep_layer(li, P):
    d_model = P["w_in"].shape[1]
    w_in_t = jnp.transpose(P["w_in"], (2, 0, 1))[:, li, :]
    bounds = np.cumsum((0,) + IN_WIDTHS)
    seg = [w_in_t[bounds[i]:bounds[i + 1]] for i in range(len(IN_WIDTHS))]
    grp = DIFF_HEADS // DIFF_KV
    seg[3] = seg[3].reshape(DIFF_KV, grp, 2 * DIFF_DIM, d_model).swapaxes(0, 1).reshape(-1, d_model)
    seg = [jnp.pad(s, ((0, wd - s.shape[0]), (0, 0))) for s, wd in zip(seg, PROJ_WIDTHS)]
    lp = {"w_cat": jnp.concatenate(seg, axis=0).astype(BF16),
          "w_gate": w_in_t[bounds[-1]:].astype(BF16)}
    for name in ("g_pre_mix", "g_post_mix", "g_pre_ffn", "g_post_ffn", "ssd_conv_b", "ssd_norm", "gla_bg",
                 "mla_q_norm", "mla_kv_norm", "ffn_conv_b"):
        lp[name] = P[name][li][None]
    lp["ssd_conv_w"] = P["ssd_conv_w"][li]
    lp["ffn_conv_w"] = P["ffn_conv_w"][li]
    prm = jnp.zeros((SUBLANES, LANES), F32)
    lp["ssd_prm"] = prm.at[0, :SSD_HEADS].set(P["ssd_dt_bias"][li]).at[1, :SSD_HEADS].set(-jnp.exp(P["ssd_a_log"][li]))
    lp["ssd_dskip"] = jnp.repeat(P["ssd_d"][li], SSD_HEAD_DIM)[None]
    lam_init = 0.8 - 0.6 * math.exp(-0.3 * li)
    lam = (jnp.exp(jnp.sum(P["diff_lq1"][li] * P["diff_lk1"][li]))
           - jnp.exp(jnp.sum(P["diff_lq2"][li] * P["diff_lk2"][li])) + lam_init)
    lp["diff_prm"] = prm.at[0].set(jnp.tile(P["diff_norm"][li], DIFF_KV) * (1.0 - lam_init)).at[1].set(lam)
    lp["gla_wg2"] = jnp.zeros((LANES, LANES), F32).at[:GLA_RANK].set(P["gla_wg2"][li]).astype(BF16)
    lp["gla_norm"] = jnp.tile(P["gla_norm"][li], GLA_HEADS)[None]
    wuq = P["mla_wuq"][li].reshape(MLA_QR, MLA_HEADS, MLA_NOPE + MLA_ROPE)
    lp["mla_wuq"] = jnp.concatenate(
        [wuq[:, :, :MLA_NOPE].reshape(MLA_QR, -1),
         wuq[:, :, MLA_NOPE:MLA_NOPE + ROPE_HALF].reshape(MLA_QR, -1),
         wuq[:, :, MLA_NOPE + ROPE_HALF:].reshape(MLA_QR, -1)], axis=1).astype(BF16)
    n_nope = MLA_HEADS * MLA_NOPE
    wcat = jnp.zeros((n_nope + LANES, MLA_HEADS * MLA_QW), F32)
    wuv = jnp.zeros((MLA_HEADS, MLA_KVR, BR), F32)
    eye = jnp.eye(ROPE_HALF, dtype=F32)
    for h in range(MLA_HEADS):
        wcat = wcat.at[h * MLA_NOPE:(h + 1) * MLA_NOPE, h * MLA_QW:h * MLA_QW + MLA_KVR].set(P["mla_wuk"][li][:, h, :].T)
        for half in range(2):
            r0 = n_nope + half * (LANES // 2) + h * ROPE_HALF
            c0 = h * MLA_QW + MLA_KVR + half * ROPE_HALF
            wcat = wcat.at[r0:r0 + ROPE_HALF, c0:c0 + ROPE_HALF].set(eye)
        wuv = wuv.at[h, :, h * MLA_V:(h + 1) * MLA_V].set(P["mla_wuv"][li][:, h, :])
    lp["mla_wcat_t"] = wcat.T.astype(BF16)
    lp["mla_wuv"] = wuv.astype(BF16)
    lp["mla_wuv_t"] = wuv.transpose(0, 2, 1).astype(BF16)
    lp["w_mem_kv"] = jnp.concatenate([P["w_mem_k"][li], P["w_mem_v"][li]], axis=1).astype(BF16)
    wb = P["w_branch"][li]
    wb_diff = wb[1].reshape(DIFF_KV, grp, 2 * DIFF_DIM, d_model).swapaxes(0, 1).reshape(BR, d_model)
    lp["w_branch"] = wb.at[1].set(wb_diff).astype(BF16)
    lp["w_out"] = P["w_out"][li].astype(BF16)
    lp["w_ffn_in"] = P["w_ffn_in"][li].astype(BF16)
    lp["w_ffn_out"] = P["w_ffn_out"][li].astype(BF16)
    return lp


def _row_tile(n, want):
    return want if n % want == 0 else n


def _layer(x, lp, li, db, steps, st, past):
    n = x.shape[0]
    tm = _row_tile(n, 256)
    prompt = past is None
    tk = _row_tile(n, 512)
    proj = _proj_in(x, lp["g_pre_mix"], lp["w_cat"], tk, transposed=(3,) if prompt else (), attn_feed=prompt)
    (z, xbc, dt, dq, dk, dv, gq, gk, gv, gg, gr, cq, ckv, kr, mq) = proj[:len(PROJ_WIDTHS)]
    y_ssd, ssd_conv, ssd_h = _ssd_branch(z, xbc, dt, st["ssd_conv"], st["ssd_h"], lp, db, steps)
    y_gla, gla_s = _gla_branch(gq, gk, gv, gg, gr, st["gla_s"], lp, db, steps)
    past_len = 0 if prompt else past["page_table"].shape[1] * PAGE
    pos = jnp.repeat(past_len + jnp.arange(steps, dtype=jnp.int32), db)
    cos, sin = _rope_tables(pos)
    prep = _mla_prep(cq, ckv, kr, cos, sin, lp, tk if prompt else tm, q_transposed=prompt)
    qcat, kcat, c_rows, kr_rows = prep[:4]
    if prompt:
        tq_diff, tq_mla = _row_tile(n, 512), _row_tile(n, 1024)
        gcol = jnp.broadcast_to(lp["diff_prm"][0][:, None], (LANES, tq_diff))
        lam_row = jnp.broadcast_to(lp["diff_prm"][1, 0], (1, tq_diff))
        y_diff = _diff_flash(dq, proj[-2], proj[-1], lam_row, gcol, tq_diff, tk)
        y_mla = _mla_flash(qcat, kcat, prep[4], lp["mla_wuv_t"], tq_mla, tk)
        y_mem = _mem_attn(mq.reshape(n // tm, tm, BR), st["mem_k"], st["mem_v"], key_minor=False).reshape(n, BR)
    else:
        seq = lambda a: _to_seq_major(a, db, steps, SUBLANES)
        pt = past["page_table"]
        g_pages = math.gcd(pt.shape[1], 64)
        y_diff = _to_time_major(_diff_decode(pt, seq(dq), seq(dk), seq(dv), lp["diff_prm"], past["diff_k"],
                                             past["diff_v"], li, steps, g_pages), steps)
        y_mla = _to_time_major(_mla_decode(pt, seq(qcat), seq(kcat), lp["mla_wuv"], past["mla_ckv"],
                                           past["mla_kr"], li, steps, g_pages), steps)
        y_mem = _to_time_major(_mem_attn(seq(mq), st["mem_k"], st["mem_v"], key_minor=True), steps)
    x = _merge_out(x, (y_ssd, y_diff, y_gla, y_mla, y_mem), lp, tm, transposed=(1, 3) if prompt else ())
    x, ffn_conv = _ffn(x, st["ffn_conv"], lp, shift=db, tm=_row_tile(n, 512) if db == 1 else n)
    return x, (dk, dv, c_rows, kr_rows, ssd_h, ssd_conv, gla_s, ffn_conv)


def kernel(x_prompt, x_sample, cache_diff_k, cache_diff_v, cache_mla_ckv, cache_mla_krope, cache_mem_k, cache_mem_v, state_ssd, state_ssd_conv, state_gla, state_ffn_conv, page_table, mem_prompt, w_in, ssd_conv_w, ssd_conv_b, ssd_dt_bias, ssd_a_log, ssd_d, ssd_norm, diff_lq1, diff_lk1, diff_lq2, diff_lk2, diff_norm, gla_wg2, gla_bg, gla_norm, mla_q_norm, mla_kv_norm, mla_wuq, mla_wuk, mla_wuv, w_mem_k, w_mem_v, w_branch, w_out, g_pre_mix, g_post_mix, g_pre_ffn, g_post_ffn, w_ffn_in, ffn_conv_w, ffn_conv_b, w_ffn_out):
    P = dict(w_in=w_in, ssd_conv_w=ssd_conv_w, ssd_conv_b=ssd_conv_b, ssd_dt_bias=ssd_dt_bias,
             ssd_a_log=ssd_a_log, ssd_d=ssd_d, ssd_norm=ssd_norm, diff_lq1=diff_lq1, diff_lk1=diff_lk1,
             diff_lq2=diff_lq2, diff_lk2=diff_lk2, diff_norm=diff_norm, gla_wg2=gla_wg2, gla_bg=gla_bg,
             gla_norm=gla_norm, mla_q_norm=mla_q_norm, mla_kv_norm=mla_kv_norm, mla_wuq=mla_wuq,
             mla_wuk=mla_wuk, mla_wuv=mla_wuv, w_mem_k=w_mem_k, w_mem_v=w_mem_v, w_branch=w_branch,
             w_out=w_out, g_pre_mix=g_pre_mix, g_post_mix=g_post_mix, g_pre_ffn=g_pre_ffn,
             g_post_ffn=g_post_ffn, w_ffn_in=w_ffn_in, ffn_conv_w=ffn_conv_w, ffn_conv_b=ffn_conv_b,
             w_ffn_out=w_ffn_out)
    depth = w_in.shape[0]
    pb, seq_len, d_model = x_prompt.shape
    assert pb == 1
    db, steps, _ = x_sample.shape
    mem_len = mem_prompt.shape[1]
    ffn_dim = ffn_conv_w.shape[2]
    past = {"page_table": page_table,
            "diff_k": jnp.transpose(cache_diff_k, (0, 1, 3, 4, 2)).reshape(depth, -1, LANES, PAGE),
            "diff_v": jnp.transpose(cache_diff_v, (0, 1, 3, 4, 2)).reshape(depth, -1, LANES, PAGE),
            "mla_ckv": cache_mla_ckv,
            "mla_kr": jnp.transpose(cache_mla_krope, (0, 1, 3, 2))}
    mem_kt = jnp.transpose(cache_mem_k, (0, 1, 3, 4, 2)).reshape(depth, db, BR, mem_len)
    mem_vt = jnp.transpose(cache_mem_v, (0, 1, 3, 4, 2)).reshape(depth, db, BR, mem_len)
    xp = x_prompt[0]
    xs = x_sample.transpose(1, 0, 2).reshape(steps * db, d_model)
    new_p, new_s, mem_kp, mem_vp = [], [], [], []
    for li in range(depth):
        lp = _prep_layer(li, P)
        mkv = _matmul(mem_prompt[0], lp["w_mem_kv"])
        mk, mv = mkv[:, :BR], mkv[:, BR:]
        mem_kp.append(mk.reshape(1, mem_len, MEM_HEADS, MEM_DIM))
        mem_vp.append(mv.reshape(1, mem_len, MEM_HEADS, MEM_DIM))
        st_p = {"ssd_conv": jnp.zeros((SUBLANES, SSD_CONV_DIM), F32),
                "ssd_h": jnp.zeros((1, SSD_HEADS, SSD_HEAD_DIM, SSD_STATE), F32),
                "gla_s": jnp.zeros((1, GLA_HEADS, GLA_DK, GLA_DV), F32),
                "ffn_conv": jnp.zeros((SUBLANES, ffn_dim), F32),
                "mem_k": mk[None], "mem_v": mv[None]}
        xp, o = _layer(xp, lp, li, 1, seq_len, st_p, None)
        dk, dv, c_rows, kr_rows, ssd_h, ssd_conv, gla_s, ffn_conv = o
        new_p.append((dk.reshape(1, seq_len, DIFF_KV, 2 * DIFF_DIM), dv.reshape(1, seq_len, DIFF_KV, 2 * DIFF_DIM),
                      c_rows[None], kr_rows[None], ssd_h, ssd_conv[None, SUBLANES - (SSD_CONV - 1):],
                      gla_s, ffn_conv[None, SUBLANES - (FFN_CONV - 1):]))
        st_s = {"ssd_conv": state_ssd_conv[li].transpose(1, 0, 2).reshape((SSD_CONV - 1) * db, SSD_CONV_DIM),
                "ssd_h": state_ssd[li],
                "gla_s": state_gla[li],
                "ffn_conv": state_ffn_conv[li].transpose(1, 0, 2).reshape((FFN_CONV - 1) * db, ffn_dim),
                "mem_k": mem_kt[li], "mem_v": mem_vt[li]}
        xs, o = _layer(xs, lp, li, db, steps, st_s, past)
        dk, dv, c_rows, kr_rows, ssd_h, ssd_conv, gla_s, ffn_conv = o
        bm = lambda a: a.reshape(-1, db, a.shape[-1]).transpose(1, 0, 2)
        new_s.append((bm(dk).reshape(db, steps, DIFF_KV, 2 * DIFF_DIM), bm(dv).reshape(db, steps, DIFF_KV, 2 * DIFF_DIM),
                      bm(c_rows), bm(kr_rows), ssd_h, bm(ssd_conv), gla_s, bm(ffn_conv)))
    stk = lambda outs, j: jnp.stack([o[j] for o in outs])
    yp = xp[None]
    ys = xs.reshape(steps, db, d_model).transpose(1, 0, 2)
    return (yp, ys,
            stk(new_p, 0), stk(new_p, 1), stk(new_p, 2), stk(new_p, 3),
            jnp.stack(mem_kp), jnp.stack(mem_vp),
            stk(new_p, 4), stk(new_p, 5), stk(new_p, 6), stk(new_p, 7),
            stk(new_s, 0), stk(new_s, 1), stk(new_s, 2), stk(new_s, 3),
            stk(new_s, 4), stk(new_s, 5), stk(new_s, 6), stk(new_s, 7))
```
